```python
import math, functools
import jax, jax.numpy as jnp
from jax import lax
import numpy as np

D_MODEL = 2048
BATCH = 2
SEQ = 8192
DEPTH = 1

GRID_W = 64
CTX_LEN = 256
NORM_EPS = 1e-6
D_INNER = 2 * D_MODEL
SSM_HEAD_DIM = 64
N_SSM_HEADS = D_INNER // SSM_HEAD_DIM
N_GROUPS = 8
HEADS_PER_GROUP = N_SSM_HEADS // N_GROUPS
D_STATE = 128
BC_W = N_GROUPS * D_STATE
XBC_W = D_INNER + 2 * BC_W
CONV_K = 5
CONV_PAD = CONV_K // 2
SSD_CHUNK = 128
DT_MIN = 1e-3
DT_MAX = 1e-1
HEAD_DIM = 128
N_Q_HEADS = D_MODEL // HEAD_DIM
N_KV_HEADS = 4
Q_PER_KV = N_Q_HEADS // N_KV_HEADS
Q_W = N_Q_HEADS * HEAD_DIM
KV_W = N_KV_HEADS * HEAD_DIM
WINDOW = 128
ATTN_BLOCK = 128
ATTN_SCALE = HEAD_DIM ** -0.5
ROPE_FREQS = HEAD_DIM // 4
ROPE_BASE = 10000.0
N_EXPERTS = 32
TOP_K = 4
D_FF = D_MODEL
SWIGLU_LIMIT = 7.0
SWIGLU_ALPHA = 1.702
MOE_BLOCK = 256
PROJ_SIZES = (XBC_W, D_INNER, N_SSM_HEADS, N_SSM_HEADS, Q_W, KV_W, KV_W, D_MODEL, D_MODEL)
PROJ_SPLITS = tuple(int(s) for s in np.cumsum(PROJ_SIZES)[:-1])
PROJ_W = sum(PROJ_SIZES)

kernel_name = 'hybrid_ssd_swa_moe_dit_block'


def rms_norm(x, w):
    xf = x.astype(jnp.float32)
    y = xf * lax.rsqrt(jnp.mean(xf * xf, axis=-1, keepdims=True) + NORM_EPS)
    return y.astype(x.dtype) * w


def modulate(h, shift, scale):
    return h * (1 + scale) + shift


def axial_rope_tables(n_rows):
    inv_freq = ROPE_BASE ** (-jnp.arange(ROPE_FREQS, dtype=jnp.float32) / ROPE_FREQS)
    row = jnp.repeat(jnp.arange(n_rows, dtype=jnp.float32), GRID_W)
    col = jnp.tile(jnp.arange(GRID_W, dtype=jnp.float32), n_rows)
    ar = row[:, None] * inv_freq
    ac = col[:, None] * inv_freq
    ang = jnp.concatenate([ar, ar, ac, ac], axis=-1)
    return jnp.cos(ang), jnp.sin(ang)


def apply_rope(x, cos, sin):
    bshape = (cos.shape[0],) + (1,) * (x.ndim - 3) + (cos.shape[1],)
    xa = x.reshape(x.shape[:-1] + (2, 2, ROPE_FREQS))
    rot = jnp.stack([-xa[..., 1, :], xa[..., 0, :]], axis=-2).reshape(x.shape)
    return x * cos.reshape(bshape).astype(x.dtype) + rot * sin.reshape(bshape).astype(x.dtype)


def depthwise_conv(x, w, b):
    y = lax.conv_general_dilated(x, w[:, None, :], window_strides=(1,),
                                 padding=((CONV_PAD, CONV_PAD),),
                                 dimension_numbers=('NWC', 'WIO', 'NWC'),
                                 feature_group_count=x.shape[-1])
    return y + b


def mixer_inputs(h, w_in, conv_w, conv_b, dt_bias_f, dt_bias_b, q_norm_w, k_norm_w):
    bsz, n = h.shape[:2]
    xbc, z, dt_f, dt_b, q, k, v, g_ssm, g_attn = jnp.split(h @ w_in, PROJ_SPLITS, axis=-1)
    xbc = jax.nn.silu(depthwise_conv(xbc, conv_w, conv_b))
    xs, bm, cm = jnp.split(xbc, (D_INNER, D_INNER + BC_W), axis=-1)
    grp = (bsz, n, N_GROUPS, HEADS_PER_GROUP)
    return dict(
        xs=xs.reshape(grp + (SSM_HEAD_DIM,)),
        bm=bm.reshape(bsz, n, N_GROUPS, D_STATE),
        cm=cm.reshape(bsz, n, N_GROUPS, D_STATE),
        dt_f=jax.nn.softplus(dt_f + dt_bias_f).reshape(grp),
        dt_b=jax.nn.softplus(dt_b + dt_bias_b).reshape(grp),
        z=z,
        q=rms_norm(q.reshape(bsz, n, N_KV_HEADS, Q_PER_KV, HEAD_DIM), q_norm_w),
        k=rms_norm(k.reshape(bsz, n, N_KV_HEADS, HEAD_DIM), k_norm_w),
        v=v.reshape(bsz, n, N_KV_HEADS, HEAD_DIM),
        g_ssm=g_ssm, g_attn=g_attn)


def ssd_chunked(xs, dt, a_log, bm, cm, h0):
    bsz, n = xs.shape[:2]
    nc = n // SSD_CHUNK
    A = -jnp.exp(a_log.astype(jnp.float32)).reshape(N_GROUPS, HEADS_PER_GROUP)
    acum = jnp.cumsum((dt.astype(jnp.float32) * A).reshape(bsz, nc, SSD_CHUNK, N_GROUPS, HEADS_PER_GROUP), axis=2)
    xc = xs.reshape(bsz, nc, SSD_CHUNK, N_GROUPS, HEADS_PER_GROUP, SSM_HEAD_DIM)
    dtc = dt.reshape(bsz, nc, SSD_CHUNK, N_GROUPS, HEADS_PER_GROUP)
    bc = bm.reshape(bsz, nc, SSD_CHUNK, N_GROUPS, D_STATE)
    cc = cm.reshape(bsz, nc, SSD_CHUNK, N_GROUPS, D_STATE)
    lower = jnp.tril(jnp.ones((SSD_CHUNK, SSD_CHUNK), dtype=bool))[None, None, :, :, None, None]
    seg = acum[:, :, :, None] - acum[:, :, None, :]
    decay = jnp.exp(jnp.where(lower, seg, -jnp.inf)).astype(xs.dtype)
    cb = jnp.einsum('bctgn,bcsgn->bctsg', cc, bc)
    y = jnp.einsum('bctsgr,bcsgrp->bctgrp', cb[..., None] * decay * dtc[:, :, None], xc)
    to_end = jnp.exp(acum[:, :, -1:] - acum).astype(xs.dtype) * dtc
    states = jnp.einsum('bcsgn,bcsgr,bcsgrp->bcgrpn', bc, to_end, xc)
    chunk_decay = jnp.exp(acum[:, :, -1]).astype(xs.dtype)

    def carry(h, inp):
        s, d = inp
        return d[..., None, None] * h + s, h

    _, h_start = lax.scan(carry, h0, (jnp.moveaxis(states, 1, 0), jnp.moveaxis(chunk_decay, 1, 0)))
    h_start = jnp.moveaxis(h_start, 0, 1)
    y = y + jnp.einsum('bctgn,bctgr,bcgrpn->bctgrp', cc, jnp.exp(acum).astype(xs.dtype), h_start)
    return y.reshape(xs.shape)


def ssd_final_state(xs, dt, a_log, bm):
    A = -jnp.exp(a_log.astype(jnp.float32)).reshape(N_GROUPS, HEADS_PER_GROUP)
    acum = jnp.cumsum(dt.astype(jnp.float32) * A, axis=1)
    w = jnp.exp(acum[:, -1:] - acum).astype(xs.dtype) * dt
    return jnp.einsum('blgn,blgr,blgrp->bgrpn', bm, w, xs)


def bidir_ssd(m, a_log_f, a_log_b, d_skip, h_f, h_b):
    fl = lambda t: jnp.flip(t, axis=1)
    y = ssd_chunked(m['xs'], m['dt_f'], a_log_f, m['bm'], m['cm'], h_f)
    y = y + fl(ssd_chunked(fl(m['xs']), fl(m['dt_b']), a_log_b, fl(m['bm']), fl(m['cm']), h_b))
    return y + d_skip.reshape(N_GROUPS, HEADS_PER_GROUP, 1) * m['xs']


def windowed_attention(q, k, v, k_ctx, v_ctx, sink):
    bsz, n = q.shape[:2]
    nb = n // ATTN_BLOCK
    qb = (q * ATTN_SCALE).reshape(bsz, nb, ATTN_BLOCK, N_KV_HEADS, Q_PER_KV, HEAD_DIM)

    def band(t):
        tp = jnp.pad(t, ((0, 0), (ATTN_BLOCK, ATTN_BLOCK), (0, 0), (0, 0)))
        tp = tp.reshape(bsz, nb + 2, ATTN_BLOCK, N_KV_HEADS, HEAD_DIM)
        return jnp.concatenate([tp[:, :nb], tp[:, 1:nb + 1], tp[:, 2:]], axis=2)

    kb, vb = band(k), band(v)
    s_lat = jnp.einsum('bnqhgd,bnkhd->bnhgqk', qb, kb).astype(jnp.float32)
    blk = jnp.arange(nb)[:, None] * ATTN_BLOCK
    q_pos = blk + jnp.arange(ATTN_BLOCK)[None]
    k_pos = blk + jnp.arange(3 * ATTN_BLOCK)[None] - ATTN_BLOCK
    rel = k_pos[:, None, :] - q_pos[:, :, None]
    valid = (jnp.abs(rel) <= WINDOW) & (k_pos[:, None, :] >= 0) & (k_pos[:, None, :] < n)
    s_lat = jnp.where(valid[None, :, None, None], s_lat, -jnp.inf)
    s_ctx = jnp.einsum('bnqhgd,bkhd->bnhgqk', qb, k_ctx).astype(jnp.float32)
    sk = sink.astype(jnp.float32).reshape(N_KV_HEADS, Q_PER_KV)[None, None, :, :, None, None]
    mx = jnp.maximum(jnp.maximum(s_lat.max(-1, keepdims=True), s_ctx.max(-1, keepdims=True)), sk)
    p_lat = jnp.exp(s_lat - mx)
    p_ctx = jnp.exp(s_ctx - mx)
    inv = 1.0 / (p_lat.sum(-1, keepdims=True) + p_ctx.sum(-1, keepdims=True) + jnp.exp(sk - mx))
    o = jnp.einsum('bnhgqk,bnkhd->bnqhgd', (p_lat * inv).astype(v.dtype), vb)
    o = o + jnp.einsum('bnhgqk,bkhd->bnqhgd', (p_ctx * inv).astype(v.dtype), v_ctx)
    return o.reshape(bsz, n, Q_W)


def context_attention(q, k, v, sink):
    bsz, n = q.shape[:2]
    s = jnp.einsum('bqhgd,bkhd->bhgqk', q * ATTN_SCALE, k).astype(jnp.float32)
    sk = sink.astype(jnp.float32).reshape(N_KV_HEADS, Q_PER_KV)[None, :, :, None, None]
    mx = jnp.maximum(s.max(-1, keepdims=True), sk)
    p = jnp.exp(s - mx)
    p = p / (p.sum(-1, keepdims=True) + jnp.exp(sk - mx))
    return jnp.einsum('bhgqk,bkhd->bqhgd', p.astype(v.dtype), v).reshape(bsz, n, Q_W)


def merge_branches(m, y_ssm, y_attn, ssm_norm_w, w_ssm_out, w_attn_out, w_o):
    bsz, n = m['z'].shape[:2]
    yg = y_ssm.reshape(bsz, n, D_INNER) * jax.nn.silu(m['z'])
    yg = rms_norm(yg.reshape(bsz, n, N_GROUPS, D_INNER // N_GROUPS),
                  ssm_norm_w.reshape(N_GROUPS, D_INNER // N_GROUPS)).reshape(bsz, n, D_INNER)
    merged = jax.nn.sigmoid(m['g_ssm']) * (yg @ w_ssm_out) + jax.nn.sigmoid(m['g_attn']) * (y_attn @ w_attn_out)
    return merged @ w_o


def moe_ffn(h, router_w, router_b, w_gate_up, b_gate_up, w_down, b_down):
    shape = h.shape
    h = h.reshape(-1, shape[-1])
    n_tok, d = h.shape
    logits = (h @ router_w + router_b).astype(jnp.float32)
    top_logits, top_idx = lax.top_k(logits, TOP_K)
    top_w = jax.nn.softmax(top_logits, axis=-1).astype(h.dtype)
    n_assign = n_tok * TOP_K
    flat_e = top_idx.reshape(-1)
    order = jnp.argsort(flat_e)
    e_sorted = flat_e[order]
    counts = jnp.zeros((N_EXPERTS,), jnp.int32).at[flat_e].add(1)
    padded = (counts + MOE_BLOCK - 1) // MOE_BLOCK * MOE_BLOCK
    pad_end = jnp.cumsum(padded)
    pad_start = pad_end - padded
    start = jnp.cumsum(counts) - counts
    dest = pad_start[e_sorted] + jnp.arange(n_assign, dtype=jnp.int32) - start[e_sorted]
    n_blocks = -(-(n_assign + N_EXPERTS * (MOE_BLOCK - 1)) // MOE_BLOCK)
    cap = n_blocks * MOE_BLOCK
    slot_tok = jnp.full((cap,), n_tok, jnp.int32).at[dest].set((order // TOP_K).astype(jnp.int32))
    slot_w = jnp.zeros((cap,), h.dtype).at[dest].set(top_w.reshape(-1)[order])
    block_e = jnp.minimum(jnp.searchsorted(pad_end, jnp.arange(n_blocks) * MOE_BLOCK, side='right'),
                          N_EXPERTS - 1)
    h_pad = jnp.concatenate([h, jnp.zeros((1, d), h.dtype)], axis=0)
    xb = h_pad[slot_tok].reshape(n_blocks, MOE_BLOCK, d)

    def expert_block(args):
        xe, e = args
        gate, up = jnp.split(xe @ w_gate_up[e] + b_gate_up[e], 2, axis=-1)
        gate = jnp.minimum(gate, SWIGLU_LIMIT)
        up = jnp.clip(up, -SWIGLU_LIMIT, SWIGLU_LIMIT)
        act = gate * jax.nn.sigmoid(SWIGLU_ALPHA * gate) * (up + 1)
        return act @ w_down[e] + b_down[e]

    yb = lax.map(expert_block, (xb, block_e)).reshape(cap, d)
    out = jnp.zeros((n_tok + 1, d), h.dtype).at[slot_tok].add(yb * slot_w[:, None])[:n_tok]
    return out.reshape(shape)


def hybrid_layer(x, ctx, c, c_ctx, cos, sin, last, w_ada, b_ada, norm1_w, norm2_w, w_in, conv_w, conv_b,
                 dt_bias_f, dt_bias_b, a_log_f, a_log_b, d_skip, ssm_norm_w, q_norm_w, k_norm_w, sink,
                 w_ssm_out, w_attn_out, w_o, router_w, router_b, w_gate_up, b_gate_up, w_down, b_down):
    bsz = x.shape[0]
    mod_x = (jax.nn.silu(c) @ w_ada + b_ada).reshape(bsz, 6, 1, D_MODEL)
    mod_c = (jax.nn.silu(c_ctx) @ w_ada + b_ada).reshape(6, 1, D_MODEL)
    sh1, sc1, g1, sh2, sc2, g2 = (mod_x[:, j] for j in range(6))
    sh1c, sc1c, g1c, sh2c, sc2c, g2c = (mod_c[j] for j in range(6))
    mix = functools.partial(mixer_inputs, w_in=w_in, conv_w=conv_w, conv_b=conv_b, dt_bias_f=dt_bias_f,
                            dt_bias_b=dt_bias_b, q_norm_w=q_norm_w, k_norm_w=k_norm_w)
    merge = functools.partial(merge_branches, ssm_norm_w=ssm_norm_w, w_ssm_out=w_ssm_out,
                              w_attn_out=w_attn_out, w_o=w_o)
    ffn = functools.partial(moe_ffn, router_w=router_w, router_b=router_b, w_gate_up=w_gate_up,
                            b_gate_up=b_gate_up, w_down=w_down, b_down=b_down)
    fl = lambda t: jnp.flip(t, axis=1)

    lat = mix(modulate(rms_norm(x, norm1_w), sh1, sc1))
    cx = mix(modulate(rms_norm(ctx, norm1_w), sh1c, sc1c))
    h_f = ssd_final_state(cx['xs'], cx['dt_f'], a_log_f, cx['bm'])
    h_b = ssd_final_state(fl(cx['xs']), fl(cx['dt_b']), a_log_b, fl(cx['bm']))
    y_ssm = bidir_ssd(lat, a_log_f, a_log_b, d_skip, h_f, h_b)
    y_attn = windowed_attention(apply_rope(lat['q'], cos, sin), apply_rope(lat['k'], cos, sin), lat['v'],
                                cx['k'], cx['v'], sink)
    x = x + g1 * merge(lat, y_ssm, y_attn)
    x = x + g2 * ffn(modulate(rms_norm(x, norm2_w), sh2, sc2))
    if not last:
        zero = jnp.zeros_like(h_f)
        y_ssm_c = bidir_ssd(cx, a_log_f, a_log_b, d_skip, zero, zero)
        y_attn_c = context_attention(cx['q'], cx['k'], cx['v'], sink)
        ctx = ctx + g1c * merge(cx, y_ssm_c, y_attn_c)
        ctx = ctx + g2c * ffn(modulate(rms_norm(ctx, norm2_w), sh2c, sc2c))
    return x, ctx


def setup_inputs(seed: int = 0) -> dict:
    key = jax.random.key(seed)
    keys = iter(jax.random.split(key, 32))

    def normal(shape, scale):
        return scale * jax.random.normal(next(keys), shape, jnp.float32)

    def gain(shape):
        return 1.0 + 0.02 * jax.random.normal(next(keys), shape, jnp.float32)

    def dt_bias():
        dt = jnp.exp(jax.random.uniform(next(keys), (DEPTH, N_SSM_HEADS), jnp.float32,
                                        math.log(DT_MIN), math.log(DT_MAX)))
        return dt + jnp.log(-jnp.expm1(-dt))

    def a_log():
        return jnp.log(jax.random.uniform(next(keys), (DEPTH, N_SSM_HEADS), jnp.float32, 1.0, 16.0))

    return {
        'x': normal((BATCH, SEQ, D_MODEL), 1.0),
        'c': normal((BATCH, D_MODEL), 1.0),
        'ctx': normal((BATCH, CTX_LEN, D_MODEL), 1.0),
        'c_ctx': normal((D_MODEL,), 1.0),
        'w_ada': normal((DEPTH, D_MODEL, 6 * D_MODEL), 0.5 * D_MODEL ** -0.5),
        'b_ada': normal((DEPTH, 6 * D_MODEL), 0.02),
        'norm1_w': gain((DEPTH, D_MODEL)),
        'norm2_w': gain((DEPTH, D_MODEL)),
        'w_in': normal((DEPTH, D_MODEL, PROJ_W), D_MODEL ** -0.5),
        'conv_w': normal((DEPTH, CONV_K, XBC_W), CONV_K ** -0.5),
        'conv_b': normal((DEPTH, XBC_W), 0.02),
        'dt_bias_f': dt_bias(),
        'dt_bias_b': dt_bias(),
        'a_log_f': a_log(),
        'a_log_b': a_log(),
        'd_skip': gain((DEPTH, N_SSM_HEADS)),
        'ssm_norm_w': gain((DEPTH, D_INNER)),
        'q_norm_w': gain((DEPTH, HEAD_DIM)),
        'k_norm_w': gain((DEPTH, HEAD_DIM)),
        'sink': normal((DEPTH, N_Q_HEADS), 1.0),
        'w_ssm_out': normal((DEPTH, D_INNER, D_MODEL), D_INNER ** -0.5),
        'w_attn_out': normal((DEPTH, Q_W, D_MODEL), Q_W ** -0.5),
        'w_o': normal((DEPTH, D_MODEL, D_MODEL), D_MODEL ** -0.5),
        'router_w': normal((DEPTH, D_MODEL, N_EXPERTS), D_MODEL ** -0.5),
        'router_b': normal((DEPTH, N_EXPERTS), 0.01),
        'w_gate_up': normal((DEPTH, N_EXPERTS, D_MODEL, 2 * D_FF), D_MODEL ** -0.5),
        'b_gate_up': normal((DEPTH, N_EXPERTS, 2 * D_FF), 0.02),
        'w_down': normal((DEPTH, N_EXPERTS, D_FF, D_MODEL), D_FF ** -0.5),
        'b_down': normal((DEPTH, N_EXPERTS, D_MODEL), 0.02),
    }


def reference(x, c, ctx, c_ctx, w_ada, b_ada, norm1_w, norm2_w, w_in, conv_w, conv_b, dt_bias_f, dt_bias_b,
              a_log_f, a_log_b, d_skip, ssm_norm_w, q_norm_w, k_norm_w, sink, w_ssm_out, w_attn_out, w_o,
              router_w, router_b, w_gate_up, b_gate_up, w_down, b_down):
    n_rows = x.shape[1] // GRID_W
    cos, sin = axial_rope_tables(n_rows)
    for i in range(DEPTH):
        x, ctx = hybrid_layer(x, ctx, c, c_ctx, cos, sin, i == DEPTH - 1,
                              w_ada[i], b_ada[i], norm1_w[i], norm2_w[i], w_in[i], conv_w[i], conv_b[i],
                              dt_bias_f[i], dt_bias_b[i], a_log_f[i], a_log_b[i], d_skip[i], ssm_norm_w[i],
                              q_norm_w[i], k_norm_w[i], sink[i], w_ssm_out[i], w_attn_out[i], w_o[i],
                              router_w[i], router_b[i], w_gate_up[i], b_gate_up[i], w_down[i], b_down[i])
    return x
```

```python
import functools
import math

import jax
import jax.numpy as jnp
from jax import lax
from jax.experimental import pallas as pl
from jax.experimental.pallas import tpu as pltpu

F32 = jnp.float32
BF16 = jnp.bfloat16

D_MODEL = 2048
GRID_W = 64
NORM_EPS = 1e-6
D_INNER = 2 * D_MODEL
SSM_HEAD_DIM = 64
N_SSM_HEADS = D_INNER // SSM_HEAD_DIM
N_GROUPS = 8
HEADS_PER_GROUP = N_SSM_HEADS // N_GROUPS
D_STATE = 128
BC_W = N_GROUPS * D_STATE
XBC_W = D_INNER + 2 * BC_W
CONV_K = 5
SSD_CHUNK = 128
HEAD_DIM = 128
N_Q_HEADS = D_MODEL // HEAD_DIM
N_KV_HEADS = 4
Q_PER_KV = N_Q_HEADS // N_KV_HEADS
Q_W = N_Q_HEADS * HEAD_DIM
KV_W = N_KV_HEADS * HEAD_DIM
WINDOW = 128
ATTN_BLOCK = 128
ATTN_SCALE = HEAD_DIM ** -0.5
ROPE_FREQS = HEAD_DIM // 4
ROPE_BASE = 10000.0
N_EXPERTS = 32
TOP_K = 4
D_FF = D_MODEL
SWIGLU_LIMIT = 7.0
SWIGLU_ALPHA = 1.702
MOE_BLOCK = 256

LANES = 128
ROW_TILE = 256
MAIN_W = D_INNER + XBC_W + Q_W + 2 * KV_W + 2 * D_MODEL
COL_Z, COL_XBC, COL_Q = 0, D_INNER, D_INNER + XBC_W
COL_K, COL_V = COL_Q + Q_W, COL_Q + Q_W + KV_W
COL_GS, COL_GA = COL_V + KV_W, COL_V + KV_W + D_MODEL
VMEM_LIMIT = 56 * 1024 * 1024


def _params(*sem):
    return pltpu.CompilerParams(dimension_semantics=sem, vmem_limit_bytes=VMEM_LIMIT)


def _dot(a, b):
    return jnp.dot(a, b, preferred_element_type=F32)


def _dot_nt(a, b):
    return lax.dot_general(a, b, (((1,), (1,)), ((), ())), preferred_element_type=F32)


def _split3(a):
    a1 = a.astype(BF16)
    r = a - a1.astype(F32)
    a2 = r.astype(BF16)
    a3 = (r - a2.astype(F32)).astype(BF16)
    return a1, a2, a3


def _dot_f32(a, b):
    a1, a2, _ = _split3(a)
    b1, b2, _ = _split3(b)
    return _dot(a1, b1) + (_dot(a1, b2) + _dot(a2, b1))


def _dot_sel_rhs(a, sel):
    a1, a2, a3 = _split3(a)
    return _dot(a1, sel) + (_dot(a2, sel) + _dot(a3, sel))


def _dot_sel_lhs(sel, a):
    a1, a2, a3 = _split3(a)
    return _dot(sel, a1) + (_dot(sel, a2) + _dot(sel, a3))


def _sigmoid(x):
    return 1.0 / (1.0 + jnp.exp(-x))


def _ada_kernel(c_ref, w_ref, b_ref, o_ref):
    c = c_ref[...]
    o_ref[...] = _dot_f32(c * _sigmoid(c), w_ref[...]) + b_ref[...]


def ada_modulation(cvec, w_ada, b_ada):
    n = w_ada.shape[1]
    tn = 1024
    return pl.pallas_call(
        _ada_kernel,
        out_shape=jax.ShapeDtypeStruct((8, n), F32),
        grid=(n // tn,),
        in_specs=[pl.BlockSpec((8, D_MODEL), lambda j: (0, 0)),
                  pl.BlockSpec((D_MODEL, tn), lambda j: (0, j)),
                  pl.BlockSpec((1, tn), lambda j: (0, j))],
        out_specs=pl.BlockSpec((8, tn), lambda j: (0, j)),
        compiler_params=_params("arbitrary"),
        name="ada_modulation",
    )(cvec, w_ada, b_ada.reshape(1, n))


def _norm_mod_kernel(x_ref, w_ref, mod_ref, o_ref):
    x = x_ref[...]
    y = x * lax.rsqrt(jnp.mean(x * x, axis=-1, keepdims=True) + NORM_EPS) * w_ref[...]
    o_ref[...] = (y * (1.0 + mod_ref[0, 1:2, :]) + mod_ref[0, 0:1, :]).astype(o_ref.dtype)


def _mod_row_map(n_lat_tiles, tiles_per_batch, n_batch):
    def index_map(i, *_):
        return (jnp.where(i < n_lat_tiles, i // tiles_per_batch, n_batch), 0, 0)
    return index_map


def norm_modulate(xa, norm_w, mod, n_batch, seq):
    rows = xa.shape[0]
    n_lat_tiles = n_batch * seq // ROW_TILE
    return pl.pallas_call(
        _norm_mod_kernel,
        out_shape=jax.ShapeDtypeStruct((rows, D_MODEL), BF16),
        grid=(rows // ROW_TILE,),
        in_specs=[pl.BlockSpec((ROW_TILE, D_MODEL), lambda i: (i, 0)),
                  pl.BlockSpec((1, D_MODEL), lambda i: (0, 0)),
                  pl.BlockSpec((1, 8, D_MODEL), _mod_row_map(n_lat_tiles, seq // ROW_TILE, n_batch))],
        out_specs=pl.BlockSpec((ROW_TILE, D_MODEL), lambda i: (i, 0)),
        compiler_params=_params("arbitrary"),
        name="norm_modulate",
    )(xa, norm_w.reshape(1, D_MODEL), mod)


def _mm_kernel(a_ref, b_ref, o_ref):
    o_ref[...] = _dot(a_ref[...], b_ref[...]).astype(o_ref.dtype)


def matmul(a, b, tm, tn, out_dtype=F32, name="matmul"):
    m, k = a.shape
    n = b.shape[1]
    return pl.pallas_call(
        _mm_kernel,
        out_shape=jax.ShapeDtypeStruct((m, n), out_dtype),
        grid=(n // tn, m // tm),
        in_specs=[pl.BlockSpec((tm, k), lambda j, i: (i, 0)),
                  pl.BlockSpec((k, tn), lambda j, i: (0, j))],
        out_specs=pl.BlockSpec((tm, tn), lambda j, i: (i, j)),
        compiler_params=_params("arbitrary", "arbitrary"),
        name=name,
    )(a, b)


CONV_COLS = 512
HALO = 8


def _conv_kernel(prev_ref, cur_ref, next_ref, w_ref, b_ref, o_ref, buf_ref, *,
                 n_lat_tiles, lat_tiles_per_seq, ctx_tiles_per_seq):
    i = pl.program_id(1)
    in_lat = i < n_lat_tiles
    pos = jnp.where(in_lat, i % lat_tiles_per_seq, (i - n_lat_tiles) % ctx_tiles_per_seq)
    per_seq = jnp.where(in_lat, lat_tiles_per_seq, ctx_tiles_per_seq)
    has_prev = pos > 0
    has_next = pos < per_seq - 1
    t = cur_ref.shape[0]
    buf_ref[0:HALO, :] = jnp.where(has_prev, prev_ref[...], 0.0)
    buf_ref[HALO:HALO + t, :] = cur_ref[...]
    buf_ref[HALO + t:, :] = jnp.where(has_next, next_ref[...], 0.0)
    acc = jnp.broadcast_to(b_ref[...], (t, CONV_COLS))
    for k in range(CONV_K):
        start = HALO - CONV_K // 2 + k
        acc = acc + w_ref[k:k + 1, :] * buf_ref[start:start + t, :]
    o_ref[...] = acc * _sigmoid(acc)


def conv_silu(proj, conv_w, conv_b, n_batch, seq, ctx_len):
    rows = proj.shape[0]
    n_lat_tiles = n_batch * seq // ROW_TILE
    col0 = COL_XBC // CONV_COLS
    per = ROW_TILE // HALO
    last_halo = rows // HALO - 1
    kern = functools.partial(_conv_kernel, n_lat_tiles=n_lat_tiles,
                             lat_tiles_per_seq=seq // ROW_TILE, ctx_tiles_per_seq=ctx_len // ROW_TILE)
    w8 = jnp.concatenate([conv_w, jnp.zeros((8 - CONV_K, XBC_W), F32)], axis=0)
    return pl.pallas_call(
        kern,
        out_shape=jax.ShapeDtypeStruct((rows, XBC_W), F32),
        grid=(XBC_W // CONV_COLS, rows // ROW_TILE),
        in_specs=[pl.BlockSpec((HALO, CONV_COLS), lambda j, i: (jnp.maximum(i * per - 1, 0), col0 + j)),
                  pl.BlockSpec((ROW_TILE, CONV_COLS), lambda j, i: (i, col0 + j)),
                  pl.BlockSpec((HALO, CONV_COLS), lambda j, i: (jnp.minimum((i + 1) * per, last_halo), col0 + j)),
                  pl.BlockSpec((8, CONV_COLS), lambda j, i: (0, j)),
                  pl.BlockSpec((1, CONV_COLS), lambda j, i: (0, j))],
        out_specs=pl.BlockSpec((ROW_TILE, CONV_COLS), lambda j, i: (i, j)),
        scratch_shapes=[pltpu.VMEM((ROW_TILE + 2 * HALO, CONV_COLS), F32)],
        compiler_params=_params("arbitrary", "arbitrary"),
        name="conv_silu",
    )(proj, proj, proj, w8, conv_b.reshape(1, XBC_W))


def _softplus(x):
    return jnp.maximum(x, 0.0) + jnp.log(1.0 + jnp.exp(-jnp.abs(x)))


def _ssd_kernel(xs_ref, b_ref, c_ref, dt_ref, bias_ref, a_ref, dskip_ref, eh_ref, y_ref, h_ref, *, rev):
    @pl.when(pl.program_id(1) == 0)
    def _():
        h_ref[...] = jnp.zeros_like(h_ref)

    L = SSD_CHUNK
    gw = HEADS_PER_GROUP * SSM_HEAD_DIM
    off = N_SSM_HEADS if rev else 0
    row = lax.broadcasted_iota(jnp.int32, (L, L), 0)
    col = lax.broadcasted_iota(jnp.int32, (L, L), 1)
    causal = (col >= row) if rev else (col <= row)
    tmat = jnp.where(causal, 1.0, 0.0).astype(BF16)

    dt = _softplus(dt_ref[...] + bias_ref[...])
    a = dt * a_ref[...]
    acum = _dot_sel_lhs(tmat, a)
    acum_t = acum.T
    dt_t = dt.T
    eh = eh_ref[...]
    acum_x = _dot_sel_rhs(acum, eh)
    dt_x = _dot_sel_rhs(dt, eh)
    last = 0 if rev else L - 1
    e_t = jnp.exp(acum_x)
    e_last = e_t[last:last + 1, :]
    to_end = jnp.exp(acum_x[last:last + 1, :] - acum_x) * dt_x
    xs = xs_ref[...]
    xw = (xs * to_end).astype(BF16)
    xsb = xs.astype(BF16)

    for g in range(N_GROUPS):
        gs = slice(g * gw, (g + 1) * gw)
        bg = b_ref[:, g * D_STATE:(g + 1) * D_STATE]
        cgb = c_ref[:, g * D_STATE:(g + 1) * D_STATE].astype(BF16)
        cb = _dot_nt(cgb, bg.astype(BF16))
        hg = h_ref[:, gs]
        y_off = _dot(cgb, hg.astype(BF16)) * e_t[:, gs]
        ys = []
        for r in range(HEADS_PER_GROUP):
            hd = off + g * HEADS_PER_GROUP + r
            seg = acum[:, hd:hd + 1] - acum_t[hd:hd + 1, :]
            decay = jnp.exp(jnp.where(causal, seg, -jnp.inf))
            m = (cb * decay * dt_t[hd:hd + 1, :]).astype(BF16)
            c0 = g * gw + r * SSM_HEAD_DIM
            ys.append(_dot(m, xsb[:, c0:c0 + SSM_HEAD_DIM]))
        y = jnp.concatenate(ys, axis=1) + y_off
        if not rev:
            y = y + dskip_ref[:, gs] * xs[:, gs]
        y_ref[:, gs] = y
        h_ref[:, gs] = e_last[:, gs] * hg + _dot(bg.T.astype(BF16), xw[:, gs])


def ssd_scan(xbc, dt_raw, dt_bias, a_neg, dskip_x, n_batch, seq, ctx_len, rev):
    rows = xbc.shape[0]
    nc, ncc = seq // SSD_CHUNK, ctx_len // SSD_CHUNK
    lat_blocks = n_batch * nc

    def blk(b, j):
        cj = (ncc - 1 - j) if rev else j
        lj = (nc - 1 - (j - ncc)) if rev else (j - ncc)
        return jnp.where(j < ncc, lat_blocks + b * ncc + cj, b * nc + lj)

    off = N_SSM_HEADS if rev else 0
    hrow = lax.broadcasted_iota(jnp.int32, (LANES, D_INNER), 0)
    hcol = lax.broadcasted_iota(jnp.int32, (LANES, D_INNER), 1)
    eh = jnp.where(hrow == off + hcol // SSM_HEAD_DIM, 1.0, 0.0).astype(BF16)
    xs_blocks = D_INNER // D_INNER
    return pl.pallas_call(
        functools.partial(_ssd_kernel, rev=rev),
        out_shape=jax.ShapeDtypeStruct((rows, D_INNER), F32),
        grid=(n_batch, ncc + nc),
        in_specs=[pl.BlockSpec((SSD_CHUNK, D_INNER), lambda b, j: (blk(b, j), 0)),
                  pl.BlockSpec((SSD_CHUNK, BC_W), lambda b, j: (blk(b, j), D_INNER // BC_W)),
                  pl.BlockSpec((SSD_CHUNK, BC_W), lambda b, j: (blk(b, j), D_INNER // BC_W + 1)),
                  pl.BlockSpec((SSD_CHUNK, LANES), lambda b, j: (blk(b, j), 0)),
                  pl.BlockSpec((1, LANES), lambda b, j: (0, 0)),
                  pl.BlockSpec((1, LANES), lambda b, j: (0, 0)),
                  pl.BlockSpec((1, D_INNER), lambda b, j: (0, 0)),
                  pl.BlockSpec((LANES, D_INNER), lambda b, j: (0, 0))],
        out_specs=pl.BlockSpec((SSD_CHUNK, D_INNER), lambda b, j: (blk(b, j), 0)),
        scratch_shapes=[pltpu.VMEM((D_STATE, D_INNER), F32)],
        compiler_params=_params("arbitrary", "arbitrary"),
        name="ssd_scan_bwd" if rev else "ssd_scan_fwd",
    )(xbc, xbc, xbc, dt_raw, dt_bias, a_neg, dskip_x, eh)


def _rope_tables(n_batch, seq, ctx_len):
    inv_freq = ROPE_BASE ** (-jnp.arange(ROPE_FREQS, dtype=F32) / ROPE_FREQS)
    n_rows = seq // GRID_W
    r = jnp.repeat(jnp.arange(n_rows, dtype=F32), GRID_W)
    c = jnp.tile(jnp.arange(GRID_W, dtype=F32), n_rows)
    ar = r[:, None] * inv_freq
    ac = c[:, None] * inv_freq
    ang = jnp.concatenate([ar, ar, ac, ac], axis=-1)
    cos, sin = jnp.cos(ang), jnp.sin(ang)
    first_half = (jnp.arange(HEAD_DIM) % (2 * ROPE_FREQS)) < ROPE_FREQS
    sin_up = jnp.where(first_half, -sin, 0.0)
    sin_dn = jnp.where(first_half, 0.0, sin)
    n_ctx = n_batch * ctx_len

    def rows(t, fill):
        return jnp.concatenate([jnp.tile(t, (n_batch, 1)), jnp.full((n_ctx, HEAD_DIM), fill, F32)], axis=0)
    return rows(cos, 1.0), rows(sin_up, 0.0), rows(sin_dn, 0.0)


def _qk_prep_kernel(x_ref, w_ref, cos_ref, su_ref, sd_ref, o_ref, *, n_heads, scale):
    cos, su, sd = cos_ref[...], su_ref[...], sd_ref[...]
    w = w_ref[...]
    for h in range(n_heads):
        hs = slice(h * HEAD_DIM, (h + 1) * HEAD_DIM)
        x = x_ref[:, hs]
        n = x * lax.rsqrt(jnp.mean(x * x, axis=-1, keepdims=True) + NORM_EPS) * w
        y = n * cos + pltpu.roll(n, HEAD_DIM - ROPE_FREQS, 1) * su + pltpu.roll(n, ROPE_FREQS, 1) * sd
        if scale != 1.0:
            y = y * scale
        o_ref[:, hs] = y.astype(o_ref.dtype)


def qk_prep(proj, norm_w, tables, rows, col, width, scale, name):
    n_heads = width // HEAD_DIM
    tab_spec = pl.BlockSpec((ROW_TILE, HEAD_DIM), lambda i: (i, 0))
    return pl.pallas_call(
        functools.partial(_qk_prep_kernel, n_heads=n_heads, scale=scale),
        out_shape=jax.ShapeDtypeStruct((rows, width), BF16),
        grid=(rows // ROW_TILE,),
        in_specs=[pl.BlockSpec((ROW_TILE, width), lambda i: (i, col // width)),
                  pl.BlockSpec((1, HEAD_DIM), lambda i: (0, 0)),
                  tab_spec, tab_spec, tab_spec],
        out_specs=pl.BlockSpec((ROW_TILE, width), lambda i: (i, 0)),
        compiler_params=_params("arbitrary"),
        name=name,
    )(proj, norm_w.reshape(1, HEAD_DIM), *tables)


def _cast_kernel(x_ref, o_ref):
    o_ref[...] = x_ref[...].astype(o_ref.dtype)


def cast_cols(proj, col, width, name):
    rows = proj.shape[0]
    return pl.pallas_call(
        _cast_kernel,
        out_shape=jax.ShapeDtypeStruct((rows, width), BF16),
        grid=(rows // ROW_TILE,),
        in_specs=[pl.BlockSpec((ROW_TILE, width), lambda i: (i, col // width))],
        out_specs=pl.BlockSpec((ROW_TILE, width), lambda i: (i, 0)),
        compiler_params=_params("arbitrary"),
        name=name,
    )(proj)


def _attn_kernel(q_ref, kp_ref, kc_ref, kn_ref, vp_ref, vc_ref, vn_ref, kx_ref, vx_ref, sink_ref, o_ref, *,
                 n_blocks):
    i = pl.program_id(1)
    T = ATTN_BLOCK
    nq = Q_PER_KV * T
    qi = lax.broadcasted_iota(jnp.int32, (nq, 3 * T), 0) % T
    kj = lax.broadcasted_iota(jnp.int32, (nq, 3 * T), 1)
    lo = jnp.maximum(qi, jnp.where(i > 0, 0, T))
    hi = jnp.minimum(qi + 2 * WINDOW, jnp.where(i < n_blocks - 1, 3 * T - 1, 2 * T - 1))
    valid = (kj >= lo) & (kj <= hi)
    for h in range(N_KV_HEADS):
        hs = slice(h * HEAD_DIM, (h + 1) * HEAD_DIM)
        q = jnp.concatenate([q_ref[:, (h * Q_PER_KV + g) * HEAD_DIM:(h * Q_PER_KV + g + 1) * HEAD_DIM]
                             for g in range(Q_PER_KV)], axis=0)
        kb = jnp.concatenate([kp_ref[:, hs], kc_ref[:, hs], kn_ref[:, hs]], axis=0)
        vb = jnp.concatenate([vp_ref[:, hs], vc_ref[:, hs], vn_ref[:, hs]], axis=0)
        s_lat = jnp.where(valid, _dot_nt(q, kb), -jnp.inf)
        s_ctx = _dot_nt(q, kx_ref[:, hs])
        sk = jnp.concatenate([jnp.broadcast_to(sink_ref[0:1, h * Q_PER_KV + g:h * Q_PER_KV + g + 1], (T, 1))
                              for g in range(Q_PER_KV)], axis=0)
        mx = jnp.maximum(jnp.maximum(jnp.max(s_lat, axis=-1, keepdims=True),
                                     jnp.max(s_ctx, axis=-1, keepdims=True)), sk)
        p_lat = jnp.exp(s_lat - mx)
        p_ctx = jnp.exp(s_ctx - mx)
        denom = (jnp.sum(p_lat, axis=-1, keepdims=True) + jnp.sum(p_ctx, axis=-1, keepdims=True)
                 + jnp.exp(sk - mx))
        o = (_dot(p_lat.astype(BF16), vb) + _dot(p_ctx.astype(BF16), vx_ref[:, hs])) * (1.0 / denom)
        for g in range(Q_PER_KV):
            c0 = (h * Q_PER_KV + g) * HEAD_DIM
            o_ref[:, c0:c0 + HEAD_DIM] = o[g * T:(g + 1) * T, :].astype(o_ref.dtype)


def windowed_attention(qn, kn, vb, sink, n_batch, seq, ctx_len):
    nb = seq // ATTN_BLOCK
    ctx0 = n_batch * seq // ctx_len
    sink_row = jnp.zeros((1, LANES), F32).at[0, :N_Q_HEADS].set(sink)

    def kv_spec(d):
        return pl.BlockSpec((ATTN_BLOCK, KV_W), lambda b, i: (b * nb + jnp.clip(i + d, 0, nb - 1), 0))
    ctx_spec = pl.BlockSpec((ctx_len, KV_W), lambda b, i: (ctx0 + b, 0))
    return pl.pallas_call(
        functools.partial(_attn_kernel, n_blocks=nb),
        out_shape=jax.ShapeDtypeStruct((n_batch * seq, Q_W), BF16),
        grid=(n_batch, nb),
        in_specs=[pl.BlockSpec((ATTN_BLOCK, Q_W), lambda b, i: (b * nb + i, 0)),
                  kv_spec(-1), kv_spec(0), kv_spec(1), kv_spec(-1), kv_spec(0), kv_spec(1),
                  ctx_spec, ctx_spec,
                  pl.BlockSpec((1, LANES), lambda b, i: (0, 0))],
        out_specs=pl.BlockSpec((ATTN_BLOCK, Q_W), lambda b, i: (b * nb + i, 0)),
        compiler_params=_params("arbitrary", "arbitrary"),
        name="windowed_attention",
    )(qn, kn, kn, kn, vb, vb, vb, kn, vb, sink_row)


def _gate_norm_kernel(yf_ref, yb_ref, z_ref, w_ref, o_ref):
    gw = D_INNER // N_GROUPS
    for g in range(N_GROUPS):
        gs = slice(g * gw, (g + 1) * gw)
        z = z_ref[:, gs]
        s = (yf_ref[:, gs] + yb_ref[:, gs]) * (z * _sigmoid(z))
        n = s * lax.rsqrt(jnp.mean(s * s, axis=-1, keepdims=True) + NORM_EPS) * w_ref[:, gs]
        o_ref[:, gs] = n.astype(o_ref.dtype)


def gate_norm(y_f, y_b, proj, ssm_norm_w, rows):
    spec = pl.BlockSpec((ROW_TILE, D_INNER), lambda i: (i, 0))
    return pl.pallas_call(
        _gate_norm_kernel,
        out_shape=jax.ShapeDtypeStruct((rows, D_INNER), BF16),
        grid=(rows // ROW_TILE,),
        in_specs=[spec, spec, pl.BlockSpec((ROW_TILE, D_INNER), lambda i: (i, COL_Z // D_INNER)),
                  pl.BlockSpec((1, D_INNER), lambda i: (0, 0))],
        out_specs=spec,
        compiler_params=_params("arbitrary"),
        name="gate_norm",
    )(y_f, y_b, proj, ssm_norm_w.reshape(1, D_INNER))


def _merge_kernel(yg_ref, ya_ref, ws_ref, wa_ref, gs_ref, ga_ref, o_ref):
    o = (_sigmoid(gs_ref[...]) * _dot(yg_ref[...], ws_ref[...])
         + _sigmoid(ga_ref[...]) * _dot(ya_ref[...], wa_ref[...]))
    o_ref[...] = o.astype(o_ref.dtype)


def merge_branches(yg, ya, w_ssm_out, w_attn_out, proj, rows):
    tm, tn = 512, 512
    return pl.pallas_call(
        _merge_kernel,
        out_shape=jax.ShapeDtypeStruct((rows, D_MODEL), BF16),
        grid=(D_MODEL // tn, rows // tm),
        in_specs=[pl.BlockSpec((tm, D_INNER), lambda j, i: (i, 0)),
                  pl.BlockSpec((tm, Q_W), lambda j, i: (i, 0)),
                  pl.BlockSpec((D_INNER, tn), lambda j, i: (0, j)),
                  pl.BlockSpec((Q_W, tn), lambda j, i: (0, j)),
                  pl.BlockSpec((tm, tn), lambda j, i: (i, COL_GS // tn + j)),
                  pl.BlockSpec((tm, tn), lambda j, i: (i, COL_GA // tn + j))],
        out_specs=pl.BlockSpec((tm, tn), lambda j, i: (i, j)),
        compiler_params=_params("arbitrary", "arbitrary"),
        name="merge_branches",
    )(yg, ya, w_ssm_out, w_attn_out, proj, proj)


def _out_proj_kernel(m_ref, wo_ref, x_ref, mod_ref, nw_ref, rw_ref, rb_ref, x1_ref, h2_ref, lg_ref):
    x1 = x_ref[...] + mod_ref[0, 2:3, :] * _dot(m_ref[...], wo_ref[...])
    x1_ref[...] = x1
    n = x1 * lax.rsqrt(jnp.mean(x1 * x1, axis=-1, keepdims=True) + NORM_EPS) * nw_ref[...]
    h2 = n * (1.0 + mod_ref[0, 4:5, :]) + mod_ref[0, 3:4, :]
    h2_ref[...] = h2
    lg_ref[...] = _dot_f32(h2, rw_ref[...]) + rb_ref[...]


def out_proj_router(merged, w_o, xa, mod, norm2_w, router_w, router_b, n_batch, seq):
    rows = n_batch * seq
    n_tiles = rows // ROW_TILE
    rw = jnp.zeros((D_MODEL, LANES), F32).at[:, :N_EXPERTS].set(router_w)
    rb = jnp.zeros((1, LANES), F32).at[0, :N_EXPERTS].set(router_b)
    tile = pl.BlockSpec((ROW_TILE, D_MODEL), lambda i: (i, 0))
    return pl.pallas_call(
        _out_proj_kernel,
        out_shape=(jax.ShapeDtypeStruct((rows, D_MODEL), F32),
                   jax.ShapeDtypeStruct((rows, D_MODEL), F32),
                   jax.ShapeDtypeStruct((rows, LANES), F32)),
        grid=(n_tiles,),
        in_specs=[tile,
                  pl.BlockSpec((D_MODEL, D_MODEL), lambda i: (0, 0)),
                  tile,
                  pl.BlockSpec((1, 8, D_MODEL), _mod_row_map(n_tiles, seq // ROW_TILE, n_batch)),
                  pl.BlockSpec((1, D_MODEL), lambda i: (0, 0)),
                  pl.BlockSpec((D_MODEL, LANES), lambda i: (0, 0)),
                  pl.BlockSpec((1, LANES), lambda i: (0, 0))],
        out_specs=(tile, tile, pl.BlockSpec((ROW_TILE, LANES), lambda i: (i, 0))),
        compiler_params=_params("arbitrary"),
        name="out_proj_router",
    )(merged, w_o, xa, mod, norm2_w.reshape(1, D_MODEL), rw, rb)


def _route_kernel(lg_ref, o_ref, cnt_ref, carry_ref):
    @pl.when(pl.program_id(0) == 0)
    def _():
        carry_ref[...] = jnp.zeros_like(carry_ref)

    t = lg_ref.shape[0]
    lane = lax.broadcasted_iota(jnp.int32, (t, LANES), 1).astype(F32)
    work = jnp.where(lane < N_EXPERTS, lg_ref[...], -jnp.inf)
    vals, idxs = [], []
    for _ in range(TOP_K):
        m = jnp.max(work, axis=-1, keepdims=True)
        idx = jnp.min(jnp.where(work == m, lane, float(LANES)), axis=-1, keepdims=True)
        vals.append(m)
        idxs.append(idx)
        work = jnp.where(lane == idx, -jnp.inf, work)
    es = [jnp.exp(v - vals[0]) for v in vals]
    inv = 1.0 / (es[0] + es[1] + es[2] + es[3])
    onehot = jnp.zeros((t, LANES), F32)
    for idx in idxs:
        onehot = onehot + jnp.where(lane == idx, 1.0, 0.0)
    r = lax.broadcasted_iota(jnp.int32, (t, t), 0)
    c = lax.broadcasted_iota(jnp.int32, (t, t), 1)
    before = jnp.where(c < r, 1.0, 0.0).astype(BF16)
    excl = _dot(before, onehot.astype(BF16)) + carry_ref[...]
    out = jnp.zeros((t, LANES), F32)
    for k in range(TOP_K):
        rank = jnp.sum(jnp.where(lane == idxs[k], excl, 0.0), axis=-1, keepdims=True)
        out = jnp.where(lane == k, idxs[k], out)
        out = jnp.where(lane == TOP_K + k, es[k] * inv, out)
        out = jnp.where(lane == 2 * TOP_K + k, rank, out)
    o_ref[...] = out
    carry_ref[...] = carry_ref[...] + jnp.sum(onehot, axis=0, keepdims=True)
    cnt_ref[...] = carry_ref[...]


def route(logits):
    n_tok = logits.shape[0]
    return pl.pallas_call(
        _route_kernel,
        out_shape=(jax.ShapeDtypeStruct((n_tok, LANES), F32), jax.ShapeDtypeStruct((1, LANES), F32)),
        grid=(n_tok // ROW_TILE,),
        in_specs=[pl.BlockSpec((ROW_TILE, LANES), lambda i: (i, 0))],
        out_specs=(pl.BlockSpec((ROW_TILE, LANES), lambda i: (i, 0)),
                   pl.BlockSpec((1, LANES), lambda i: (0, 0))),
        scratch_shapes=[pltpu.VMEM((1, LANES), F32)],
        compiler_params=_params("arbitrary"),
        name="route",
    )(logits)


def _row_copy(src_hbm, dst_vmem, sem, src_row, dst_row):
    return pltpu.make_async_copy(src_hbm.at[pl.ds(src_row, 1)], dst_vmem.at[pl.ds(dst_row, 1)], sem)


def _gather_kernel(tok_ref, h_hbm, o_ref, buf_ref, sem):
    n = buf_ref.shape[0]

    def start(r, carry):
        _row_copy(h_hbm, buf_ref, sem, tok_ref[0, 0, r], r).start()
        return carry
    lax.fori_loop(0, n, start, 0)

    def wait(r, carry):
        _row_copy(h_hbm, buf_ref, sem, 0, r).wait()
        return carry
    lax.fori_loop(0, n, wait, 0)
    o_ref[...] = buf_ref[...].astype(o_ref.dtype)


def gather_rows(h2, slot_tok):
    cap = slot_tok.shape[0]
    n_blocks = cap // MOE_BLOCK
    return pl.pallas_call(
        _gather_kernel,
        out_shape=jax.ShapeDtypeStruct((cap, D_MODEL), BF16),
        grid=(n_blocks,),
        in_specs=[pl.BlockSpec((1, 1, MOE_BLOCK), lambda i: (i, 0, 0), memory_space=pltpu.SMEM),
                  pl.BlockSpec(memory_space=pl.ANY)],
        out_specs=pl.BlockSpec((MOE_BLOCK, D_MODEL), lambda i: (i, 0)),
        scratch_shapes=[pltpu.VMEM((MOE_BLOCK, D_MODEL), F32), pltpu.SemaphoreType.DMA],
        compiler_params=_params("arbitrary"),
        name="moe_gather",
    )(slot_tok.reshape(n_blocks, 1, MOE_BLOCK), h2)


def _expert_up_kernel(be_ref, x_ref, wg_ref, wu_ref, bg_ref, bu_ref, o_ref):
    x = x_ref[...]
    gate = jnp.minimum(_dot(x, wg_ref[...]) + bg_ref[0], SWIGLU_LIMIT)
    up = jnp.clip(_dot(x, wu_ref[...]) + bu_ref[0], -SWIGLU_LIMIT, SWIGLU_LIMIT)
    o_ref[...] = (gate * _sigmoid(SWIGLU_ALPHA * gate) * (up + 1.0)).astype(o_ref.dtype)


def expert_up(xs, block_e, w_gate_up, b_gate_up):
    cap = xs.shape[0]
    tf = 1024
    nj = D_FF // tf
    b3 = b_gate_up.reshape(N_EXPERTS, 1, 2 * D_FF)
    grid_spec = pltpu.PrefetchScalarGridSpec(
        num_scalar_prefetch=1,
        grid=(nj, cap // MOE_BLOCK),
        in_specs=[pl.BlockSpec((MOE_BLOCK, D_MODEL), lambda j, i, be: (i, 0)),
                  pl.BlockSpec((None, D_MODEL, tf), lambda j, i, be: (be[i], 0, j)),
                  pl.BlockSpec((None, D_MODEL, tf), lambda j, i, be: (be[i], 0, nj + j)),
                  pl.BlockSpec((None, 1, tf), lambda j, i, be: (be[i], 0, j)),
                  pl.BlockSpec((None, 1, tf), lambda j, i, be: (be[i], 0, nj + j))],
        out_specs=pl.BlockSpec((MOE_BLOCK, tf), lambda j, i, be: (i, j)))
    return pl.pallas_call(
        _expert_up_kernel,
        out_shape=jax.ShapeDtypeStruct((cap, D_FF), BF16),
        grid_spec=grid_spec,
        compiler_params=_params("arbitrary", "arbitrary"),
        name="expert_up",
    )(block_e, xs, w_gate_up, w_gate_up, b3, b3)


def _expert_down_kernel(be_ref, a_ref, w_ref, b_ref, o_ref):
    o_ref[...] = _dot(a_ref[...], w_ref[...]) + b_ref[0]


def expert_down(act, block_e, w_down, b_down):
    cap = act.shape[0]
    tn = 1024
    grid_spec = pltpu.PrefetchScalarGridSpec(
        num_scalar_prefetch=1,
        grid=(D_MODEL // tn, cap // MOE_BLOCK),
        in_specs=[pl.BlockSpec((MOE_BLOCK, D_FF), lambda j, i, be: (i, 0)),
                  pl.BlockSpec((None, D_FF, tn), lambda j, i, be: (be[i], 0, j)),
                  pl.BlockSpec((None, 1, tn), lambda j, i, be: (be[i], 0, j))],
        out_specs=pl.BlockSpec((MOE_BLOCK, tn), lambda j, i, be: (i, j)))
    return pl.pallas_call(
        _expert_down_kernel,
        out_shape=jax.ShapeDtypeStruct((cap, D_MODEL), F32),
        grid_spec=grid_spec,
        compiler_params=_params("arbitrary", "arbitrary"),
        name="expert_down",
    )(block_e, act, w_down, b_down.reshape(N_EXPERTS, 1, D_MODEL))


COMBINE_TILE = 128


def _combine_kernel(dest_ref, yb_hbm, x1_ref, w_ref, mod_ref, o_ref, buf_ref, sem):
    t = COMBINE_TILE

    def start(r, carry):
        for k in range(TOP_K):
            _row_copy(yb_hbm, buf_ref.at[k], sem, dest_ref[0, 0, r * TOP_K + k], r).start()
        return carry
    lax.fori_loop(0, t, start, 0)

    def wait(r, carry):
        for k in range(TOP_K):
            _row_copy(yb_hbm, buf_ref.at[k], sem, 0, r).wait()
        return carry
    lax.fori_loop(0, t, wait, 0)
    acc = w_ref[:, TOP_K:TOP_K + 1] * buf_ref[0]
    for k in range(1, TOP_K):
        acc = acc + w_ref[:, TOP_K + k:TOP_K + k + 1] * buf_ref[k]
    o_ref[...] = x1_ref[...] + mod_ref[0, 5:6, :] * acc


def combine(yb, dest, x1, route_out, mod, n_batch, seq):
    rows = n_batch * seq
    t = COMBINE_TILE
    n_tiles = rows // t
    return pl.pallas_call(
        _combine_kernel,
        out_shape=jax.ShapeDtypeStruct((rows, D_MODEL), F32),
        grid=(n_tiles,),
        in_specs=[pl.BlockSpec((1, 1, t * TOP_K), lambda i: (i, 0, 0), memory_space=pltpu.SMEM),
                  pl.BlockSpec(memory_space=pl.ANY),
                  pl.BlockSpec((t, D_MODEL), lambda i: (i, 0)),
                  pl.BlockSpec((t, LANES), lambda i: (i, 0)),
                  pl.BlockSpec((1, 8, D_MODEL), _mod_row_map(n_tiles, seq // t, n_batch))],
        out_specs=pl.BlockSpec((t, D_MODEL), lambda i: (i, 0)),
        scratch_shapes=[pltpu.VMEM((TOP_K, t, D_MODEL), F32), pltpu.SemaphoreType.DMA],
        compiler_params=_params("arbitrary"),
        name="moe_combine",
    )(dest.reshape(n_tiles, 1, t * TOP_K), yb, x1, route_out, mod)


def moe_layout(route_out, counts):
    n_tok = route_out.shape[0]
    idx = route_out[:, :TOP_K].astype(jnp.int32)
    rank = route_out[:, 2 * TOP_K:3 * TOP_K].astype(jnp.int32)
    cnt = counts[0, :N_EXPERTS].astype(jnp.int32)
    padded = (cnt + MOE_BLOCK - 1) // MOE_BLOCK * MOE_BLOCK
    pad_end = jnp.cumsum(padded)
    pad_start = pad_end - padded
    dest = (pad_start[idx] + rank).reshape(-1)
    n_blocks = -(-(n_tok * TOP_K + N_EXPERTS * (MOE_BLOCK - 1)) // MOE_BLOCK)
    tok = jnp.arange(n_tok * TOP_K, dtype=jnp.int32) // TOP_K
    slot_tok = jnp.zeros((n_blocks * MOE_BLOCK,), jnp.int32).at[dest].set(tok)
    block_e = jnp.minimum(jnp.searchsorted(pad_end, jnp.arange(n_blocks, dtype=jnp.int32) * MOE_BLOCK,
                                           side='right'), N_EXPERTS - 1).astype(jnp.int32)
    return dest, slot_tok, block_e


def _in_proj_weights(w_in):
    sizes = (XBC_W, D_INNER, N_SSM_HEADS, N_SSM_HEADS, Q_W, KV_W, KV_W, D_MODEL, D_MODEL)
    offs = [0]
    for s in sizes:
        offs.append(offs[-1] + s)
    seg = lambda i: w_in[:, offs[i]:offs[i + 1]]
    w_main = jnp.concatenate([seg(1), seg(0), seg(4), seg(5), seg(6), seg(7), seg(8)], axis=1).astype(BF16)
    w_dt = jnp.concatenate([seg(2), seg(3)], axis=1).astype(BF16)
    return w_main, w_dt


def hybrid_layer(x, ctx, c, c_ctx, w_ada, b_ada, norm1_w, norm2_w, w_in, conv_w, conv_b, dt_bias_f, dt_bias_b,
                 a_log_f, a_log_b, d_skip, ssm_norm_w, q_norm_w, k_norm_w, sink, w_ssm_out, w_attn_out, w_o,
                 router_w, router_b, w_gate_up, b_gate_up, w_down, b_down):
    n_batch, seq, _ = x.shape
    ctx_len = ctx.shape[1]
    n_lat = n_batch * seq
    xa = jnp.concatenate([x.reshape(n_lat, D_MODEL), ctx.reshape(n_batch * ctx_len, D_MODEL)], axis=0)

    cvec = jnp.zeros((8, D_MODEL), F32).at[:n_batch].set(c).at[n_batch].set(c_ctx)
    mod = ada_modulation(cvec, w_ada, b_ada).reshape(8, 6, D_MODEL)
    mod = jnp.concatenate([mod, jnp.zeros((8, 2, D_MODEL), F32)], axis=1)

    hn = norm_modulate(xa, norm1_w, mod, n_batch, seq)
    w_main, w_dt = _in_proj_weights(w_in)
    proj = matmul(hn, w_main, 512, 1024, name="in_proj")
    dt_raw = matmul(hn, w_dt, 512, LANES, name="in_proj_dt")

    xbc = conv_silu(proj, conv_w, conv_b, n_batch, seq, ctx_len)
    dt_bias = jnp.concatenate([dt_bias_f, dt_bias_b]).reshape(1, LANES)
    a_neg = -jnp.exp(jnp.concatenate([a_log_f, a_log_b])).reshape(1, LANES)
    dskip_x = jnp.repeat(d_skip, SSM_HEAD_DIM).reshape(1, D_INNER)
    y_f = ssd_scan(xbc, dt_raw, dt_bias, a_neg, dskip_x, n_batch, seq, ctx_len, rev=False)
    y_b = ssd_scan(xbc, dt_raw, dt_bias, a_neg, dskip_x, n_batch, seq, ctx_len, rev=True)

    tables = _rope_tables(n_batch, seq, ctx_len)
    qn = qk_prep(proj, q_norm_w, tables, n_lat, COL_Q, Q_W, ATTN_SCALE, "q_prep")
    kn = qk_prep(proj, k_norm_w, tables, xa.shape[0], COL_K, KV_W, 1.0, "k_prep")
    vb = cast_cols(proj, COL_V, KV_W, "v_cast")
    y_attn = windowed_attention(qn, kn, vb, sink, n_batch, seq, ctx_len)

    yg = gate_norm(y_f, y_b, proj, ssm_norm_w, n_lat)
    merged = merge_branches(yg, y_attn, w_ssm_out.astype(BF16), w_attn_out.astype(BF16), proj, n_lat)
    x1, h2, logits = out_proj_router(merged, w_o.astype(BF16), xa, mod, norm2_w, router_w, router_b,
                                     n_batch, seq)

    route_out, counts = route(logits)
    dest, slot_tok, block_e = moe_layout(route_out, counts)
    xs = gather_rows(h2, slot_tok)
    act = expert_up(xs, block_e, w_gate_up.astype(BF16), b_gate_up)
    yb = expert_down(act, block_e, w_down.astype(BF16), b_down)
    out = combine(yb, dest, x1, route_out, mod, n_batch, seq)
    return out.reshape(n_batch, seq, D_MODEL)


def kernel(x, c, ctx, c_ctx, w_ada, b_ada, norm1_w, norm2_w, w_in, conv_w, conv_b, dt_bias_f, dt_bias_b,
           a_log_f, a_log_b, d_skip, ssm_norm_w, q_norm_w, k_norm_w, sink, w_ssm_out, w_attn_out, w_o,
           router_w, router_b, w_gate_up, b_gate_up, w_down, b_down):
    assert w_ada.shape[0] == 1, "single-layer block"
    return hybrid_layer(x, ctx, c, c_ctx, w_ada[0], b_ada[0], norm1_w[0], norm2_w[0], w_in[0], conv_w[0],
                        conv_b[0], dt_bias_f[0], dt_bias_b[0], a_log_f[0], a_log_b[0], d_skip[0],
                        ssm_norm_w[0], q_norm_w[0], k_norm_w[0], sink[0], w_ssm_out[0], w_attn_out[0], w_o[0],
                        router_w[0], router_b[0], w_gate_up[0], b_gate_up[0], w_down[0], b_down[0])
```

```python
import functools
import math

import jax
import jax.numpy as jnp
from jax import lax
from jax.experimental import pallas as pl
from jax.experimental.pallas import tpu as pltpu

F32 = jnp.float32
BF16 = jnp.bfloat16

D_MODEL = 2048
GRID_W = 64
NORM_EPS = 1e-6
D_INNER = 2 * D_MODEL
SSM_HEAD_DIM = 64
N_SSM_HEADS = D_INNER // SSM_HEAD_DIM
N_GROUPS = 8
HEADS_PER_GROUP = N_SSM_HEADS // N_GROUPS
D_STATE = 128
BC_W = N_GROUPS * D_STATE
XBC_W = D_INNER + 2 * BC_W
CONV_K = 5
SSD_CHUNK = 128
HEAD_DIM = 128
N_Q_HEADS = D_MODEL // HEAD_DIM
N_KV_HEADS = 4
Q_PER_KV = N_Q_HEADS // N_KV_HEADS
Q_W = N_Q_HEADS * HEAD_DIM
KV_W = N_KV_HEADS * HEAD_DIM
WINDOW = 128
ATTN_BLOCK = 128
ATTN_SCALE = HEAD_DIM ** -0.5
ROPE_FREQS = HEAD_DIM // 4
ROPE_BASE = 10000.0
N_EXPERTS = 32
TOP_K = 4
D_FF = D_MODEL
SWIGLU_LIMIT = 7.0
SWIGLU_ALPHA = 1.702
MOE_BLOCK = 256

LANES = 128
ROW_TILE = 256
MAIN_W = D_INNER + XBC_W + Q_W + 2 * KV_W + 2 * D_MODEL
COL_Z, COL_XBC, COL_Q = 0, D_INNER, D_INNER + XBC_W
COL_K, COL_V = COL_Q + Q_W, COL_Q + Q_W + KV_W
COL_GS, COL_GA = COL_V + KV_W, COL_V + KV_W + D_MODEL
VMEM_LIMIT = 56 * 1024 * 1024


def _params(*sem):
    return pltpu.CompilerParams(dimension_semantics=sem, vmem_limit_bytes=VMEM_LIMIT)


def _dot(a, b):
    return jnp.dot(a, b, preferred_element_type=F32)


def _dot_nt(a, b):
    return lax.dot_general(a, b, (((1,), (1,)), ((), ())), preferred_element_type=F32)


def _split3(a):
    a1 = a.astype(BF16)
    r = a - a1.astype(F32)
    a2 = r.astype(BF16)
    a3 = (r - a2.astype(F32)).astype(BF16)
    return a1, a2, a3


def _dot_f32(a, b):
    a1, a2, _ = _split3(a)
    b1, b2, _ = _split3(b)
    return _dot(a1, b1) + (_dot(a1, b2) + _dot(a2, b1))


def _dot_sel_rhs(a, sel):
    a1, a2, a3 = _split3(a)
    return _dot(a1, sel) + (_dot(a2, sel) + _dot(a3, sel))


def _dot_sel_lhs(sel, a):
    a1, a2, a3 = _split3(a)
    return _dot(sel, a1) + (_dot(sel, a2) + _dot(sel, a3))


def _sigmoid(x):
    return 1.0 / (1.0 + jnp.exp(-x))


def _ada_kernel(c_ref, w_ref, b_ref, o_ref):
    c = c_ref[...]
    o_ref[...] = _dot_f32(c * _sigmoid(c), w_ref[...]) + b_ref[...]


def ada_modulation(cvec, w_ada, b_ada):
    n = w_ada.shape[1]
    tn = 1024
    return pl.pallas_call(
        _ada_kernel,
        out_shape=jax.ShapeDtypeStruct((8, n), F32),
        grid=(n // tn,),
        in_specs=[pl.BlockSpec((8, D_MODEL), lambda j: (0, 0)),
                  pl.BlockSpec((D_MODEL, tn), lambda j: (0, j)),
                  pl.BlockSpec((1, tn), lambda j: (0, j))],
        out_specs=pl.BlockSpec((8, tn), lambda j: (0, j)),
        compiler_params=_params("arbitrary"),
        name="ada_modulation",
    )(cvec, w_ada, b_ada.reshape(1, n))


def _norm_mod_kernel(x_ref, w_ref, mod_ref, o_ref):
    x = x_ref[...]
    y = x * lax.rsqrt(jnp.mean(x * x, axis=-1, keepdims=True) + NORM_EPS) * w_ref[...]
    o_ref[...] = (y * (1.0 + mod_ref[0, 1:2, :]) + mod_ref[0, 0:1, :]).astype(o_ref.dtype)


def _mod_row_map(n_lat_tiles, tiles_per_batch, n_batch):
    def index_map(i, *_):
        return (jnp.where(i < n_lat_tiles, i // tiles_per_batch, n_batch), 0, 0)
    return index_map


def norm_modulate(xa, norm_w, mod, n_batch, seq):
    rows = xa.shape[0]
    n_lat_tiles = n_batch * seq // ROW_TILE
    return pl.pallas_call(
        _norm_mod_kernel,
        out_shape=jax.ShapeDtypeStruct((rows, D_MODEL), BF16),
        grid=(rows // ROW_TILE,),
        in_specs=[pl.BlockSpec((ROW_TILE, D_MODEL), lambda i: (i, 0)),
                  pl.BlockSpec((1, D_MODEL), lambda i: (0, 0)),
                  pl.BlockSpec((1, 8, D_MODEL), _mod_row_map(n_lat_tiles, seq // ROW_TILE, n_batch))],
        out_specs=pl.BlockSpec((ROW_TILE, D_MODEL), lambda i: (i, 0)),
        compiler_params=_params("arbitrary"),
        name="norm_modulate",
    )(xa, norm_w.reshape(1, D_MODEL), mod)


def _mm_kernel(a_ref, b_ref, o_ref):
    o_ref[...] = _dot(a_ref[...], b_ref[...]).astype(o_ref.dtype)


def matmul(a, b, tm, tn, out_dtype=F32, name="matmul"):
    m, k = a.shape
    n = b.shape[1]
    return pl.pallas_call(
        _mm_kernel,
        out_shape=jax.ShapeDtypeStruct((m, n), out_dtype),
        grid=(n // tn, m // tm),
        in_specs=[pl.BlockSpec((tm, k), lambda j, i: (i, 0)),
                  pl.BlockSpec((k, tn), lambda j, i: (0, j))],
        out_specs=pl.BlockSpec((tm, tn), lambda j, i: (i, j)),
        compiler_params=_params("arbitrary", "arbitrary"),
        name=name,
    )(a, b)


CONV_COLS = 2048
HALO = 8


def _conv_kernel(prev_ref, cur_ref, next_ref, w_ref, b_ref, o_ref, buf_ref, *,
                 n_lat_tiles, lat_tiles_per_seq, ctx_tiles_per_seq):
    i = pl.program_id(1)
    in_lat = i < n_lat_tiles
    pos = jnp.where(in_lat, i % lat_tiles_per_seq, (i - n_lat_tiles) % ctx_tiles_per_seq)
    per_seq = jnp.where(in_lat, lat_tiles_per_seq, ctx_tiles_per_seq)
    has_prev = pos > 0
    has_next = pos < per_seq - 1
    t = cur_ref.shape[0]
    buf_ref[0:HALO, :] = jnp.where(has_prev, prev_ref[...], 0.0)
    buf_ref[HALO:HALO + t, :] = cur_ref[...]
    buf_ref[HALO + t:, :] = jnp.where(has_next, next_ref[...], 0.0)
    acc = jnp.broadcast_to(b_ref[...], (t, CONV_COLS))
    for k in range(CONV_K):
        start = HALO - CONV_K // 2 + k
        acc = acc + w_ref[k:k + 1, :] * buf_ref[start:start + t, :]
    o_ref[...] = acc * _sigmoid(acc)


def conv_silu(proj, conv_w, conv_b, n_batch, seq, ctx_len):
    rows = proj.shape[0]
    n_lat_tiles = n_batch * seq // ROW_TILE
    col0 = COL_XBC // CONV_COLS
    per = ROW_TILE // HALO
    last_halo = rows // HALO - 1
    kern = functools.partial(_conv_kernel, n_lat_tiles=n_lat_tiles,
                             lat_tiles_per_seq=seq // ROW_TILE, ctx_tiles_per_seq=ctx_len // ROW_TILE)
    w8 = jnp.concatenate([conv_w, jnp.zeros((8 - CONV_K, XBC_W), F32)], axis=0)
    return pl.pallas_call(
        kern,
        out_shape=jax.ShapeDtypeStruct((rows, XBC_W), F32),
        grid=(XBC_W // CONV_COLS, rows // ROW_TILE),
        in_specs=[pl.BlockSpec((HALO, CONV_COLS), lambda j, i: (jnp.maximum(i * per - 1, 0), col0 + j)),
                  pl.BlockSpec((ROW_TILE, CONV_COLS), lambda j, i: (i, col0 + j)),
                  pl.BlockSpec((HALO, CONV_COLS), lambda j, i: (jnp.minimum((i + 1) * per, last_halo), col0 + j)),
                  pl.BlockSpec((8, CONV_COLS), lambda j, i: (0, j)),
                  pl.BlockSpec((1, CONV_COLS), lambda j, i: (0, j))],
        out_specs=pl.BlockSpec((ROW_TILE, CONV_COLS), lambda j, i: (i, j)),
        scratch_shapes=[pltpu.VMEM((ROW_TILE + 2 * HALO, CONV_COLS), F32)],
        compiler_params=_params("arbitrary", "arbitrary"),
        name="conv_silu",
    )(proj, proj, proj, w8, conv_b.reshape(1, XBC_W))


def _softplus(x):
    return jnp.maximum(x, 0.0) + jnp.log(1.0 + jnp.exp(-jnp.abs(x)))


def _ssd_kernel(xs_ref, b_ref, c_ref, dt_ref, bias_ref, a_ref, dskip_ref, eh_ref, y_ref, h_ref, *, rev):
    @pl.when(pl.program_id(1) == 0)
    def _():
        h_ref[...] = jnp.zeros_like(h_ref)

    L = SSD_CHUNK
    gw = HEADS_PER_GROUP * SSM_HEAD_DIM
    off = N_SSM_HEADS if rev else 0
    row = lax.broadcasted_iota(jnp.int32, (L, L), 0)
    col = lax.broadcasted_iota(jnp.int32, (L, L), 1)
    causal = (col >= row) if rev else (col <= row)
    tmat = jnp.where(causal, 1.0, 0.0).astype(BF16)

    dt = _softplus(dt_ref[...] + bias_ref[...])
    a = dt * a_ref[...]
    acum = _dot_sel_lhs(tmat, a)
    acum_t = acum.T
    dt_t = dt.T
    eh = eh_ref[...]
    acum_x = _dot_sel_rhs(acum, eh)
    dt_x = _dot_sel_rhs(dt, eh)
    last = 0 if rev else L - 1
    e_t = jnp.exp(acum_x)
    e_last = e_t[last:last + 1, :]
    to_end = jnp.exp(acum_x[last:last + 1, :] - acum_x) * dt_x
    xs = xs_ref[...]
    xw = (xs * to_end).astype(BF16)
    xsb = xs.astype(BF16)

    for g in range(N_GROUPS):
        gs = slice(g * gw, (g + 1) * gw)
        bg = b_ref[:, g * D_STATE:(g + 1) * D_STATE]
        cgb = c_ref[:, g * D_STATE:(g + 1) * D_STATE].astype(BF16)
        cb = _dot_nt(cgb, bg.astype(BF16))
        hg = h_ref[:, gs]
        y_off = _dot(cgb, hg.astype(BF16)) * e_t[:, gs]
        ys = []
        for r in range(HEADS_PER_GROUP):
            hd = off + g * HEADS_PER_GROUP + r
            seg = acum[:, hd:hd + 1] - acum_t[hd:hd + 1, :]
            decay = jnp.exp(jnp.where(causal, seg, -jnp.inf))
            m = (cb * decay * dt_t[hd:hd + 1, :]).astype(BF16)
            c0 = g * gw + r * SSM_HEAD_DIM
            ys.append(_dot(m, xsb[:, c0:c0 + SSM_HEAD_DIM]))
        y = jnp.concatenate(ys, axis=1) + y_off
        if not rev:
            y = y + dskip_ref[:, gs] * xs[:, gs]
        y_ref[:, gs] = y
        h_ref[:, gs] = e_last[:, gs] * hg + _dot(bg.T.astype(BF16), xw[:, gs])


def ssd_scan(xbc, dt_raw, dt_bias, a_neg, dskip_x, n_batch, seq, ctx_len, rev):
    rows = xbc.shape[0]
    nc, ncc = seq // SSD_CHUNK, ctx_len // SSD_CHUNK
    lat_blocks = n_batch * nc

    def blk(b, j):
        cj = (ncc - 1 - j) if rev else j
        lj = (nc - 1 - (j - ncc)) if rev else (j - ncc)
        return jnp.where(j < ncc, lat_blocks + b * ncc + cj, b * nc + lj)

    off = N_SSM_HEADS if rev else 0
    hrow = lax.broadcasted_iota(jnp.int32, (LANES, D_INNER), 0)
    hcol = lax.broadcasted_iota(jnp.int32, (LANES, D_INNER), 1)
    eh = jnp.where(hrow == off + hcol // SSM_HEAD_DIM, 1.0, 0.0).astype(BF16)
    return pl.pallas_call(
        functools.partial(_ssd_kernel, rev=rev),
        out_shape=jax.ShapeDtypeStruct((rows, D_INNER), F32),
        grid=(n_batch, ncc + nc),
        in_specs=[pl.BlockSpec((SSD_CHUNK, D_INNER), lambda b, j: (blk(b, j), 0)),
                  pl.BlockSpec((SSD_CHUNK, BC_W), lambda b, j: (blk(b, j), D_INNER // BC_W)),
                  pl.BlockSpec((SSD_CHUNK, BC_W), lambda b, j: (blk(b, j), D_INNER // BC_W + 1)),
                  pl.BlockSpec((SSD_CHUNK, LANES), lambda b, j: (blk(b, j), 0)),
                  pl.BlockSpec((1, LANES), lambda b, j: (0, 0)),
                  pl.BlockSpec((1, LANES), lambda b, j: (0, 0)),
                  pl.BlockSpec((1, D_INNER), lambda b, j: (0, 0)),
                  pl.BlockSpec((LANES, D_INNER), lambda b, j: (0, 0))],
        out_specs=pl.BlockSpec((SSD_CHUNK, D_INNER), lambda b, j: (blk(b, j), 0)),
        scratch_shapes=[pltpu.VMEM((D_STATE, D_INNER), F32)],
        compiler_params=_params("arbitrary", "arbitrary"),
        name="ssd_scan_bwd" if rev else "ssd_scan_fwd",
    )(xbc, xbc, xbc, dt_raw, dt_bias, a_neg, dskip_x, eh)


def _rope_tables(n_batch, seq, ctx_len):
    inv_freq = ROPE_BASE ** (-jnp.arange(ROPE_FREQS, dtype=F32) / ROPE_FREQS)
    n_rows = seq // GRID_W
    r = jnp.repeat(jnp.arange(n_rows, dtype=F32), GRID_W)
    c = jnp.tile(jnp.arange(GRID_W, dtype=F32), n_rows)
    ar = r[:, None] * inv_freq
    ac = c[:, None] * inv_freq
    ang = jnp.concatenate([ar, ar, ac, ac], axis=-1)
    cos, sin = jnp.cos(ang), jnp.sin(ang)
    first_half = (jnp.arange(HEAD_DIM) % (2 * ROPE_FREQS)) < ROPE_FREQS
    sin_up = jnp.where(first_half, -sin, 0.0)
    sin_dn = jnp.where(first_half, 0.0, sin)
    n_ctx = n_batch * ctx_len

    def rows(t, fill):
        return jnp.concatenate([jnp.tile(t, (n_batch, 1)), jnp.full((n_ctx, HEAD_DIM), fill, F32)], axis=0)
    return rows(cos, 1.0), rows(sin_up, 0.0), rows(sin_dn, 0.0)


def _qk_prep_kernel(x_ref, w_ref, cos_ref, su_ref, sd_ref, o_ref, *, n_heads, scale):
    cos, su, sd = cos_ref[...], su_ref[...], sd_ref[...]
    w = w_ref[...]
    for h in range(n_heads):
        hs = slice(h * HEAD_DIM, (h + 1) * HEAD_DIM)
        x = x_ref[:, hs]
        n = x * lax.rsqrt(jnp.mean(x * x, axis=-1, keepdims=True) + NORM_EPS) * w
        y = n * cos + pltpu.roll(n, HEAD_DIM - ROPE_FREQS, 1) * su + pltpu.roll(n, ROPE_FREQS, 1) * sd
        if scale != 1.0:
            y = y * scale
        o_ref[:, hs] = y.astype(o_ref.dtype)


def qk_prep(proj, norm_w, tables, rows, col, width, scale, name):
    n_heads = width // HEAD_DIM
    tab_spec = pl.BlockSpec((ROW_TILE, HEAD_DIM), lambda i: (i, 0))
    return pl.pallas_call(
        functools.partial(_qk_prep_kernel, n_heads=n_heads, scale=scale),
        out_shape=jax.ShapeDtypeStruct((rows, width), BF16),
        grid=(rows // ROW_TILE,),
        in_specs=[pl.BlockSpec((ROW_TILE, width), lambda i: (i, col // width)),
                  pl.BlockSpec((1, HEAD_DIM), lambda i: (0, 0)),
                  tab_spec, tab_spec, tab_spec],
        out_specs=pl.BlockSpec((ROW_TILE, width), lambda i: (i, 0)),
        compiler_params=_params("arbitrary"),
        name=name,
    )(proj, norm_w.reshape(1, HEAD_DIM), *tables)


def _cast_kernel(x_ref, o_ref):
    o_ref[...] = x_ref[...].astype(o_ref.dtype)


def cast_cols(proj, col, width, name):
    rows = proj.shape[0]
    return pl.pallas_call(
        _cast_kernel,
        out_shape=jax.ShapeDtypeStruct((rows, width), BF16),
        grid=(rows // ROW_TILE,),
        in_specs=[pl.BlockSpec((ROW_TILE, width), lambda i: (i, col // width))],
        out_specs=pl.BlockSpec((ROW_TILE, width), lambda i: (i, 0)),
        compiler_params=_params("arbitrary"),
        name=name,
    )(proj)


def _attn_kernel(q_ref, kp_ref, kc_ref, kn_ref, vp_ref, vc_ref, vn_ref, kx_ref, vx_ref, sink_ref, o_ref, *,
                 n_blocks):
    i = pl.program_id(1)
    T = ATTN_BLOCK
    nq = Q_PER_KV * T
    qi = lax.broadcasted_iota(jnp.int32, (nq, 3 * T), 0) % T
    kj = lax.broadcasted_iota(jnp.int32, (nq, 3 * T), 1)
    lo = jnp.maximum(qi, jnp.where(i > 0, 0, T))
    hi = jnp.minimum(qi + 2 * WINDOW, jnp.where(i < n_blocks - 1, 3 * T - 1, 2 * T - 1))
    valid = (kj >= lo) & (kj <= hi)
    for h in range(N_KV_HEADS):
        hs = slice(h * HEAD_DIM, (h + 1) * HEAD_DIM)
        q = jnp.concatenate([q_ref[:, (h * Q_PER_KV + g) * HEAD_DIM:(h * Q_PER_KV + g + 1) * HEAD_DIM]
                             for g in range(Q_PER_KV)], axis=0)
        kb = jnp.concatenate([kp_ref[:, hs], kc_ref[:, hs], kn_ref[:, hs]], axis=0)
        vb = jnp.concatenate([vp_ref[:, hs], vc_ref[:, hs], vn_ref[:, hs]], axis=0)
        s_lat = jnp.where(valid, _dot_nt(q, kb), -jnp.inf)
        s_ctx = _dot_nt(q, kx_ref[:, hs])
        sk = jnp.concatenate([jnp.broadcast_to(sink_ref[0:1, h * Q_PER_KV + g:h * Q_PER_KV + g + 1], (T, 1))
                              for g in range(Q_PER_KV)], axis=0)
        mx = jnp.maximum(jnp.maximum(jnp.max(s_lat, axis=-1, keepdims=True),
                                     jnp.max(s_ctx, axis=-1, keepdims=True)), sk)
        p_lat = jnp.exp(s_lat - mx)
        p_ctx = jnp.exp(s_ctx - mx)
        denom = (jnp.sum(p_lat, axis=-1, keepdims=True) + jnp.sum(p_ctx, axis=-1, keepdims=True)
                 + jnp.exp(sk - mx))
        o = (_dot(p_lat.astype(BF16), vb) + _dot(p_ctx.astype(BF16), vx_ref[:, hs])) * (1.0 / denom)
        for g in range(Q_PER_KV):
            c0 = (h * Q_PER_KV + g) * HEAD_DIM
            o_ref[:, c0:c0 + HEAD_DIM] = o[g * T:(g + 1) * T, :].astype(o_ref.dtype)


def windowed_attention(qn, kn, vb, sink, n_batch, seq, ctx_len):
    nb = seq // ATTN_BLOCK
    ctx0 = n_batch * seq // ctx_len
    sink_row = jnp.zeros((1, LANES), F32).at[0, :N_Q_HEADS].set(sink)

    def kv_spec(d):
        return pl.BlockSpec((ATTN_BLOCK, KV_W), lambda b, i: (b * nb + jnp.clip(i + d, 0, nb - 1), 0))
    ctx_spec = pl.BlockSpec((ctx_len, KV_W), lambda b, i: (ctx0 + b, 0))
    return pl.pallas_call(
        functools.partial(_attn_kernel, n_blocks=nb),
        out_shape=jax.ShapeDtypeStruct((n_batch * seq, Q_W), BF16),
        grid=(n_batch, nb),
        in_specs=[pl.BlockSpec((ATTN_BLOCK, Q_W), lambda b, i: (b * nb + i, 0)),
                  kv_spec(-1), kv_spec(0), kv_spec(1), kv_spec(-1), kv_spec(0), kv_spec(1),
                  ctx_spec, ctx_spec,
                  pl.BlockSpec((1, LANES), lambda b, i: (0, 0))],
        out_specs=pl.BlockSpec((ATTN_BLOCK, Q_W), lambda b, i: (b * nb + i, 0)),
        compiler_params=_params("arbitrary", "arbitrary"),
        name="windowed_attention",
    )(qn, kn, kn, kn, vb, vb, vb, kn, vb, sink_row)


def _gate_norm_kernel(yf_ref, yb_ref, z_ref, w_ref, o_ref):
    gw = D_INNER // N_GROUPS
    for g in range(N_GROUPS):
        gs = slice(g * gw, (g + 1) * gw)
        z = z_ref[:, gs]
        s = (yf_ref[:, gs] + yb_ref[:, gs]) * (z * _sigmoid(z))
        n = s * lax.rsqrt(jnp.mean(s * s, axis=-1, keepdims=True) + NORM_EPS) * w_ref[:, gs]
        o_ref[:, gs] = n.astype(o_ref.dtype)


def gate_norm(y_f, y_b, proj, ssm_norm_w, rows):
    spec = pl.BlockSpec((ROW_TILE, D_INNER), lambda i: (i, 0))
    return pl.pallas_call(
        _gate_norm_kernel,
        out_shape=jax.ShapeDtypeStruct((rows, D_INNER), BF16),
        grid=(rows // ROW_TILE,),
        in_specs=[spec, spec, pl.BlockSpec((ROW_TILE, D_INNER), lambda i: (i, COL_Z // D_INNER)),
                  pl.BlockSpec((1, D_INNER), lambda i: (0, 0))],
        out_specs=spec,
        compiler_params=_params("arbitrary"),
        name="gate_norm",
    )(y_f, y_b, proj, ssm_norm_w.reshape(1, D_INNER))


def _merge_kernel(yg_ref, ya_ref, ws_ref, wa_ref, gs_ref, ga_ref, o_ref):
    o = (_sigmoid(gs_ref[...]) * _dot(yg_ref[...], ws_ref[...])
         + _sigmoid(ga_ref[...]) * _dot(ya_ref[...], wa_ref[...]))
    o_ref[...] = o.astype(o_ref.dtype)


def merge_branches(yg, ya, w_ssm_out, w_attn_out, proj, rows):
    tm, tn = 512, 512
    return pl.pallas_call(
        _merge_kernel,
        out_shape=jax.ShapeDtypeStruct((rows, D_MODEL), BF16),
        grid=(D_MODEL // tn, rows // tm),
        in_specs=[pl.BlockSpec((tm, D_INNER), lambda j, i: (i, 0)),
                  pl.BlockSpec((tm, Q_W), lambda j, i: (i, 0)),
                  pl.BlockSpec((D_INNER, tn), lambda j, i: (0, j)),
                  pl.BlockSpec((Q_W, tn), lambda j, i: (0, j)),
                  pl.BlockSpec((tm, tn), lambda j, i: (i, COL_GS // tn + j)),
                  pl.BlockSpec((tm, tn), lambda j, i: (i, COL_GA // tn + j))],
        out_specs=pl.BlockSpec((tm, tn), lambda j, i: (i, j)),
        compiler_params=_params("arbitrary", "arbitrary"),
        name="merge_branches",
    )(yg, ya, w_ssm_out, w_attn_out, proj, proj)


def _out_proj_kernel(m_ref, wo_ref, x_ref, mod_ref, nw_ref, rw_ref, rb_ref, x1_ref, h2_ref, lg_ref):
    x1 = x_ref[...] + mod_ref[0, 2:3, :] * _dot(m_ref[...], wo_ref[...])
    x1_ref[...] = x1
    n = x1 * lax.rsqrt(jnp.mean(x1 * x1, axis=-1, keepdims=True) + NORM_EPS) * nw_ref[...]
    h2 = n * (1.0 + mod_ref[0, 4:5, :]) + mod_ref[0, 3:4, :]
    h2_ref[...] = h2
    lg_ref[...] = _dot_f32(h2, rw_ref[...]) + rb_ref[...]


def out_proj_router(merged, w_o, xa, mod, norm2_w, router_w, router_b, n_batch, seq):
    rows = n_batch * seq
    n_tiles = rows // ROW_TILE
    rw = jnp.zeros((D_MODEL, LANES), F32).at[:, :N_EXPERTS].set(router_w)
    rb = jnp.zeros((1, LANES), F32).at[0, :N_EXPERTS].set(router_b)
    tile = pl.BlockSpec((ROW_TILE, D_MODEL), lambda i: (i, 0))
    return pl.pallas_call(
        _out_proj_kernel,
        out_shape=(jax.ShapeDtypeStruct((rows, D_MODEL), F32),
                   jax.ShapeDtypeStruct((rows, D_MODEL), F32),
                   jax.ShapeDtypeStruct((rows, LANES), F32)),
        grid=(n_tiles,),
        in_specs=[tile,
                  pl.BlockSpec((D_MODEL, D_MODEL), lambda i: (0, 0)),
                  tile,
                  pl.BlockSpec((1, 8, D_MODEL), _mod_row_map(n_tiles, seq // ROW_TILE, n_batch)),
                  pl.BlockSpec((1, D_MODEL), lambda i: (0, 0)),
                  pl.BlockSpec((D_MODEL, LANES), lambda i: (0, 0)),
                  pl.BlockSpec((1, LANES), lambda i: (0, 0))],
        out_specs=(tile, tile, pl.BlockSpec((ROW_TILE, LANES), lambda i: (i, 0))),
        compiler_params=_params("arbitrary"),
        name="out_proj_router",
    )(merged, w_o, xa, mod, norm2_w.reshape(1, D_MODEL), rw, rb)


def _route_kernel(lg_ref, o_ref, cnt_ref, carry_ref):
    @pl.when(pl.program_id(0) == 0)
    def _():
        carry_ref[...] = jnp.zeros_like(carry_ref)

    t = lg_ref.shape[0]
    lane = lax.broadcasted_iota(jnp.int32, (t, LANES), 1).astype(F32)
    work = jnp.where(lane < N_EXPERTS, lg_ref[...], -jnp.inf)
    vals, idxs = [], []
    for _ in range(TOP_K):
        m = jnp.max(work, axis=-1, keepdims=True)
        idx = jnp.min(jnp.where(work == m, lane, float(LANES)), axis=-1, keepdims=True)
        vals.append(m)
        idxs.append(idx)
        work = jnp.where(lane == idx, -jnp.inf, work)
    es = [jnp.exp(v - vals[0]) for v in vals]
    inv = 1.0 / (es[0] + es[1] + es[2] + es[3])
    onehot = jnp.zeros((t, LANES), F32)
    for idx in idxs:
        onehot = onehot + jnp.where(lane == idx, 1.0, 0.0)
    r = lax.broadcasted_iota(jnp.int32, (t, t), 0)
    c = lax.broadcasted_iota(jnp.int32, (t, t), 1)
    before = jnp.where(c < r, 1.0, 0.0).astype(BF16)
    excl = _dot(before, onehot.astype(BF16)) + carry_ref[...]
    out = jnp.zeros((t, LANES), F32)
    for k in range(TOP_K):
        rank = jnp.sum(jnp.where(lane == idxs[k], excl, 0.0), axis=-1, keepdims=True)
        out = jnp.where(lane == k, idxs[k], out)
        out = jnp.where(lane == TOP_K + k, es[k] * inv, out)
        out = jnp.where(lane == 2 * TOP_K + k, rank, out)
    o_ref[...] = out
    carry_ref[...] = carry_ref[...] + jnp.sum(onehot, axis=0, keepdims=True)
    cnt_ref[...] = carry_ref[...]


def route(logits):
    n_tok = logits.shape[0]
    return pl.pallas_call(
        _route_kernel,
        out_shape=(jax.ShapeDtypeStruct((n_tok, LANES), F32), jax.ShapeDtypeStruct((1, LANES), F32)),
        grid=(n_tok // ROW_TILE,),
        in_specs=[pl.BlockSpec((ROW_TILE, LANES), lambda i: (i, 0))],
        out_specs=(pl.BlockSpec((ROW_TILE, LANES), lambda i: (i, 0)),
                   pl.BlockSpec((1, LANES), lambda i: (0, 0))),
        scratch_shapes=[pltpu.VMEM((1, LANES), F32)],
        compiler_params=_params("arbitrary"),
        name="route",
    )(logits)


def _row_copy(src_ref, dst_ref, sem, src_row, dst_row):
    return pltpu.make_async_copy(src_ref.at[pl.ds(src_row, 1)], dst_ref.at[pl.ds(dst_row, 1)], sem)


DISPATCH_TILE = 128


def _dispatch_kernel(fill_ref, dest_ref, h_ref, o_hbm, zero_ref, fill_sem, sem):
    def fill_copy(blk):
        return pltpu.make_async_copy(zero_ref, o_hbm.at[pl.ds(blk * MOE_BLOCK, MOE_BLOCK)], fill_sem)

    @pl.when(pl.program_id(0) == 0)
    def _():
        zero_ref[...] = jnp.zeros_like(zero_ref)
        for s in range(fill_ref.shape[0]):
            @pl.when(fill_ref[s] >= 0)
            def _():
                fill_copy(fill_ref[s]).start()
        for s in range(fill_ref.shape[0]):
            @pl.when(fill_ref[s] >= 0)
            def _():
                fill_copy(0).wait()

    t = h_ref.shape[0]

    def start(r, carry):
        for k in range(TOP_K):
            _row_copy(h_ref, o_hbm, sem, r, dest_ref[0, 0, r * TOP_K + k]).start()
        return carry
    lax.fori_loop(0, t, start, 0)

    def wait(r, carry):
        for k in range(TOP_K):
            _row_copy(h_ref, o_hbm, sem, r, 0).wait()
        return carry
    lax.fori_loop(0, t, wait, 0)


def dispatch_rows(h2, dest, fill_blocks, cap):
    n_tok = h2.shape[0]
    t = DISPATCH_TILE
    n_tiles = n_tok // t
    grid_spec = pltpu.PrefetchScalarGridSpec(
        num_scalar_prefetch=1,
        grid=(n_tiles,),
        in_specs=[pl.BlockSpec((1, 1, t * TOP_K), lambda i, fb: (i, 0, 0), memory_space=pltpu.SMEM),
                  pl.BlockSpec((t, D_MODEL), lambda i, fb: (i, 0))],
        out_specs=pl.BlockSpec(memory_space=pl.ANY),
        scratch_shapes=[pltpu.VMEM((MOE_BLOCK, D_MODEL), F32), pltpu.SemaphoreType.DMA,
                        pltpu.SemaphoreType.DMA])
    return pl.pallas_call(
        _dispatch_kernel,
        out_shape=jax.ShapeDtypeStruct((cap, D_MODEL), F32),
        grid_spec=grid_spec,
        compiler_params=_params("arbitrary"),
        name="moe_dispatch",
    )(fill_blocks, dest.reshape(n_tiles, 1, t * TOP_K), h2)


def _expert_changed(be_ref, i):
    return (i == 0) | (be_ref[i] != be_ref[jnp.maximum(i - 1, 0)])


def _expert_up_kernel(be_ref, nu_ref, x_ref, wg_ref, wu_ref, bg_ref, bu_ref, o_ref, wgb_ref, wub_ref):
    i = pl.program_id(1)

    @pl.when(_expert_changed(be_ref, i))
    def _():
        wgb_ref[...] = wg_ref[...].astype(BF16)
        wub_ref[...] = wu_ref[...].astype(BF16)

    @pl.when(i < nu_ref[0])
    def _():
        x = x_ref[...].astype(BF16)
        gate = jnp.minimum(_dot(x, wgb_ref[...]) + bg_ref[0], SWIGLU_LIMIT)
        up = jnp.clip(_dot(x, wub_ref[...]) + bu_ref[0], -SWIGLU_LIMIT, SWIGLU_LIMIT)
        o_ref[...] = (gate * _sigmoid(SWIGLU_ALPHA * gate) * (up + 1.0)).astype(o_ref.dtype)

    @pl.when(i >= nu_ref[0])
    def _():
        o_ref[...] = jnp.zeros_like(o_ref)


def expert_up(xs, block_e, n_used, w_gate_up, b_gate_up):
    cap = xs.shape[0]
    tf = 1024
    nj = D_FF // tf
    b3 = b_gate_up.reshape(N_EXPERTS, 1, 2 * D_FF)
    grid_spec = pltpu.PrefetchScalarGridSpec(
        num_scalar_prefetch=2,
        grid=(nj, cap // MOE_BLOCK),
        in_specs=[pl.BlockSpec((MOE_BLOCK, D_MODEL), lambda j, i, be, nu: (i, 0)),
                  pl.BlockSpec((None, D_MODEL, tf), lambda j, i, be, nu: (be[i], 0, j)),
                  pl.BlockSpec((None, D_MODEL, tf), lambda j, i, be, nu: (be[i], 0, nj + j)),
                  pl.BlockSpec((None, 1, tf), lambda j, i, be, nu: (be[i], 0, j)),
                  pl.BlockSpec((None, 1, tf), lambda j, i, be, nu: (be[i], 0, nj + j))],
        out_specs=pl.BlockSpec((MOE_BLOCK, tf), lambda j, i, be, nu: (i, j)),
        scratch_shapes=[pltpu.VMEM((D_MODEL, tf), BF16), pltpu.VMEM((D_MODEL, tf), BF16)])
    return pl.pallas_call(
        _expert_up_kernel,
        out_shape=jax.ShapeDtypeStruct((cap, D_FF), BF16),
        grid_spec=grid_spec,
        compiler_params=_params("arbitrary", "arbitrary"),
        name="expert_up",
    )(block_e, n_used, xs, w_gate_up, w_gate_up, b3, b3)


def _expert_down_kernel(be_ref, nu_ref, a_ref, w_ref, b_ref, o_ref, wb_ref):
    i = pl.program_id(1)

    @pl.when(_expert_changed(be_ref, i))
    def _():
        wb_ref[...] = w_ref[...].astype(BF16)

    @pl.when(i < nu_ref[0])
    def _():
        o_ref[...] = _dot(a_ref[...], wb_ref[...]) + b_ref[0]

    @pl.when(i >= nu_ref[0])
    def _():
        o_ref[...] = jnp.zeros_like(o_ref)


def expert_down(act, block_e, n_used, w_down, b_down):
    cap = act.shape[0]
    tn = 1024
    grid_spec = pltpu.PrefetchScalarGridSpec(
        num_scalar_prefetch=2,
        grid=(D_MODEL // tn, cap // MOE_BLOCK),
        in_specs=[pl.BlockSpec((MOE_BLOCK, D_FF), lambda j, i, be, nu: (i, 0)),
                  pl.BlockSpec((None, D_FF, tn), lambda j, i, be, nu: (be[i], 0, j)),
                  pl.BlockSpec((None, 1, tn), lambda j, i, be, nu: (be[i], 0, j))],
        out_specs=pl.BlockSpec((MOE_BLOCK, tn), lambda j, i, be, nu: (i, j)),
        scratch_shapes=[pltpu.VMEM((D_FF, tn), BF16)])
    return pl.pallas_call(
        _expert_down_kernel,
        out_shape=jax.ShapeDtypeStruct((cap, D_MODEL), F32),
        grid_spec=grid_spec,
        compiler_params=_params("arbitrary", "arbitrary"),
        name="expert_down",
    )(block_e, n_used, act, w_down, b_down.reshape(N_EXPERTS, 1, D_MODEL))


COMBINE_TILE = 128


def _combine_kernel(dest_ref, yb_hbm, x1_ref, w_ref, mod_ref, o_ref, buf_ref, sem):
    t = COMBINE_TILE

    def start(r, carry):
        for k in range(TOP_K):
            _row_copy(yb_hbm, buf_ref.at[k], sem, dest_ref[0, 0, r * TOP_K + k], r).start()
        return carry
    lax.fori_loop(0, t, start, 0)

    def wait(r, carry):
        for k in range(TOP_K):
            _row_copy(yb_hbm, buf_ref.at[k], sem, 0, r).wait()
        return carry
    lax.fori_loop(0, t, wait, 0)
    acc = w_ref[:, TOP_K:TOP_K + 1] * buf_ref[0]
    for k in range(1, TOP_K):
        acc = acc + w_ref[:, TOP_K + k:TOP_K + k + 1] * buf_ref[k]
    o_ref[...] = x1_ref[...] + mod_ref[0, 5:6, :] * acc


def combine(yb, dest, x1, route_out, mod, n_batch, seq):
    rows = n_batch * seq
    t = COMBINE_TILE
    n_tiles = rows // t
    return pl.pallas_call(
        _combine_kernel,
        out_shape=jax.ShapeDtypeStruct((rows, D_MODEL), F32),
        grid=(n_tiles,),
        in_specs=[pl.BlockSpec((1, 1, t * TOP_K), lambda i: (i, 0, 0), memory_space=pltpu.SMEM),
                  pl.BlockSpec(memory_space=pl.ANY),
                  pl.BlockSpec((t, D_MODEL), lambda i: (i, 0)),
                  pl.BlockSpec((t, LANES), lambda i: (i, 0)),
                  pl.BlockSpec((1, 8, D_MODEL), _mod_row_map(n_tiles, seq // t, n_batch))],
        out_specs=pl.BlockSpec((t, D_MODEL), lambda i: (i, 0)),
        scratch_shapes=[pltpu.VMEM((TOP_K, t, D_MODEL), F32), pltpu.SemaphoreType.DMA],
        compiler_params=_params("arbitrary"),
        name="moe_combine",
    )(dest.reshape(n_tiles, 1, t * TOP_K), yb, x1, route_out, mod)


def moe_layout(route_out, counts):
    n_tok = route_out.shape[0]
    idx = route_out[:, :TOP_K].astype(jnp.int32)
    rank = route_out[:, 2 * TOP_K:3 * TOP_K].astype(jnp.int32)
    cnt = counts[0, :N_EXPERTS].astype(jnp.int32)
    padded = (cnt + MOE_BLOCK - 1) // MOE_BLOCK * MOE_BLOCK
    pad_end = jnp.cumsum(padded)
    pad_start = pad_end - padded
    dest = (pad_start[idx] + rank).reshape(-1)
    n_blocks = -(-(n_tok * TOP_K + N_EXPERTS * (MOE_BLOCK - 1)) // MOE_BLOCK)
    blk_row = jnp.arange(n_blocks, dtype=jnp.int32) * MOE_BLOCK
    block_e = jnp.minimum(jnp.sum(pad_end[None, :] <= blk_row[:, None], axis=1), N_EXPERTS - 1).astype(jnp.int32)
    n_used = pad_end[-1] // MOE_BLOCK
    last_blk = jnp.where(cnt > 0, pad_end // MOE_BLOCK - 1, -1)
    tail = n_used + jnp.arange(n_blocks - (n_tok * TOP_K) // MOE_BLOCK, dtype=jnp.int32)
    fill_blocks = jnp.concatenate([last_blk, jnp.where(tail < n_blocks, tail, -1)]).astype(jnp.int32)
    return dest, block_e, n_used.reshape(1).astype(jnp.int32), fill_blocks, n_blocks * MOE_BLOCK


def _in_proj_weights(w_in):
    sizes = (XBC_W, D_INNER, N_SSM_HEADS, N_SSM_HEADS, Q_W, KV_W, KV_W, D_MODEL, D_MODEL)
    offs = [0]
    for s in sizes:
        offs.append(offs[-1] + s)
    seg = lambda i: w_in[:, offs[i]:offs[i + 1]]
    w_main = jnp.concatenate([seg(1), seg(0), seg(4), seg(5), seg(6), seg(7), seg(8)], axis=1).astype(BF16)
    w_dt = jnp.concatenate([seg(2), seg(3)], axis=1).astype(BF16)
    return w_main, w_dt


def hybrid_layer(x, ctx, c, c_ctx, w_ada, b_ada, norm1_w, norm2_w, w_in, conv_w, conv_b, dt_bias_f, dt_bias_b,
                 a_log_f, a_log_b, d_skip, ssm_norm_w, q_norm_w, k_norm_w, sink, w_ssm_out, w_attn_out, w_o,
                 router_w, router_b, w_gate_up, b_gate_up, w_down, b_down):
    n_batch, seq, _ = x.shape
    ctx_len = ctx.shape[1]
    n_lat = n_batch * seq
    xa = jnp.concatenate([x.reshape(n_lat, D_MODEL), ctx.reshape(n_batch * ctx_len, D_MODEL)], axis=0)

    cvec = jnp.zeros((8, D_MODEL), F32).at[:n_batch].set(c).at[n_batch].set(c_ctx)
    mod = ada_modulation(cvec, w_ada, b_ada).reshape(8, 6, D_MODEL)
    mod = jnp.concatenate([mod, jnp.zeros((8, 2, D_MODEL), F32)], axis=1)

    hn = norm_modulate(xa, norm1_w, mod, n_batch, seq)
    w_main, w_dt = _in_proj_weights(w_in)
    proj = matmul(hn, w_main, 512, 1024, name="in_proj")
    dt_raw = matmul(hn, w_dt, 512, LANES, name="in_proj_dt")

    xbc = conv_silu(proj, conv_w, conv_b, n_batch, seq, ctx_len)
    dt_bias = jnp.concatenate([dt_bias_f, dt_bias_b]).reshape(1, LANES)
    a_neg = -jnp.exp(jnp.concatenate([a_log_f, a_log_b])).reshape(1, LANES)
    dskip_x = jnp.repeat(d_skip, SSM_HEAD_DIM).reshape(1, D_INNER)
    y_f = ssd_scan(xbc, dt_raw, dt_bias, a_neg, dskip_x, n_batch, seq, ctx_len, rev=False)
    y_b = ssd_scan(xbc, dt_raw, dt_bias, a_neg, dskip_x, n_batch, seq, ctx_len, rev=True)

    tables = _rope_tables(n_batch, seq, ctx_len)
    qn = qk_prep(proj, q_norm_w, tables, n_lat, COL_Q, Q_W, ATTN_SCALE, "q_prep")
    kn = qk_prep(proj, k_norm_w, tables, xa.shape[0], COL_K, KV_W, 1.0, "k_prep")
    vb = cast_cols(proj, COL_V, KV_W, "v_cast")
    y_attn = windowed_attention(qn, kn, vb, sink, n_batch, seq, ctx_len)

    yg = gate_norm(y_f, y_b, proj, ssm_norm_w, n_lat)
    merged = merge_branches(yg, y_attn, w_ssm_out.astype(BF16), w_attn_out.astype(BF16), proj, n_lat)
    x1, h2, logits = out_proj_router(merged, w_o.astype(BF16), xa, mod, norm2_w, router_w, router_b,
                                     n_batch, seq)

    route_out, counts = route(logits)
    dest, block_e, n_used, fill_blocks, cap = moe_layout(route_out, counts)
    xs = dispatch_rows(h2, dest, fill_blocks, cap)
    act = expert_up(xs, block_e, n_used, w_gate_up, b_gate_up)
    yb = expert_down(act, block_e, n_used, w_down, b_down)
    out = combine(yb, dest, x1, route_out, mod, n_batch, seq)
    return out.reshape(n_batch, seq, D_MODEL)


def kernel(x, c, ctx, c_ctx, w_ada, b_ada, norm1_w, norm2_w, w_in, conv_w, conv_b, dt_bias_f, dt_bias_b,
           a_log_f, a_log_b, d_skip, ssm_norm_w, q_norm_w, k_norm_w, sink, w_ssm_out, w_attn_out, w_o,
           router_w, router_b, w_gate_up, b_gate_up, w_down, b_down):
    assert w_ada.shape[0] == 1, "single-layer block"
    return hybrid_layer(x, ctx, c, c_ctx, w_ada[0], b_ada[0], norm1_w[0], norm2_w[0], w_in[0], conv_w[0],
                        conv_b[0], dt_bias_f[0], dt_bias_b[0], a_log_f[0], a_log_b[0], d_skip[0],
                        ssm_norm_w[0], q_norm_w[0], k_norm_w[0], sink[0], w_ssm_out[0], w_attn_out[0], w_o[0],
                        router_w[0], router_b[0], w_gate_up[0], b_gate_up[0], w_down[0], b_down[0])
```

```python
import functools
import math

import jax
import jax.numpy as jnp
from jax import lax
from jax.experimental import pallas as pl
from jax.experimental.pallas import tpu as pltpu

F32 = jnp.float32
BF16 = jnp.bfloat16

D_MODEL = 2048
GRID_W = 64
NORM_EPS = 1e-6
D_INNER = 2 * D_MODEL
SSM_HEAD_DIM = 64
N_SSM_HEADS = D_INNER // SSM_HEAD_DIM
N_GROUPS = 8
HEADS_PER_GROUP = N_SSM_HEADS // N_GROUPS
D_STATE = 128
BC_W = N_GROUPS * D_STATE
XBC_W = D_INNER + 2 * BC_W
CONV_K = 5
SSD_CHUNK = 128
HEAD_DIM = 128
N_Q_HEADS = D_MODEL // HEAD_DIM
N_KV_HEADS = 4
Q_PER_KV = N_Q_HEADS // N_KV_HEADS
Q_W = N_Q_HEADS * HEAD_DIM
KV_W = N_KV_HEADS * HEAD_DIM
WINDOW = 128
ATTN_BLOCK = 128
ATTN_SCALE = HEAD_DIM ** -0.5
ROPE_FREQS = HEAD_DIM // 4
ROPE_BASE = 10000.0
N_EXPERTS = 32
TOP_K = 4
D_FF = D_MODEL
SWIGLU_LIMIT = 7.0
SWIGLU_ALPHA = 1.702
MOE_BLOCK = 256

LANES = 128
ROW_TILE = 256
MAIN_W = D_INNER + XBC_W + Q_W + 2 * KV_W + 2 * D_MODEL
COL_Z, COL_XBC, COL_Q = 0, D_INNER, D_INNER + XBC_W
COL_K, COL_V = COL_Q + Q_W, COL_Q + Q_W + KV_W
COL_GS, COL_GA = COL_V + KV_W, COL_V + KV_W + D_MODEL
VMEM_LIMIT = 56 * 1024 * 1024


def _params(*sem):
    return pltpu.CompilerParams(dimension_semantics=sem, vmem_limit_bytes=VMEM_LIMIT)


def _dot(a, b):
    return jnp.dot(a, b, preferred_element_type=F32)


def _dot_nt(a, b):
    return lax.dot_general(a, b, (((1,), (1,)), ((), ())), preferred_element_type=F32)


def _split3(a):
    a1 = a.astype(BF16)
    r = a - a1.astype(F32)
    a2 = r.astype(BF16)
    a3 = (r - a2.astype(F32)).astype(BF16)
    return a1, a2, a3


def _dot_f32(a, b):
    a1, a2, _ = _split3(a)
    b1, b2, _ = _split3(b)
    return _dot(a1, b1) + (_dot(a1, b2) + _dot(a2, b1))


def _dot_sel_rhs(a, sel):
    a1, a2, a3 = _split3(a)
    return _dot(a1, sel) + (_dot(a2, sel) + _dot(a3, sel))


def _dot_sel_lhs(sel, a):
    a1, a2, a3 = _split3(a)
    return _dot(sel, a1) + (_dot(sel, a2) + _dot(sel, a3))


def _sigmoid(x):
    return 1.0 / (1.0 + jnp.exp(-x))


def _ada_kernel(c_ref, w_ref, b_ref, o_ref):
    c = c_ref[...]
    o_ref[...] = _dot_f32(c * _sigmoid(c), w_ref[...]) + b_ref[...]


def ada_modulation(cvec, w_ada, b_ada):
    n = w_ada.shape[1]
    tn = 1024
    return pl.pallas_call(
        _ada_kernel,
        out_shape=jax.ShapeDtypeStruct((8, n), F32),
        grid=(n // tn,),
        in_specs=[pl.BlockSpec((8, D_MODEL), lambda j: (0, 0)),
                  pl.BlockSpec((D_MODEL, tn), lambda j: (0, j)),
                  pl.BlockSpec((1, tn), lambda j: (0, j))],
        out_specs=pl.BlockSpec((8, tn), lambda j: (0, j)),
        compiler_params=_params("arbitrary"),
        name="ada_modulation",
    )(cvec, w_ada, b_ada.reshape(1, n))


def _norm_mod_kernel(x_ref, w_ref, mod_ref, o_ref):
    x = x_ref[...]
    y = x * lax.rsqrt(jnp.mean(x * x, axis=-1, keepdims=True) + NORM_EPS) * w_ref[...]
    o_ref[...] = (y * (1.0 + mod_ref[0, 1:2, :]) + mod_ref[0, 0:1, :]).astype(o_ref.dtype)


def _mod_row_map(n_lat_tiles, tiles_per_batch, n_batch):
    def index_map(i, *_):
        return (jnp.where(i < n_lat_tiles, i // tiles_per_batch, n_batch), 0, 0)
    return index_map


def norm_modulate(xa, norm_w, mod, n_batch, seq):
    rows = xa.shape[0]
    n_lat_tiles = n_batch * seq // ROW_TILE
    return pl.pallas_call(
        _norm_mod_kernel,
        out_shape=jax.ShapeDtypeStruct((rows, D_MODEL), BF16),
        grid=(rows // ROW_TILE,),
        in_specs=[pl.BlockSpec((ROW_TILE, D_MODEL), lambda i: (i, 0)),
                  pl.BlockSpec((1, D_MODEL), lambda i: (0, 0)),
                  pl.BlockSpec((1, 8, D_MODEL), _mod_row_map(n_lat_tiles, seq // ROW_TILE, n_batch))],
        out_specs=pl.BlockSpec((ROW_TILE, D_MODEL), lambda i: (i, 0)),
        compiler_params=_params("arbitrary"),
        name="norm_modulate",
    )(xa, norm_w.reshape(1, D_MODEL), mod)


def _mm_kernel(a_ref, b_ref, o_ref):
    o_ref[...] = _dot(a_ref[...], b_ref[...]).astype(o_ref.dtype)


def matmul(a, b, tm, tn, out_dtype=F32, name="matmul"):
    m, k = a.shape
    n = b.shape[1]
    return pl.pallas_call(
        _mm_kernel,
        out_shape=jax.ShapeDtypeStruct((m, n), out_dtype),
        grid=(n // tn, m // tm),
        in_specs=[pl.BlockSpec((tm, k), lambda j, i: (i, 0)),
                  pl.BlockSpec((k, tn), lambda j, i: (0, j))],
        out_specs=pl.BlockSpec((tm, tn), lambda j, i: (i, j)),
        compiler_params=_params("arbitrary", "arbitrary"),
        name=name,
    )(a, b)


CONV_COLS = 2048
HALO = 8


def _conv_kernel(prev_ref, cur_ref, next_ref, w_ref, b_ref, o_ref, buf_ref, *,
                 n_lat_tiles, lat_tiles_per_seq, ctx_tiles_per_seq):
    i = pl.program_id(1)
    in_lat = i < n_lat_tiles
    pos = jnp.where(in_lat, i % lat_tiles_per_seq, (i - n_lat_tiles) % ctx_tiles_per_seq)
    per_seq = jnp.where(in_lat, lat_tiles_per_seq, ctx_tiles_per_seq)
    has_prev = pos > 0
    has_next = pos < per_seq - 1
    t = cur_ref.shape[0]
    buf_ref[0:HALO, :] = jnp.where(has_prev, prev_ref[...], 0.0)
    buf_ref[HALO:HALO + t, :] = cur_ref[...]
    buf_ref[HALO + t:, :] = jnp.where(has_next, next_ref[...], 0.0)
    acc = jnp.broadcast_to(b_ref[...], (t, CONV_COLS))
    for k in range(CONV_K):
        start = HALO - CONV_K // 2 + k
        acc = acc + w_ref[k:k + 1, :] * buf_ref[start:start + t, :]
    o_ref[...] = acc * _sigmoid(acc)


def conv_silu(proj, conv_w, conv_b, n_batch, seq, ctx_len):
    rows = proj.shape[0]
    n_lat_tiles = n_batch * seq // ROW_TILE
    col0 = COL_XBC // CONV_COLS
    per = ROW_TILE // HALO
    last_halo = rows // HALO - 1
    kern = functools.partial(_conv_kernel, n_lat_tiles=n_lat_tiles,
                             lat_tiles_per_seq=seq // ROW_TILE, ctx_tiles_per_seq=ctx_len // ROW_TILE)
    w8 = jnp.concatenate([conv_w, jnp.zeros((8 - CONV_K, XBC_W), F32)], axis=0)
    return pl.pallas_call(
        kern,
        out_shape=jax.ShapeDtypeStruct((rows, XBC_W), F32),
        grid=(XBC_W // CONV_COLS, rows // ROW_TILE),
        in_specs=[pl.BlockSpec((HALO, CONV_COLS), lambda j, i: (jnp.maximum(i * per - 1, 0), col0 + j)),
                  pl.BlockSpec((ROW_TILE, CONV_COLS), lambda j, i: (i, col0 + j)),
                  pl.BlockSpec((HALO, CONV_COLS), lambda j, i: (jnp.minimum((i + 1) * per, last_halo), col0 + j)),
                  pl.BlockSpec((8, CONV_COLS), lambda j, i: (0, j)),
                  pl.BlockSpec((1, CONV_COLS), lambda j, i: (0, j))],
        out_specs=pl.BlockSpec((ROW_TILE, CONV_COLS), lambda j, i: (i, j)),
        scratch_shapes=[pltpu.VMEM((ROW_TILE + 2 * HALO, CONV_COLS), F32)],
        compiler_params=_params("arbitrary", "arbitrary"),
        name="conv_silu",
    )(proj, proj, proj, w8, conv_b.reshape(1, XBC_W))


def _softplus(x):
    return jnp.maximum(x, 0.0) + jnp.log(1.0 + jnp.exp(-jnp.abs(x)))


def _ssd_kernel(xs_ref, b_ref, c_ref, dt_ref, bias_ref, a_ref, dskip_ref, y_ref, h_ref, *, rev):
    @pl.when(pl.program_id(1) == 0)
    def _():
        h_ref[...] = jnp.zeros_like(h_ref)

    L = SSD_CHUNK
    gw = HEADS_PER_GROUP * SSM_HEAD_DIM
    off = N_SSM_HEADS if rev else 0
    row = lax.broadcasted_iota(jnp.int32, (L, L), 0)
    col = lax.broadcasted_iota(jnp.int32, (L, L), 1)
    causal = (col >= row) if rev else (col <= row)
    first_head = col < SSM_HEAD_DIM
    tmat = jnp.where(causal, 1.0, 0.0).astype(BF16)

    dt = _softplus(dt_ref[...] + bias_ref[...])
    a = dt * a_ref[...]
    acum = _dot_sel_lhs(tmat, a)
    acum_t = acum.T
    dt_t = dt.T
    last = 0 if rev else L - 1
    to_end_t = jnp.exp(acum_t[:, last:last + 1] - acum_t) * dt_t
    src_t = acum_t - jnp.log(dt_t)

    for g in range(N_GROUPS):
        bg = b_ref[:, g * D_STATE:(g + 1) * D_STATE]
        cgb = c_ref[:, g * D_STATE:(g + 1) * D_STATE].astype(BF16)
        cb = _dot_nt(cgb, bg.astype(BF16))
        bg_t = bg.T
        y_off = _dot(cgb, h_ref[:, g * gw:(g + 1) * gw].astype(BF16))
        for pair in range(HEADS_PER_GROUP // 2):
            c0 = g * gw + pair * LANES
            x_f = xs_ref[:, c0:c0 + LANES]
            x2 = jnp.concatenate([jnp.where(first_head, x_f, 0.0).astype(BF16),
                                  jnp.where(first_head, 0.0, x_f).astype(BF16)], axis=0)
            ms, ws, bcs = [], [], []
            for k in range(2):
                hd = off + g * HEADS_PER_GROUP + 2 * pair + k
                bc = jnp.broadcast_to(acum[:, hd:hd + 1], (L, L))
                decay_dt = jnp.exp(jnp.where(causal, bc - src_t[hd:hd + 1, :], -jnp.inf))
                ms.append((cb * decay_dt).astype(BF16))
                ws.append((bg_t * to_end_t[hd:hd + 1, :]).astype(BF16))
                bcs.append(bc)
            lhs = jnp.concatenate([jnp.concatenate(ms, axis=1), jnp.concatenate(ws, axis=1)], axis=0)
            res = _dot(lhs, x2)
            e_t = jnp.exp(jnp.where(first_head, bcs[0], bcs[1]))
            y = res[:L] + y_off[:, pair * LANES:(pair + 1) * LANES] * e_t
            if not rev:
                y = y + dskip_ref[:, c0:c0 + LANES] * x_f
            y_ref[:, c0:c0 + LANES] = y
            h_ref[:, c0:c0 + LANES] = e_t[last:last + 1, :] * h_ref[:, c0:c0 + LANES] + res[L:]


def ssd_scan(xbc, dt_raw, dt_bias, a_neg, dskip_x, n_batch, seq, ctx_len, rev):
    rows = xbc.shape[0]
    nc, ncc = seq // SSD_CHUNK, ctx_len // SSD_CHUNK
    lat_blocks = n_batch * nc

    def blk(b, j):
        cj = (ncc - 1 - j) if rev else j
        lj = (nc - 1 - (j - ncc)) if rev else (j - ncc)
        return jnp.where(j < ncc, lat_blocks + b * ncc + cj, b * nc + lj)

    return pl.pallas_call(
        functools.partial(_ssd_kernel, rev=rev),
        out_shape=jax.ShapeDtypeStruct((rows, D_INNER), F32),
        grid=(n_batch, ncc + nc),
        in_specs=[pl.BlockSpec((SSD_CHUNK, D_INNER), lambda b, j: (blk(b, j), 0)),
                  pl.BlockSpec((SSD_CHUNK, BC_W), lambda b, j: (blk(b, j), D_INNER // BC_W)),
                  pl.BlockSpec((SSD_CHUNK, BC_W), lambda b, j: (blk(b, j), D_INNER // BC_W + 1)),
                  pl.BlockSpec((SSD_CHUNK, LANES), lambda b, j: (blk(b, j), 0)),
                  pl.BlockSpec((1, LANES), lambda b, j: (0, 0)),
                  pl.BlockSpec((1, LANES), lambda b, j: (0, 0)),
                  pl.BlockSpec((1, D_INNER), lambda b, j: (0, 0))],
        out_specs=pl.BlockSpec((SSD_CHUNK, D_INNER), lambda b, j: (blk(b, j), 0)),
        scratch_shapes=[pltpu.VMEM((D_STATE, D_INNER), F32)],
        compiler_params=_params("arbitrary", "arbitrary"),
        name="ssd_scan_bwd" if rev else "ssd_scan_fwd",
    )(xbc, xbc, xbc, dt_raw, dt_bias, a_neg, dskip_x)


def _rope_tables(n_batch, seq, ctx_len):
    inv_freq = ROPE_BASE ** (-jnp.arange(ROPE_FREQS, dtype=F32) / ROPE_FREQS)
    n_rows = seq // GRID_W
    r = jnp.repeat(jnp.arange(n_rows, dtype=F32), GRID_W)
    c = jnp.tile(jnp.arange(GRID_W, dtype=F32), n_rows)
    ar = r[:, None] * inv_freq
    ac = c[:, None] * inv_freq
    ang = jnp.concatenate([ar, ar, ac, ac], axis=-1)
    cos, sin = jnp.cos(ang), jnp.sin(ang)
    first_half = (jnp.arange(HEAD_DIM) % (2 * ROPE_FREQS)) < ROPE_FREQS
    sin_up = jnp.where(first_half, -sin, 0.0)
    sin_dn = jnp.where(first_half, 0.0, sin)
    n_ctx = n_batch * ctx_len

    def rows(t, fill):
        return jnp.concatenate([jnp.tile(t, (n_batch, 1)), jnp.full((n_ctx, HEAD_DIM), fill, F32)], axis=0)
    return rows(cos, 1.0), rows(sin_up, 0.0), rows(sin_dn, 0.0)


def _qk_prep_kernel(x_ref, w_ref, cos_ref, su_ref, sd_ref, o_ref, *, n_heads, scale):
    cos, su, sd = cos_ref[...], su_ref[...], sd_ref[...]
    w = w_ref[...]
    for h in range(n_heads):
        hs = slice(h * HEAD_DIM, (h + 1) * HEAD_DIM)
        x = x_ref[:, hs]
        n = x * lax.rsqrt(jnp.mean(x * x, axis=-1, keepdims=True) + NORM_EPS) * w
        y = n * cos + pltpu.roll(n, HEAD_DIM - ROPE_FREQS, 1) * su + pltpu.roll(n, ROPE_FREQS, 1) * sd
        if scale != 1.0:
            y = y * scale
        o_ref[:, hs] = y.astype(o_ref.dtype)


def qk_prep(proj, norm_w, tables, rows, col, width, scale, name):
    n_heads = width // HEAD_DIM
    tab_spec = pl.BlockSpec((ROW_TILE, HEAD_DIM), lambda i: (i, 0))
    return pl.pallas_call(
        functools.partial(_qk_prep_kernel, n_heads=n_heads, scale=scale),
        out_shape=jax.ShapeDtypeStruct((rows, width), BF16),
        grid=(rows // ROW_TILE,),
        in_specs=[pl.BlockSpec((ROW_TILE, width), lambda i: (i, col // width)),
                  pl.BlockSpec((1, HEAD_DIM), lambda i: (0, 0)),
                  tab_spec, tab_spec, tab_spec],
        out_specs=pl.BlockSpec((ROW_TILE, width), lambda i: (i, 0)),
        compiler_params=_params("arbitrary"),
        name=name,
    )(proj, norm_w.reshape(1, HEAD_DIM), *tables)


def _cast_kernel(x_ref, o_ref):
    o_ref[...] = x_ref[...].astype(o_ref.dtype)


def cast_cols(proj, col, width, name):
    rows = proj.shape[0]
    return pl.pallas_call(
        _cast_kernel,
        out_shape=jax.ShapeDtypeStruct((rows, width), BF16),
        grid=(rows // ROW_TILE,),
        in_specs=[pl.BlockSpec((ROW_TILE, width), lambda i: (i, col // width))],
        out_specs=pl.BlockSpec((ROW_TILE, width), lambda i: (i, 0)),
        compiler_params=_params("arbitrary"),
        name=name,
    )(proj)


def _attn_kernel(q_ref, kp_ref, kc_ref, kn_ref, vp_ref, vc_ref, vn_ref, kx_ref, vx_ref, sink_ref, o_ref, *,
                 n_blocks):
    i = pl.program_id(1)
    T = ATTN_BLOCK
    nq = Q_PER_KV * T
    qi = lax.broadcasted_iota(jnp.int32, (nq, 3 * T), 0) % T
    kj = lax.broadcasted_iota(jnp.int32, (nq, 3 * T), 1)
    lo = jnp.maximum(qi, jnp.where(i > 0, 0, T))
    hi = jnp.minimum(qi + 2 * WINDOW, jnp.where(i < n_blocks - 1, 3 * T - 1, 2 * T - 1))
    valid = (kj >= lo) & (kj <= hi)
    for h in range(N_KV_HEADS):
        hs = slice(h * HEAD_DIM, (h + 1) * HEAD_DIM)
        q = jnp.concatenate([q_ref[:, (h * Q_PER_KV + g) * HEAD_DIM:(h * Q_PER_KV + g + 1) * HEAD_DIM]
                             for g in range(Q_PER_KV)], axis=0)
        kb = jnp.concatenate([kp_ref[:, hs], kc_ref[:, hs], kn_ref[:, hs]], axis=0)
        vb = jnp.concatenate([vp_ref[:, hs], vc_ref[:, hs], vn_ref[:, hs]], axis=0)
        s_lat = jnp.where(valid, _dot_nt(q, kb), -jnp.inf)
        s_ctx = _dot_nt(q, kx_ref[:, hs])
        sk = jnp.concatenate([jnp.broadcast_to(sink_ref[0:1, h * Q_PER_KV + g:h * Q_PER_KV + g + 1], (T, 1))
                              for g in range(Q_PER_KV)], axis=0)
        mx = jnp.maximum(jnp.maximum(jnp.max(s_lat, axis=-1, keepdims=True),
                                     jnp.max(s_ctx, axis=-1, keepdims=True)), sk)
        p_lat = jnp.exp(s_lat - mx)
        p_ctx = jnp.exp(s_ctx - mx)
        denom = (jnp.sum(p_lat, axis=-1, keepdims=True) + jnp.sum(p_ctx, axis=-1, keepdims=True)
                 + jnp.exp(sk - mx))
        o = (_dot(p_lat.astype(BF16), vb) + _dot(p_ctx.astype(BF16), vx_ref[:, hs])) * (1.0 / denom)
        for g in range(Q_PER_KV):
            c0 = (h * Q_PER_KV + g) * HEAD_DIM
            o_ref[:, c0:c0 + HEAD_DIM] = o[g * T:(g + 1) * T, :].astype(o_ref.dtype)


def windowed_attention(qn, kn, vb, sink, n_batch, seq, ctx_len):
    nb = seq // ATTN_BLOCK
    ctx0 = n_batch * seq // ctx_len
    sink_row = jnp.zeros((1, LANES), F32).at[0, :N_Q_HEADS].set(sink)

    def kv_spec(d):
        return pl.BlockSpec((ATTN_BLOCK, KV_W), lambda b, i: (b * nb + jnp.clip(i + d, 0, nb - 1), 0))
    ctx_spec = pl.BlockSpec((ctx_len, KV_W), lambda b, i: (ctx0 + b, 0))
    return pl.pallas_call(
        functools.partial(_attn_kernel, n_blocks=nb),
        out_shape=jax.ShapeDtypeStruct((n_batch * seq, Q_W), BF16),
        grid=(n_batch, nb),
        in_specs=[pl.BlockSpec((ATTN_BLOCK, Q_W), lambda b, i: (b * nb + i, 0)),
                  kv_spec(-1), kv_spec(0), kv_spec(1), kv_spec(-1), kv_spec(0), kv_spec(1),
                  ctx_spec, ctx_spec,
                  pl.BlockSpec((1, LANES), lambda b, i: (0, 0))],
        out_specs=pl.BlockSpec((ATTN_BLOCK, Q_W), lambda b, i: (b * nb + i, 0)),
        compiler_params=_params("arbitrary", "arbitrary"),
        name="windowed_attention",
    )(qn, kn, kn, kn, vb, vb, vb, kn, vb, sink_row)


def _gate_norm_kernel(yf_ref, yb_ref, z_ref, w_ref, o_ref):
    gw = D_INNER // N_GROUPS
    for g in range(N_GROUPS):
        gs = slice(g * gw, (g + 1) * gw)
        z = z_ref[:, gs]
        s = (yf_ref[:, gs] + yb_ref[:, gs]) * (z * _sigmoid(z))
        n = s * lax.rsqrt(jnp.mean(s * s, axis=-1, keepdims=True) + NORM_EPS) * w_ref[:, gs]
        o_ref[:, gs] = n.astype(o_ref.dtype)


def gate_norm(y_f, y_b, proj, ssm_norm_w, rows):
    spec = pl.BlockSpec((ROW_TILE, D_INNER), lambda i: (i, 0))
    return pl.pallas_call(
        _gate_norm_kernel,
        out_shape=jax.ShapeDtypeStruct((rows, D_INNER), BF16),
        grid=(rows // ROW_TILE,),
        in_specs=[spec, spec, pl.BlockSpec((ROW_TILE, D_INNER), lambda i: (i, COL_Z // D_INNER)),
                  pl.BlockSpec((1, D_INNER), lambda i: (0, 0))],
        out_specs=spec,
        compiler_params=_params("arbitrary"),
        name="gate_norm",
    )(y_f, y_b, proj, ssm_norm_w.reshape(1, D_INNER))


def _merge_kernel(yg_ref, ya_ref, ws_ref, wa_ref, gs_ref, ga_ref, o_ref):
    o = (_sigmoid(gs_ref[...]) * _dot(yg_ref[...], ws_ref[...])
         + _sigmoid(ga_ref[...]) * _dot(ya_ref[...], wa_ref[...]))
    o_ref[...] = o.astype(o_ref.dtype)


def merge_branches(yg, ya, w_ssm_out, w_attn_out, proj, rows):
    tm, tn = 512, 512
    return pl.pallas_call(
        _merge_kernel,
        out_shape=jax.ShapeDtypeStruct((rows, D_MODEL), BF16),
        grid=(D_MODEL // tn, rows // tm),
        in_specs=[pl.BlockSpec((tm, D_INNER), lambda j, i: (i, 0)),
                  pl.BlockSpec((tm, Q_W), lambda j, i: (i, 0)),
                  pl.BlockSpec((D_INNER, tn), lambda j, i: (0, j)),
                  pl.BlockSpec((Q_W, tn), lambda j, i: (0, j)),
                  pl.BlockSpec((tm, tn), lambda j, i: (i, COL_GS // tn + j)),
                  pl.BlockSpec((tm, tn), lambda j, i: (i, COL_GA // tn + j))],
        out_specs=pl.BlockSpec((tm, tn), lambda j, i: (i, j)),
        compiler_params=_params("arbitrary", "arbitrary"),
        name="merge_branches",
    )(yg, ya, w_ssm_out, w_attn_out, proj, proj)


def _out_proj_kernel(m_ref, wo_ref, x_ref, mod_ref, nw_ref, rw_ref, rb_ref, x1_ref, h2_ref, lg_ref):
    x1 = x_ref[...] + mod_ref[0, 2:3, :] * _dot(m_ref[...], wo_ref[...])
    x1_ref[...] = x1
    n = x1 * lax.rsqrt(jnp.mean(x1 * x1, axis=-1, keepdims=True) + NORM_EPS) * nw_ref[...]
    h2 = n * (1.0 + mod_ref[0, 4:5, :]) + mod_ref[0, 3:4, :]
    h2_ref[...] = h2
    lg_ref[...] = _dot_f32(h2, rw_ref[...]) + rb_ref[...]


def out_proj_router(merged, w_o, xa, mod, norm2_w, router_w, router_b, n_batch, seq):
    rows = n_batch * seq
    n_tiles = rows // ROW_TILE
    rw = jnp.zeros((D_MODEL, LANES), F32).at[:, :N_EXPERTS].set(router_w)
    rb = jnp.zeros((1, LANES), F32).at[0, :N_EXPERTS].set(router_b)
    tile = pl.BlockSpec((ROW_TILE, D_MODEL), lambda i: (i, 0))
    return pl.pallas_call(
        _out_proj_kernel,
        out_shape=(jax.ShapeDtypeStruct((rows, D_MODEL), F32),
                   jax.ShapeDtypeStruct((rows, D_MODEL), F32),
                   jax.ShapeDtypeStruct((rows, LANES), F32)),
        grid=(n_tiles,),
        in_specs=[tile,
                  pl.BlockSpec((D_MODEL, D_MODEL), lambda i: (0, 0)),
                  tile,
                  pl.BlockSpec((1, 8, D_MODEL), _mod_row_map(n_tiles, seq // ROW_TILE, n_batch)),
                  pl.BlockSpec((1, D_MODEL), lambda i: (0, 0)),
                  pl.BlockSpec((D_MODEL, LANES), lambda i: (0, 0)),
                  pl.BlockSpec((1, LANES), lambda i: (0, 0))],
        out_specs=(tile, tile, pl.BlockSpec((ROW_TILE, LANES), lambda i: (i, 0))),
        compiler_params=_params("arbitrary"),
        name="out_proj_router",
    )(merged, w_o, xa, mod, norm2_w.reshape(1, D_MODEL), rw, rb)


def _route_kernel(lg_ref, o_ref, cnt_ref, carry_ref):
    @pl.when(pl.program_id(0) == 0)
    def _():
        carry_ref[...] = jnp.zeros_like(carry_ref)

    t = lg_ref.shape[0]
    lane = lax.broadcasted_iota(jnp.int32, (t, LANES), 1).astype(F32)
    work = jnp.where(lane < N_EXPERTS, lg_ref[...], -jnp.inf)
    vals, idxs = [], []
    for _ in range(TOP_K):
        m = jnp.max(work, axis=-1, keepdims=True)
        idx = jnp.min(jnp.where(work == m, lane, float(LANES)), axis=-1, keepdims=True)
        vals.append(m)
        idxs.append(idx)
        work = jnp.where(lane == idx, -jnp.inf, work)
    es = [jnp.exp(v - vals[0]) for v in vals]
    inv = 1.0 / (es[0] + es[1] + es[2] + es[3])
    onehot = jnp.zeros((t, LANES), F32)
    for idx in idxs:
        onehot = onehot + jnp.where(lane == idx, 1.0, 0.0)
    r = lax.broadcasted_iota(jnp.int32, (t, t), 0)
    c = lax.broadcasted_iota(jnp.int32, (t, t), 1)
    before = jnp.where(c < r, 1.0, 0.0).astype(BF16)
    excl = _dot(before, onehot.astype(BF16)) + carry_ref[...]
    out = jnp.zeros((t, LANES), F32)
    for k in range(TOP_K):
        rank = jnp.sum(jnp.where(lane == idxs[k], excl, 0.0), axis=-1, keepdims=True)
        out = jnp.where(lane == k, idxs[k], out)
        out = jnp.where(lane == TOP_K + k, es[k] * inv, out)
        out = jnp.where(lane == 2 * TOP_K + k, rank, out)
    o_ref[...] = out
    carry_ref[...] = carry_ref[...] + jnp.sum(onehot, axis=0, keepdims=True)
    cnt_ref[...] = carry_ref[...]


def route(logits):
    n_tok = logits.shape[0]
    return pl.pallas_call(
        _route_kernel,
        out_shape=(jax.ShapeDtypeStruct((n_tok, LANES), F32), jax.ShapeDtypeStruct((1, LANES), F32)),
        grid=(n_tok // ROW_TILE,),
        in_specs=[pl.BlockSpec((ROW_TILE, LANES), lambda i: (i, 0))],
        out_specs=(pl.BlockSpec((ROW_TILE, LANES), lambda i: (i, 0)),
                   pl.BlockSpec((1, LANES), lambda i: (0, 0))),
        scratch_shapes=[pltpu.VMEM((1, LANES), F32)],
        compiler_params=_params("arbitrary"),
        name="route",
    )(logits)


def _row_copy(src_ref, dst_ref, sem, src_row, dst_row):
    return pltpu.make_async_copy(src_ref.at[pl.ds(src_row, 1)], dst_ref.at[pl.ds(dst_row, 1)], sem)


DISPATCH_TILE = 128


def _dispatch_kernel(fill_ref, dest_ref, h_ref, o_hbm, zero_ref, fill_sem, sem):
    def fill_copy(blk):
        return pltpu.make_async_copy(zero_ref, o_hbm.at[pl.ds(blk * MOE_BLOCK, MOE_BLOCK)], fill_sem)

    @pl.when(pl.program_id(0) == 0)
    def _():
        zero_ref[...] = jnp.zeros_like(zero_ref)
        for s in range(fill_ref.shape[0]):
            @pl.when(fill_ref[s] >= 0)
            def _():
                fill_copy(fill_ref[s]).start()
        for s in range(fill_ref.shape[0]):
            @pl.when(fill_ref[s] >= 0)
            def _():
                fill_copy(0).wait()

    t = h_ref.shape[0]

    def start(r, carry):
        for k in range(TOP_K):
            _row_copy(h_ref, o_hbm, sem, r, dest_ref[0, 0, r * TOP_K + k]).start()
        return carry
    lax.fori_loop(0, t, start, 0)

    def wait(r, carry):
        for k in range(TOP_K):
            _row_copy(h_ref, o_hbm, sem, r, 0).wait()
        return carry
    lax.fori_loop(0, t, wait, 0)


def dispatch_rows(h2, dest, fill_blocks, cap):
    n_tok = h2.shape[0]
    t = DISPATCH_TILE
    n_tiles = n_tok // t
    grid_spec = pltpu.PrefetchScalarGridSpec(
        num_scalar_prefetch=1,
        grid=(n_tiles,),
        in_specs=[pl.BlockSpec((1, 1, t * TOP_K), lambda i, fb: (i, 0, 0), memory_space=pltpu.SMEM),
                  pl.BlockSpec((t, D_MODEL), lambda i, fb: (i, 0))],
        out_specs=pl.BlockSpec(memory_space=pl.ANY),
        scratch_shapes=[pltpu.VMEM((MOE_BLOCK, D_MODEL), F32), pltpu.SemaphoreType.DMA,
                        pltpu.SemaphoreType.DMA])
    return pl.pallas_call(
        _dispatch_kernel,
        out_shape=jax.ShapeDtypeStruct((cap, D_MODEL), F32),
        grid_spec=grid_spec,
        compiler_params=_params("arbitrary"),
        name="moe_dispatch",
    )(fill_blocks, dest.reshape(n_tiles, 1, t * TOP_K), h2)


EXPERT_COLS = 1024


def _expert_rows_pipeline(e, col, blk0_ref, nblk_ref, tail_ref, x_hbm, o_hbm, xbuf, obuf, zbuf,
                          sem_in, sem_out, sem_tail, compute, tail_max):
    n = nblk_ref[e]
    b0 = blk0_ref[e]
    width = obuf.shape[2]

    def rows(blk):
        return pl.ds(pl.multiple_of((b0 + blk) * MOE_BLOCK, MOE_BLOCK), MOE_BLOCK)

    def x_copy(blk, slot):
        return pltpu.make_async_copy(x_hbm.at[rows(blk)], xbuf.at[slot], sem_in.at[slot])

    def o_copy(blk, slot):
        return pltpu.make_async_copy(obuf.at[slot], o_hbm.at[rows(blk), pl.ds(col, width)], sem_out.at[slot])

    @pl.when(n > 0)
    def _():
        x_copy(0, 0).start()

        def body(blk, carry):
            slot = lax.rem(blk, 2)
            x_copy(blk, slot).wait()

            @pl.when(blk + 1 < n)
            def _():
                x_copy(blk + 1, 1 - slot).start()

            @pl.when(blk >= 2)
            def _():
                o_copy(blk - 2, slot).wait()
            obuf[slot] = compute(xbuf[slot]).astype(obuf.dtype)
            o_copy(blk, slot).start()
            return carry
        lax.fori_loop(0, n, body, 0)

        @pl.when(n >= 2)
        def _():
            o_copy(n - 2, lax.rem(n, 2)).wait()
        o_copy(n - 1, lax.rem(n - 1, 2)).wait()

    @pl.when(e == N_EXPERTS - 1)
    def _():
        zbuf[...] = jnp.zeros_like(zbuf)
        first, count = tail_ref[0], tail_ref[1]

        def z_copy(t):
            dst = pl.ds(pl.multiple_of((first + t) * MOE_BLOCK, MOE_BLOCK), MOE_BLOCK)
            return pltpu.make_async_copy(zbuf, o_hbm.at[dst, pl.ds(col, width)], sem_tail)
        for t in range(tail_max):
            @pl.when(t < count)
            def _():
                z_copy(t).start()
        for t in range(tail_max):
            @pl.when(t < count)
            def _():
                z_copy(t).wait()


def _expert_up_kernel(blk0_ref, nblk_ref, tail_ref, x_hbm, wg_ref, wu_ref, bg_ref, bu_ref, o_hbm,
                      xbuf, obuf, zbuf, wgb_ref, wub_ref, sem_in, sem_out, sem_tail, *, tail_max):
    j, e = pl.program_id(0), pl.program_id(1)

    @pl.when(nblk_ref[e] > 0)
    def _():
        wgb_ref[...] = wg_ref[...].astype(BF16)
        wub_ref[...] = wu_ref[...].astype(BF16)

    def compute(x):
        xb = x.astype(BF16)
        gate = jnp.minimum(_dot(xb, wgb_ref[...]) + bg_ref[0], SWIGLU_LIMIT)
        up = jnp.clip(_dot(xb, wub_ref[...]) + bu_ref[0], -SWIGLU_LIMIT, SWIGLU_LIMIT)
        return gate * _sigmoid(SWIGLU_ALPHA * gate) * (up + 1.0)
    _expert_rows_pipeline(e, pl.multiple_of(j * EXPERT_COLS, EXPERT_COLS), blk0_ref, nblk_ref, tail_ref,
                          x_hbm, o_hbm, xbuf, obuf, zbuf, sem_in, sem_out, sem_tail, compute, tail_max)


def _expert_scratch(in_width, in_dtype, out_dtype):
    return [pltpu.VMEM((2, MOE_BLOCK, in_width), in_dtype),
            pltpu.VMEM((2, MOE_BLOCK, EXPERT_COLS), out_dtype),
            pltpu.VMEM((MOE_BLOCK, EXPERT_COLS), out_dtype)]


_EXPERT_SEMS = [pltpu.SemaphoreType.DMA((2,)), pltpu.SemaphoreType.DMA((2,)), pltpu.SemaphoreType.DMA]


def expert_up(xs, blk0, nblk, tail, tail_max, w_gate_up, b_gate_up):
    cap = xs.shape[0]
    tf = EXPERT_COLS
    nj = D_FF // tf
    b3 = b_gate_up.reshape(N_EXPERTS, 1, 2 * D_FF)
    grid_spec = pltpu.PrefetchScalarGridSpec(
        num_scalar_prefetch=3,
        grid=(nj, N_EXPERTS),
        in_specs=[pl.BlockSpec(memory_space=pl.ANY),
                  pl.BlockSpec((None, D_MODEL, tf), lambda j, e, *_: (e, 0, j)),
                  pl.BlockSpec((None, D_MODEL, tf), lambda j, e, *_: (e, 0, nj + j)),
                  pl.BlockSpec((None, 1, tf), lambda j, e, *_: (e, 0, j)),
                  pl.BlockSpec((None, 1, tf), lambda j, e, *_: (e, 0, nj + j))],
        out_specs=pl.BlockSpec(memory_space=pl.ANY),
        scratch_shapes=_expert_scratch(D_MODEL, F32, BF16)
        + [pltpu.VMEM((D_MODEL, tf), BF16), pltpu.VMEM((D_MODEL, tf), BF16)] + _EXPERT_SEMS)
    return pl.pallas_call(
        functools.partial(_expert_up_kernel, tail_max=tail_max),
        out_shape=jax.ShapeDtypeStruct((cap, D_FF), BF16),
        grid_spec=grid_spec,
        compiler_params=_params("arbitrary", "arbitrary"),
        name="expert_up",
    )(blk0, nblk, tail, xs, w_gate_up, w_gate_up, b3, b3)


def _expert_down_kernel(blk0_ref, nblk_ref, tail_ref, a_hbm, w_ref, b_ref, o_hbm,
                        xbuf, obuf, zbuf, wb_ref, sem_in, sem_out, sem_tail, *, tail_max):
    j, e = pl.program_id(0), pl.program_id(1)

    @pl.when(nblk_ref[e] > 0)
    def _():
        wb_ref[...] = w_ref[...].astype(BF16)

    def compute(a):
        return _dot(a, wb_ref[...]) + b_ref[0]
    _expert_rows_pipeline(e, pl.multiple_of(j * EXPERT_COLS, EXPERT_COLS), blk0_ref, nblk_ref, tail_ref,
                          a_hbm, o_hbm, xbuf, obuf, zbuf, sem_in, sem_out, sem_tail, compute, tail_max)


def expert_down(act, blk0, nblk, tail, tail_max, w_down, b_down):
    cap = act.shape[0]
    tn = EXPERT_COLS
    grid_spec = pltpu.PrefetchScalarGridSpec(
        num_scalar_prefetch=3,
        grid=(D_MODEL // tn, N_EXPERTS),
        in_specs=[pl.BlockSpec(memory_space=pl.ANY),
                  pl.BlockSpec((None, D_FF, tn), lambda j, e, *_: (e, 0, j)),
                  pl.BlockSpec((None, 1, tn), lambda j, e, *_: (e, 0, j))],
        out_specs=pl.BlockSpec(memory_space=pl.ANY),
        scratch_shapes=_expert_scratch(D_FF, BF16, F32) + [pltpu.VMEM((D_FF, tn), BF16)] + _EXPERT_SEMS)
    return pl.pallas_call(
        functools.partial(_expert_down_kernel, tail_max=tail_max),
        out_shape=jax.ShapeDtypeStruct((cap, D_MODEL), F32),
        grid_spec=grid_spec,
        compiler_params=_params("arbitrary", "arbitrary"),
        name="expert_down",
    )(blk0, nblk, tail, act, w_down, b_down.reshape(N_EXPERTS, 1, D_MODEL))


COMBINE_TILE = 128


def _combine_kernel(dest_ref, yb_hbm, x1_ref, w_ref, mod_ref, o_ref, buf_ref, sem):
    t = COMBINE_TILE

    def start(r, carry):
        for k in range(TOP_K):
            _row_copy(yb_hbm, buf_ref.at[k], sem, dest_ref[0, 0, r * TOP_K + k], r).start()
        return carry
    lax.fori_loop(0, t, start, 0)

    def wait(r, carry):
        for k in range(TOP_K):
            _row_copy(yb_hbm, buf_ref.at[k], sem, 0, r).wait()
        return carry
    lax.fori_loop(0, t, wait, 0)
    acc = w_ref[:, TOP_K:TOP_K + 1] * buf_ref[0]
    for k in range(1, TOP_K):
        acc = acc + w_ref[:, TOP_K + k:TOP_K + k + 1] * buf_ref[k]
    o_ref[...] = x1_ref[...] + mod_ref[0, 5:6, :] * acc


def combine(yb, dest, x1, route_out, mod, n_batch, seq):
    rows = n_batch * seq
    t = COMBINE_TILE
    n_tiles = rows // t
    return pl.pallas_call(
        _combine_kernel,
        out_shape=jax.ShapeDtypeStruct((rows, D_MODEL), F32),
        grid=(n_tiles,),
        in_specs=[pl.BlockSpec((1, 1, t * TOP_K), lambda i: (i, 0, 0), memory_space=pltpu.SMEM),
                  pl.BlockSpec(memory_space=pl.ANY),
                  pl.BlockSpec((t, D_MODEL), lambda i: (i, 0)),
                  pl.BlockSpec((t, LANES), lambda i: (i, 0)),
                  pl.BlockSpec((1, 8, D_MODEL), _mod_row_map(n_tiles, seq // t, n_batch))],
        out_specs=pl.BlockSpec((t, D_MODEL), lambda i: (i, 0)),
        scratch_shapes=[pltpu.VMEM((TOP_K, t, D_MODEL), F32), pltpu.SemaphoreType.DMA],
        compiler_params=_params("arbitrary"),
        name="moe_combine",
    )(dest.reshape(n_tiles, 1, t * TOP_K), yb, x1, route_out, mod)


def moe_layout(route_out, counts):
    n_tok = route_out.shape[0]
    idx = route_out[:, :TOP_K].astype(jnp.int32)
    rank = route_out[:, 2 * TOP_K:3 * TOP_K].astype(jnp.int32)
    cnt = counts[0, :N_EXPERTS].astype(jnp.int32)
    padded = (cnt + MOE_BLOCK - 1) // MOE_BLOCK * MOE_BLOCK
    pad_end = jnp.cumsum(padded)
    pad_start = pad_end - padded
    dest = (pad_start[idx] + rank).reshape(-1)
    n_blocks = -(-(n_tok * TOP_K + N_EXPERTS * (MOE_BLOCK - 1)) // MOE_BLOCK)
    n_used = pad_end[-1] // MOE_BLOCK
    tail_max = n_blocks - (n_tok * TOP_K) // MOE_BLOCK
    tail = jnp.stack([n_used, n_blocks - n_used]).astype(jnp.int32)
    last_blk = jnp.where(cnt > 0, pad_end // MOE_BLOCK - 1, -1)
    tail_blk = n_used + jnp.arange(tail_max, dtype=jnp.int32)
    fill_blocks = jnp.concatenate([last_blk, jnp.where(tail_blk < n_blocks, tail_blk, -1)]).astype(jnp.int32)
    blk0 = (pad_start // MOE_BLOCK).astype(jnp.int32)
    nblk = (padded // MOE_BLOCK).astype(jnp.int32)
    return dest, blk0, nblk, tail, tail_max, fill_blocks, n_blocks * MOE_BLOCK


def _in_proj_weights(w_in):
    sizes = (XBC_W, D_INNER, N_SSM_HEADS, N_SSM_HEADS, Q_W, KV_W, KV_W, D_MODEL, D_MODEL)
    offs = [0]
    for s in sizes:
        offs.append(offs[-1] + s)
    seg = lambda i: w_in[:, offs[i]:offs[i + 1]]
    w_main = jnp.concatenate([seg(1), seg(0), seg(4), seg(5), seg(6), seg(7), seg(8)], axis=1).astype(BF16)
    w_dt = jnp.concatenate([seg(2), seg(3)], axis=1).astype(BF16)
    return w_main, w_dt


def hybrid_layer(x, ctx, c, c_ctx, w_ada, b_ada, norm1_w, norm2_w, w_in, conv_w, conv_b, dt_bias_f, dt_bias_b,
                 a_log_f, a_log_b, d_skip, ssm_norm_w, q_norm_w, k_norm_w, sink, w_ssm_out, w_attn_out, w_o,
                 router_w, router_b, w_gate_up, b_gate_up, w_down, b_down):
    n_batch, seq, _ = x.shape
    ctx_len = ctx.shape[1]
    n_lat = n_batch * seq
    xa = jnp.concatenate([x.reshape(n_lat, D_MODEL), ctx.reshape(n_batch * ctx_len, D_MODEL)], axis=0)

    cvec = jnp.zeros((8, D_MODEL), F32).at[:n_batch].set(c).at[n_batch].set(c_ctx)
    mod = ada_modulation(cvec, w_ada, b_ada).reshape(8, 6, D_MODEL)
    mod = jnp.concatenate([mod, jnp.zeros((8, 2, D_MODEL), F32)], axis=1)

    hn = norm_modulate(xa, norm1_w, mod, n_batch, seq)
    w_main, w_dt = _in_proj_weights(w_in)
    proj = matmul(hn, w_main, 512, 1024, name="in_proj")
    dt_raw = matmul(hn, w_dt, 512, LANES, name="in_proj_dt")

    xbc = conv_silu(proj, conv_w, conv_b, n_batch, seq, ctx_len)
    dt_bias = jnp.concatenate([dt_bias_f, dt_bias_b]).reshape(1, LANES)
    a_neg = -jnp.exp(jnp.concatenate([a_log_f, a_log_b])).reshape(1, LANES)
    dskip_x = jnp.repeat(d_skip, SSM_HEAD_DIM).reshape(1, D_INNER)
    y_f = ssd_scan(xbc, dt_raw, dt_bias, a_neg, dskip_x, n_batch, seq, ctx_len, rev=False)
    y_b = ssd_scan(xbc, dt_raw, dt_bias, a_neg, dskip_x, n_batch, seq, ctx_len, rev=True)

    tables = _rope_tables(n_batch, seq, ctx_len)
    qn = qk_prep(proj, q_norm_w, tables, n_lat, COL_Q, Q_W, ATTN_SCALE, "q_prep")
    kn = qk_prep(proj, k_norm_w, tables, xa.shape[0], COL_K, KV_W, 1.0, "k_prep")
    vb = cast_cols(proj, COL_V, KV_W, "v_cast")
    y_attn = windowed_attention(qn, kn, vb, sink, n_batch, seq, ctx_len)

    yg = gate_norm(y_f, y_b, proj, ssm_norm_w, n_lat)
    merged = merge_branches(yg, y_attn, w_ssm_out.astype(BF16), w_attn_out.astype(BF16), proj, n_lat)
    x1, h2, logits = out_proj_router(merged, w_o.astype(BF16), xa, mod, norm2_w, router_w, router_b,
                                     n_batch, seq)

    route_out, counts = route(logits)
    dest, blk0, nblk, tail, tail_max, fill_blocks, cap = moe_layout(route_out, counts)
    xs = dispatch_rows(h2, dest, fill_blocks, cap)
    act = expert_up(xs, blk0, nblk, tail, tail_max, w_gate_up, b_gate_up)
    yb = expert_down(act, blk0, nblk, tail, tail_max, w_down, b_down)
    out = combine(yb, dest, x1, route_out, mod, n_batch, seq)
    return out.reshape(n_batch, seq, D_MODEL)


def kernel(x, c, ctx, c_ctx, w_ada, b_ada, norm1_w, norm2_w, w_in, conv_w, conv_b, dt_bias_f, dt_bias_b,
           a_log_f, a_log_b, d_skip, ssm_norm_w, q_norm_w, k_norm_w, sink, w_ssm_out, w_attn_out, w_o,
           router_w, router_b, w_gate_up, b_gate_up, w_down, b_down):
    assert w_ada.shape[0] == 1, "single-layer block"
    return hybrid_layer(x, ctx, c, c_ctx, w_ada[0], b_ada[0], norm1_w[0], norm2_w[0], w_in[0], conv_w[0],
                        conv_b[0], dt_bias_f[0], dt_bias_b[0], a_log_f[0], a_log_b[0], d_skip[0],
                        ssm_norm_w[0], q_norm_w[0], k_norm_w[0], sink[0], w_ssm_out[0], w_attn_out[0], w_o[0],
                        router_w[0], router_b[0], w_gate_up[0], b_gate_up[0], w_down[0], b_down[0])
```

```python
import functools
import math

import jax
import jax.numpy as jnp
from jax import lax
from jax.experimental import pallas as pl
from jax.experimental.pallas import tpu as pltpu

F32 = jnp.float32
BF16 = jnp.bfloat16

D_MODEL = 2048
GRID_W = 64
NORM_EPS = 1e-6
D_INNER = 2 * D_MODEL
SSM_HEAD_DIM = 64
N_SSM_HEADS = D_INNER // SSM_HEAD_DIM
N_GROUPS = 8
HEADS_PER_GROUP = N_SSM_HEADS // N_GROUPS
D_STATE = 128
BC_W = N_GROUPS * D_STATE
XBC_W = D_INNER + 2 * BC_W
CONV_K = 5
SSD_CHUNK = 128
HEAD_DIM = 128
N_Q_HEADS = D_MODEL // HEAD_DIM
N_KV_HEADS = 4
Q_PER_KV = N_Q_HEADS // N_KV_HEADS
Q_W = N_Q_HEADS * HEAD_DIM
KV_W = N_KV_HEADS * HEAD_DIM
WINDOW = 128
ATTN_BLOCK = 128
ATTN_SCALE = HEAD_DIM ** -0.5
ROPE_FREQS = HEAD_DIM // 4
ROPE_BASE = 10000.0
N_EXPERTS = 32
TOP_K = 4
D_FF = D_MODEL
SWIGLU_LIMIT = 7.0
SWIGLU_ALPHA = 1.702
MOE_BLOCK = 512

LANES = 128
ROW_TILE = 256
MAIN_W = D_INNER + XBC_W + Q_W + 2 * KV_W + 2 * D_MODEL
COL_Z, COL_XBC, COL_Q = 0, D_INNER, D_INNER + XBC_W
COL_K, COL_V = COL_Q + Q_W, COL_Q + Q_W + KV_W
COL_GS, COL_GA = COL_V + KV_W, COL_V + KV_W + D_MODEL
VMEM_LIMIT = 56 * 1024 * 1024


def _params(*sem, vmem_limit=VMEM_LIMIT):
    return pltpu.CompilerParams(dimension_semantics=sem, vmem_limit_bytes=vmem_limit)


def _dot(a, b):
    return jnp.dot(a, b, preferred_element_type=F32)


def _dot_nt(a, b):
    return lax.dot_general(a, b, (((1,), (1,)), ((), ())), preferred_element_type=F32)


def _split3(a):
    a1 = a.astype(BF16)
    r = a - a1.astype(F32)
    a2 = r.astype(BF16)
    a3 = (r - a2.astype(F32)).astype(BF16)
    return a1, a2, a3


def _dot_f32(a, b):
    a1, a2, _ = _split3(a)
    b1, b2, _ = _split3(b)
    return _dot(a1, b1) + (_dot(a1, b2) + _dot(a2, b1))


def _dot_sel_rhs(a, sel):
    a1, a2, a3 = _split3(a)
    return _dot(a1, sel) + (_dot(a2, sel) + _dot(a3, sel))


def _dot_sel_lhs(sel, a):
    a1, a2, a3 = _split3(a)
    return _dot(sel, a1) + (_dot(sel, a2) + _dot(sel, a3))


def _sigmoid(x):
    return 1.0 / (1.0 + jnp.exp(-x))


def _ada_kernel(c_ref, w_ref, b_ref, o_ref):
    c = c_ref[...]
    o_ref[...] = _dot_f32(c * _sigmoid(c), w_ref[...]) + b_ref[...]


def ada_modulation(cvec, w_ada, b_ada):
    n = w_ada.shape[1]
    tn = 1024
    return pl.pallas_call(
        _ada_kernel,
        out_shape=jax.ShapeDtypeStruct((8, n), F32),
        grid=(n // tn,),
        in_specs=[pl.BlockSpec((8, D_MODEL), lambda j: (0, 0)),
                  pl.BlockSpec((D_MODEL, tn), lambda j: (0, j)),
                  pl.BlockSpec((1, tn), lambda j: (0, j))],
        out_specs=pl.BlockSpec((8, tn), lambda j: (0, j)),
        compiler_params=_params("arbitrary"),
        name="ada_modulation",
    )(cvec, w_ada, b_ada.reshape(1, n))


def _norm_mod_kernel(x_ref, c_ref, w_ref, mod_ref, o_ref, *, n_lat_tiles):
    def emit(x):
        y = x * lax.rsqrt(jnp.mean(x * x, axis=-1, keepdims=True) + NORM_EPS) * w_ref[...]
        o_ref[...] = (y * (1.0 + mod_ref[0, 1:2, :]) + mod_ref[0, 0:1, :]).astype(o_ref.dtype)

    @pl.when(pl.program_id(0) < n_lat_tiles)
    def _():
        emit(x_ref[...])

    @pl.when(pl.program_id(0) >= n_lat_tiles)
    def _():
        emit(c_ref[...])


def _mod_row_map(n_lat_tiles, tiles_per_batch, n_batch):
    def index_map(i, *_):
        return (jnp.where(i < n_lat_tiles, i // tiles_per_batch, n_batch), 0, 0)
    return index_map


def norm_modulate(x2, ctx2, norm_w, mod, n_batch, seq):
    rows = x2.shape[0] + ctx2.shape[0]
    n_lat_tiles = n_batch * seq // ROW_TILE
    return pl.pallas_call(
        functools.partial(_norm_mod_kernel, n_lat_tiles=n_lat_tiles),
        out_shape=jax.ShapeDtypeStruct((rows, D_MODEL), BF16),
        grid=(rows // ROW_TILE,),
        in_specs=[pl.BlockSpec((ROW_TILE, D_MODEL), lambda i: (jnp.minimum(i, n_lat_tiles - 1), 0)),
                  pl.BlockSpec((ROW_TILE, D_MODEL), lambda i: (jnp.maximum(i - n_lat_tiles, 0), 0)),
                  pl.BlockSpec((1, D_MODEL), lambda i: (0, 0)),
                  pl.BlockSpec((1, 8, D_MODEL), _mod_row_map(n_lat_tiles, seq // ROW_TILE, n_batch))],
        out_specs=pl.BlockSpec((ROW_TILE, D_MODEL), lambda i: (i, 0)),
        compiler_params=_params("arbitrary"),
        name="norm_modulate",
    )(x2, ctx2, norm_w.reshape(1, D_MODEL), mod)


def _mm_kernel(a_ref, b_ref, o_ref):
    o_ref[...] = _dot(a_ref[...], b_ref[...]).astype(o_ref.dtype)


def matmul(a, b, tm, tn, out_dtype=F32, name="matmul"):
    m, k = a.shape
    n = b.shape[1]
    return pl.pallas_call(
        _mm_kernel,
        out_shape=jax.ShapeDtypeStruct((m, n), out_dtype),
        grid=(n // tn, m // tm),
        in_specs=[pl.BlockSpec((tm, k), lambda j, i: (i, 0)),
                  pl.BlockSpec((k, tn), lambda j, i: (0, j))],
        out_specs=pl.BlockSpec((tm, tn), lambda j, i: (i, j)),
        compiler_params=_params("arbitrary", "arbitrary"),
        name=name,
    )(a, b)


CONV_COLS = 2048
HALO = 8


def _conv_kernel(prev_ref, cur_ref, next_ref, w_ref, b_ref, o_ref, *,
                 n_lat_tiles, lat_tiles_per_seq, ctx_tiles_per_seq):
    i = pl.program_id(1)
    in_lat = i < n_lat_tiles
    pos = jnp.where(in_lat, i % lat_tiles_per_seq, (i - n_lat_tiles) % ctx_tiles_per_seq)
    per_seq = jnp.where(in_lat, lat_tiles_per_seq, ctx_tiles_per_seq)
    has_prev = pos > 0
    has_next = pos < per_seq - 1
    t = cur_ref.shape[0]
    n = t + 2 * HALO
    full = jnp.concatenate([jnp.where(has_prev, prev_ref[...], 0.0), cur_ref[...],
                            jnp.where(has_next, next_ref[...], 0.0)], axis=0)
    acc = jnp.broadcast_to(b_ref[...], (t, CONV_COLS))
    for k in range(CONV_K):
        shift = CONV_K // 2 - k
        shifted = full if shift == 0 else pltpu.roll(full, shift % n, 0)
        acc = acc + w_ref[k:k + 1, :] * shifted[HALO:HALO + t, :]
    o_ref[...] = acc * _sigmoid(acc)


def conv_silu(proj, conv_w, conv_b, n_batch, seq, ctx_len):
    rows = proj.shape[0]
    n_lat_tiles = n_batch * seq // ROW_TILE
    col0 = COL_XBC // CONV_COLS
    per = ROW_TILE // HALO
    last_halo = rows // HALO - 1
    kern = functools.partial(_conv_kernel, n_lat_tiles=n_lat_tiles,
                             lat_tiles_per_seq=seq // ROW_TILE, ctx_tiles_per_seq=ctx_len // ROW_TILE)
    w8 = jnp.concatenate([conv_w, jnp.zeros((8 - CONV_K, XBC_W), F32)], axis=0)
    return pl.pallas_call(
        kern,
        out_shape=jax.ShapeDtypeStruct((rows, XBC_W), F32),
        grid=(XBC_W // CONV_COLS, rows // ROW_TILE),
        in_specs=[pl.BlockSpec((HALO, CONV_COLS), lambda j, i: (jnp.maximum(i * per - 1, 0), col0 + j)),
                  pl.BlockSpec((ROW_TILE, CONV_COLS), lambda j, i: (i, col0 + j)),
                  pl.BlockSpec((HALO, CONV_COLS), lambda j, i: (jnp.minimum((i + 1) * per, last_halo), col0 + j)),
                  pl.BlockSpec((8, CONV_COLS), lambda j, i: (0, j)),
                  pl.BlockSpec((1, CONV_COLS), lambda j, i: (0, j))],
        out_specs=pl.BlockSpec((ROW_TILE, CONV_COLS), lambda j, i: (i, j)),
        compiler_params=_params("arbitrary", "arbitrary"),
        name="conv_silu",
    )(proj, proj, proj, w8, conv_b.reshape(1, XBC_W))


def _softplus(x):
    return jnp.maximum(x, 0.0) + jnp.log(1.0 + jnp.exp(-jnp.abs(x)))


def _ssd_kernel(xs_ref, b_ref, c_ref, dt_ref, bias_ref, a_ref, dskip_ref, y_ref, h_ref, *, rev):
    @pl.when(pl.program_id(1) == 0)
    def _():
        h_ref[...] = jnp.zeros_like(h_ref)

    L = SSD_CHUNK
    gw = HEADS_PER_GROUP * SSM_HEAD_DIM
    off = N_SSM_HEADS if rev else 0
    row = lax.broadcasted_iota(jnp.int32, (L, L), 0)
    col = lax.broadcasted_iota(jnp.int32, (L, L), 1)
    causal = (col >= row) if rev else (col <= row)
    first_head = col < SSM_HEAD_DIM
    tmat = jnp.where(causal, 1.0, 0.0).astype(BF16)

    dt = _softplus(dt_ref[...] + bias_ref[...])
    a = dt * a_ref[...]
    acum = _dot_sel_lhs(tmat, a)
    acum_t = acum.T
    dt_t = dt.T
    last = 0 if rev else L - 1
    to_end_t = jnp.exp(acum_t[:, last:last + 1] - acum_t) * dt_t
    src_t = acum_t - jnp.log(dt_t)

    for g in range(N_GROUPS):
        bg = b_ref[:, g * D_STATE:(g + 1) * D_STATE]
        cgb = c_ref[:, g * D_STATE:(g + 1) * D_STATE].astype(BF16)
        cb = _dot_nt(cgb, bg.astype(BF16))
        bg_t = bg.T
        y_off = _dot(cgb, h_ref[:, g * gw:(g + 1) * gw].astype(BF16))
        for pair in range(HEADS_PER_GROUP // 2):
            c0 = g * gw + pair * LANES
            x_f = xs_ref[:, c0:c0 + LANES]
            x2 = jnp.concatenate([jnp.where(first_head, x_f, 0.0).astype(BF16),
                                  jnp.where(first_head, 0.0, x_f).astype(BF16)], axis=0)
            ms, ws, bcs = [], [], []
            for k in range(2):
                hd = off + g * HEADS_PER_GROUP + 2 * pair + k
                bc = jnp.broadcast_to(acum[:, hd:hd + 1], (L, L))
                decay_dt = jnp.exp(jnp.where(causal, bc - src_t[hd:hd + 1, :], -jnp.inf))
                ms.append((cb * decay_dt).astype(BF16))
                ws.append((bg_t * to_end_t[hd:hd + 1, :]).astype(BF16))
                bcs.append(bc)
            lhs = jnp.concatenate([jnp.concatenate(ms, axis=1), jnp.concatenate(ws, axis=1)], axis=0)
            res = _dot(lhs, x2)
            e_t = jnp.exp(jnp.where(first_head, bcs[0], bcs[1]))
            y = res[:L] + y_off[:, pair * LANES:(pair + 1) * LANES] * e_t
            if not rev:
                y = y + dskip_ref[:, c0:c0 + LANES] * x_f
            y_ref[:, c0:c0 + LANES] = y
            h_ref[:, c0:c0 + LANES] = e_t[last:last + 1, :] * h_ref[:, c0:c0 + LANES] + res[L:]


def ssd_scan(xbc, dt_raw, dt_bias, a_neg, dskip_x, n_batch, seq, ctx_len, rev):
    rows = xbc.shape[0]
    nc, ncc = seq // SSD_CHUNK, ctx_len // SSD_CHUNK
    lat_blocks = n_batch * nc

    def blk(b, j):
        cj = (ncc - 1 - j) if rev else j
        lj = (nc - 1 - (j - ncc)) if rev else (j - ncc)
        return jnp.where(j < ncc, lat_blocks + b * ncc + cj, b * nc + lj)

    return pl.pallas_call(
        functools.partial(_ssd_kernel, rev=rev),
        out_shape=jax.ShapeDtypeStruct((rows, D_INNER), F32),
        grid=(n_batch, ncc + nc),
        in_specs=[pl.BlockSpec((SSD_CHUNK, D_INNER), lambda b, j: (blk(b, j), 0)),
                  pl.BlockSpec((SSD_CHUNK, BC_W), lambda b, j: (blk(b, j), D_INNER // BC_W)),
                  pl.BlockSpec((SSD_CHUNK, BC_W), lambda b, j: (blk(b, j), D_INNER // BC_W + 1)),
                  pl.BlockSpec((SSD_CHUNK, LANES), lambda b, j: (blk(b, j), 0)),
                  pl.BlockSpec((1, LANES), lambda b, j: (0, 0)),
                  pl.BlockSpec((1, LANES), lambda b, j: (0, 0)),
                  pl.BlockSpec((1, D_INNER), lambda b, j: (0, 0))],
        out_specs=pl.BlockSpec((SSD_CHUNK, D_INNER), lambda b, j: (blk(b, j), 0)),
        scratch_shapes=[pltpu.VMEM((D_STATE, D_INNER), F32)],
        compiler_params=_params("arbitrary", "arbitrary"),
        name="ssd_scan_bwd" if rev else "ssd_scan_fwd",
    )(xbc, xbc, xbc, dt_raw, dt_bias, a_neg, dskip_x)


def _rope_tables(n_batch, seq, ctx_len):
    inv_freq = ROPE_BASE ** (-jnp.arange(ROPE_FREQS, dtype=F32) / ROPE_FREQS)
    n_rows = seq // GRID_W
    r = jnp.repeat(jnp.arange(n_rows, dtype=F32), GRID_W)
    c = jnp.tile(jnp.arange(GRID_W, dtype=F32), n_rows)
    ar = r[:, None] * inv_freq
    ac = c[:, None] * inv_freq
    ang = jnp.concatenate([ar, ar, ac, ac], axis=-1)
    cos, sin = jnp.cos(ang), jnp.sin(ang)
    first_half = (jnp.arange(HEAD_DIM) % (2 * ROPE_FREQS)) < ROPE_FREQS
    sin_up = jnp.where(first_half, -sin, 0.0)
    sin_dn = jnp.where(first_half, 0.0, sin)
    n_ctx = n_batch * ctx_len

    def rows(t, fill):
        return jnp.concatenate([jnp.tile(t, (n_batch, 1)), jnp.full((n_ctx, HEAD_DIM), fill, F32)], axis=0)
    return rows(cos, 1.0), rows(sin_up, 0.0), rows(sin_dn, 0.0)


def _qk_prep_kernel(x_ref, w_ref, cos_ref, su_ref, sd_ref, o_ref, *, n_heads, scale):
    cos, su, sd = cos_ref[...], su_ref[...], sd_ref[...]
    w = w_ref[...]
    for h in range(n_heads):
        hs = slice(h * HEAD_DIM, (h + 1) * HEAD_DIM)
        x = x_ref[:, hs]
        n = x * lax.rsqrt(jnp.mean(x * x, axis=-1, keepdims=True) + NORM_EPS) * w
        y = n * cos + pltpu.roll(n, HEAD_DIM - ROPE_FREQS, 1) * su + pltpu.roll(n, ROPE_FREQS, 1) * sd
        if scale != 1.0:
            y = y * scale
        o_ref[:, hs] = y.astype(o_ref.dtype)


def qk_prep(proj, norm_w, tables, rows, col, width, scale, name):
    n_heads = width // HEAD_DIM
    tab_spec = pl.BlockSpec((ROW_TILE, HEAD_DIM), lambda i: (i, 0))
    return pl.pallas_call(
        functools.partial(_qk_prep_kernel, n_heads=n_heads, scale=scale),
        out_shape=jax.ShapeDtypeStruct((rows, width), BF16),
        grid=(rows // ROW_TILE,),
        in_specs=[pl.BlockSpec((ROW_TILE, width), lambda i: (i, col // width)),
                  pl.BlockSpec((1, HEAD_DIM), lambda i: (0, 0)),
                  tab_spec, tab_spec, tab_spec],
        out_specs=pl.BlockSpec((ROW_TILE, width), lambda i: (i, 0)),
        compiler_params=_params("arbitrary"),
        name=name,
    )(proj, norm_w.reshape(1, HEAD_DIM), *tables)


def _cast_kernel(x_ref, o_ref):
    o_ref[...] = x_ref[...].astype(o_ref.dtype)


def cast_cols(proj, col, width, name):
    rows = proj.shape[0]
    return pl.pallas_call(
        _cast_kernel,
        out_shape=jax.ShapeDtypeStruct((rows, width), BF16),
        grid=(rows // ROW_TILE,),
        in_specs=[pl.BlockSpec((ROW_TILE, width), lambda i: (i, col // width))],
        out_specs=pl.BlockSpec((ROW_TILE, width), lambda i: (i, 0)),
        compiler_params=_params("arbitrary"),
        name=name,
    )(proj)


def _attn_kernel(q_ref, kp_ref, kc_ref, kn_ref, vp_ref, vc_ref, vn_ref, kx_ref, vx_ref, sink_ref, o_ref, *,
                 n_blocks):
    i = pl.program_id(1)
    T = ATTN_BLOCK
    nq = Q_PER_KV * T
    nk = 3 * T + kx_ref.shape[0]
    qi = lax.broadcasted_iota(jnp.int32, (nq, nk), 0) % T
    kj = lax.broadcasted_iota(jnp.int32, (nq, nk), 1)
    lo = jnp.maximum(qi, jnp.where(i > 0, 0, T))
    hi = jnp.minimum(qi + 2 * WINDOW, jnp.where(i < n_blocks - 1, 3 * T - 1, 2 * T - 1))
    valid = ((kj >= lo) & (kj <= hi)) | (kj >= 3 * T)
    for h in range(N_KV_HEADS):
        hs = slice(h * HEAD_DIM, (h + 1) * HEAD_DIM)
        q = jnp.concatenate([q_ref[:, (h * Q_PER_KV + g) * HEAD_DIM:(h * Q_PER_KV + g + 1) * HEAD_DIM]
                             for g in range(Q_PER_KV)], axis=0)
        kb = jnp.concatenate([kp_ref[:, hs], kc_ref[:, hs], kn_ref[:, hs], kx_ref[:, hs]], axis=0)
        vb = jnp.concatenate([vp_ref[:, hs], vc_ref[:, hs], vn_ref[:, hs], vx_ref[:, hs]], axis=0)
        s = jnp.where(valid, _dot_nt(q, kb), -jnp.inf)
        sk = jnp.concatenate([jnp.broadcast_to(sink_ref[0:1, h * Q_PER_KV + g:h * Q_PER_KV + g + 1], (T, 1))
                              for g in range(Q_PER_KV)], axis=0)
        mx = jnp.maximum(jnp.max(s, axis=-1, keepdims=True), sk)
        p = jnp.exp(s - mx)
        denom = jnp.sum(p, axis=-1, keepdims=True) + jnp.exp(sk - mx)
        o = _dot(p.astype(BF16), vb) * (1.0 / denom)
        for g in range(Q_PER_KV):
            c0 = (h * Q_PER_KV + g) * HEAD_DIM
            o_ref[:, c0:c0 + HEAD_DIM] = o[g * T:(g + 1) * T, :].astype(o_ref.dtype)


def windowed_attention(qn, kn, vb, sink, n_batch, seq, ctx_len):
    nb = seq // ATTN_BLOCK
    ctx0 = n_batch * seq // ctx_len
    sink_row = jnp.zeros((1, LANES), F32).at[0, :N_Q_HEADS].set(sink)

    def kv_spec(d):
        return pl.BlockSpec((ATTN_BLOCK, KV_W), lambda b, i: (b * nb + jnp.clip(i + d, 0, nb - 1), 0))
    ctx_spec = pl.BlockSpec((ctx_len, KV_W), lambda b, i: (ctx0 + b, 0))
    return pl.pallas_call(
        functools.partial(_attn_kernel, n_blocks=nb),
        out_shape=jax.ShapeDtypeStruct((n_batch * seq, Q_W), BF16),
        grid=(n_batch, nb),
        in_specs=[pl.BlockSpec((ATTN_BLOCK, Q_W), lambda b, i: (b * nb + i, 0)),
                  kv_spec(-1), kv_spec(0), kv_spec(1), kv_spec(-1), kv_spec(0), kv_spec(1),
                  ctx_spec, ctx_spec,
                  pl.BlockSpec((1, LANES), lambda b, i: (0, 0))],
        out_specs=pl.BlockSpec((ATTN_BLOCK, Q_W), lambda b, i: (b * nb + i, 0)),
        compiler_params=_params("arbitrary", "arbitrary"),
        name="windowed_attention",
    )(qn, kn, kn, kn, vb, vb, vb, kn, vb, sink_row)


def _gate_norm_kernel(yf_ref, yb_ref, z_ref, w_ref, o_ref):
    gw = D_INNER // N_GROUPS
    for g in range(N_GROUPS):
        gs = slice(g * gw, (g + 1) * gw)
        z = z_ref[:, gs]
        s = (yf_ref[:, gs] + yb_ref[:, gs]) * (z * _sigmoid(z))
        n = s * lax.rsqrt(jnp.mean(s * s, axis=-1, keepdims=True) + NORM_EPS) * w_ref[:, gs]
        o_ref[:, gs] = n.astype(o_ref.dtype)


def gate_norm(y_f, y_b, proj, ssm_norm_w, rows):
    spec = pl.BlockSpec((ROW_TILE, D_INNER), lambda i: (i, 0))
    return pl.pallas_call(
        _gate_norm_kernel,
        out_shape=jax.ShapeDtypeStruct((rows, D_INNER), BF16),
        grid=(rows // ROW_TILE,),
        in_specs=[spec, spec, pl.BlockSpec((ROW_TILE, D_INNER), lambda i: (i, COL_Z // D_INNER)),
                  pl.BlockSpec((1, D_INNER), lambda i: (0, 0))],
        out_specs=spec,
        compiler_params=_params("arbitrary"),
        name="gate_norm",
    )(y_f, y_b, proj, ssm_norm_w.reshape(1, D_INNER))


def _merge_kernel(yg_ref, ya_ref, ws_ref, wa_ref, gs_ref, ga_ref, o_ref):
    o = (_sigmoid(gs_ref[...]) * _dot(yg_ref[...], ws_ref[...])
         + _sigmoid(ga_ref[...]) * _dot(ya_ref[...], wa_ref[...]))
    o_ref[...] = o.astype(o_ref.dtype)


def merge_branches(yg, ya, w_ssm_out, w_attn_out, proj, rows):
    tm, tn = 512, 512
    return pl.pallas_call(
        _merge_kernel,
        out_shape=jax.ShapeDtypeStruct((rows, D_MODEL), BF16),
        grid=(D_MODEL // tn, rows // tm),
        in_specs=[pl.BlockSpec((tm, D_INNER), lambda j, i: (i, 0)),
                  pl.BlockSpec((tm, Q_W), lambda j, i: (i, 0)),
                  pl.BlockSpec((D_INNER, tn), lambda j, i: (0, j)),
                  pl.BlockSpec((Q_W, tn), lambda j, i: (0, j)),
                  pl.BlockSpec((tm, tn), lambda j, i: (i, COL_GS // tn + j)),
                  pl.BlockSpec((tm, tn), lambda j, i: (i, COL_GA // tn + j))],
        out_specs=pl.BlockSpec((tm, tn), lambda j, i: (i, j)),
        compiler_params=_params("arbitrary", "arbitrary"),
        name="merge_branches",
    )(yg, ya, w_ssm_out, w_attn_out, proj, proj)


def _out_proj_kernel(m_ref, wo_ref, x_ref, mod_ref, nw_ref, rw_ref, rb_ref, x1_ref, h2_ref, lg_ref):
    x1 = x_ref[...] + mod_ref[0, 2:3, :] * _dot(m_ref[...], wo_ref[...])
    x1_ref[...] = x1
    n = x1 * lax.rsqrt(jnp.mean(x1 * x1, axis=-1, keepdims=True) + NORM_EPS) * nw_ref[...]
    h2 = n * (1.0 + mod_ref[0, 4:5, :]) + mod_ref[0, 3:4, :]
    h2_ref[...] = h2
    lg_ref[...] = _dot_f32(h2, rw_ref[...]) + rb_ref[...]


def out_proj_router(merged, w_o, xa, mod, norm2_w, router_w, router_b, n_batch, seq):
    rows = n_batch * seq
    n_tiles = rows // ROW_TILE
    rw = jnp.zeros((D_MODEL, LANES), F32).at[:, :N_EXPERTS].set(router_w)
    rb = jnp.zeros((1, LANES), F32).at[0, :N_EXPERTS].set(router_b)
    tile = pl.BlockSpec((ROW_TILE, D_MODEL), lambda i: (i, 0))
    return pl.pallas_call(
        _out_proj_kernel,
        out_shape=(jax.ShapeDtypeStruct((rows, D_MODEL), F32),
                   jax.ShapeDtypeStruct((rows, D_MODEL), F32),
                   jax.ShapeDtypeStruct((rows, LANES), F32)),
        grid=(n_tiles,),
        in_specs=[tile,
                  pl.BlockSpec((D_MODEL, D_MODEL), lambda i: (0, 0)),
                  tile,
                  pl.BlockSpec((1, 8, D_MODEL), _mod_row_map(n_tiles, seq // ROW_TILE, n_batch)),
                  pl.BlockSpec((1, D_MODEL), lambda i: (0, 0)),
                  pl.BlockSpec((D_MODEL, LANES), lambda i: (0, 0)),
                  pl.BlockSpec((1, LANES), lambda i: (0, 0))],
        out_specs=(tile, tile, pl.BlockSpec((ROW_TILE, LANES), lambda i: (i, 0))),
        compiler_params=_params("arbitrary"),
        name="out_proj_router",
    )(merged, w_o, xa, mod, norm2_w.reshape(1, D_MODEL), rw, rb)


def _route_kernel(lg_ref, o_ref, cnt_ref, carry_ref):
    @pl.when(pl.program_id(0) == 0)
    def _():
        carry_ref[...] = jnp.zeros_like(carry_ref)

    t = lg_ref.shape[0]
    lane = lax.broadcasted_iota(jnp.int32, (t, LANES), 1).astype(F32)
    work = jnp.where(lane < N_EXPERTS, lg_ref[...], -jnp.inf)
    vals, idxs = [], []
    for _ in range(TOP_K):
        m = jnp.max(work, axis=-1, keepdims=True)
        idx = jnp.min(jnp.where(work == m, lane, float(LANES)), axis=-1, keepdims=True)
        vals.append(m)
        idxs.append(idx)
        work = jnp.where(lane == idx, -jnp.inf, work)
    es = [jnp.exp(v - vals[0]) for v in vals]
    inv = 1.0 / (es[0] + es[1] + es[2] + es[3])
    onehot = jnp.zeros((t, LANES), F32)
    for idx in idxs:
        onehot = onehot + jnp.where(lane == idx, 1.0, 0.0)
    r = lax.broadcasted_iota(jnp.int32, (t, t), 0)
    c = lax.broadcasted_iota(jnp.int32, (t, t), 1)
    before = jnp.where(c < r, 1.0, 0.0).astype(BF16)
    excl = _dot(before, onehot.astype(BF16)) + carry_ref[...]
    out = jnp.zeros((t, LANES), F32)
    for k in range(TOP_K):
        rank = jnp.sum(jnp.where(lane == idxs[k], excl, 0.0), axis=-1, keepdims=True)
        out = jnp.where(lane == k, idxs[k], out)
        out = jnp.where(lane == TOP_K + k, es[k] * inv, out)
        out = jnp.where(lane == 2 * TOP_K + k, rank, out)
    o_ref[...] = out
    carry_ref[...] = carry_ref[...] + jnp.sum(onehot, axis=0, keepdims=True)
    cnt_ref[...] = carry_ref[...]


def route(logits):
    n_tok = logits.shape[0]
    return pl.pallas_call(
        _route_kernel,
        out_shape=(jax.ShapeDtypeStruct((n_tok, LANES), F32), jax.ShapeDtypeStruct((1, LANES), F32)),
        grid=(n_tok // ROW_TILE,),
        in_specs=[pl.BlockSpec((ROW_TILE, LANES), lambda i: (i, 0))],
        out_specs=(pl.BlockSpec((ROW_TILE, LANES), lambda i: (i, 0)),
                   pl.BlockSpec((1, LANES), lambda i: (0, 0))),
        scratch_shapes=[pltpu.VMEM((1, LANES), F32)],
        compiler_params=_params("arbitrary"),
        name="route",
    )(logits)


def _row_copy(src_ref, dst_ref, sem, src_row, dst_row):
    return pltpu.make_async_copy(src_ref.at[pl.ds(src_row, 1)], dst_ref.at[pl.ds(dst_row, 1)], sem)


DISPATCH_TILE = 128


def _dispatch_kernel(fill_ref, dest_ref, h_ref, o_hbm, zero_ref, fill_sem, sem):
    def fill_copy(blk):
        return pltpu.make_async_copy(zero_ref, o_hbm.at[pl.ds(blk * MOE_BLOCK, MOE_BLOCK)], fill_sem)

    @pl.when(pl.program_id(0) == 0)
    def _():
        zero_ref[...] = jnp.zeros_like(zero_ref)
        for s in range(fill_ref.shape[0]):
            @pl.when(fill_ref[s] >= 0)
            def _():
                fill_copy(fill_ref[s]).start()
        for s in range(fill_ref.shape[0]):
            @pl.when(fill_ref[s] >= 0)
            def _():
                fill_copy(0).wait()

    t = h_ref.shape[0]

    def start(r, carry):
        for k in range(TOP_K):
            _row_copy(h_ref, o_hbm, sem, r, dest_ref[0, 0, r * TOP_K + k]).start()
        return carry
    lax.fori_loop(0, t, start, 0)

    def wait(r, carry):
        for k in range(TOP_K):
            _row_copy(h_ref, o_hbm, sem, r, 0).wait()
        return carry
    lax.fori_loop(0, t, wait, 0)


def dispatch_rows(h2, dest, fill_blocks, cap):
    n_tok = h2.shape[0]
    t = DISPATCH_TILE
    n_tiles = n_tok // t
    grid_spec = pltpu.PrefetchScalarGridSpec(
        num_scalar_prefetch=1,
        grid=(n_tiles,),
        in_specs=[pl.BlockSpec((1, 1, t * TOP_K), lambda i, fb: (i, 0, 0), memory_space=pltpu.SMEM),
                  pl.BlockSpec((t, D_MODEL), lambda i, fb: (i, 0))],
        out_specs=pl.BlockSpec(memory_space=pl.ANY),
        scratch_shapes=[pltpu.VMEM((MOE_BLOCK, D_MODEL), F32), pltpu.SemaphoreType.DMA,
                        pltpu.SemaphoreType.DMA])
    return pl.pallas_call(
        _dispatch_kernel,
        out_shape=jax.ShapeDtypeStruct((cap, D_MODEL), F32),
        grid_spec=grid_spec,
        compiler_params=_params("arbitrary"),
        name="moe_dispatch",
    )(fill_blocks, dest.reshape(n_tiles, 1, t * TOP_K), h2)


EXPERT_COLS = 1024


def _expert_rows_pipeline(e, col, blk0_ref, nblk_ref, tail_ref, x_hbm, o_hbm, xbuf, obuf, zbuf,
                          sem_in, sem_out, sem_tail, compute, tail_max):
    n = nblk_ref[e]
    b0 = blk0_ref[e]
    width = obuf.shape[2]

    def rows(blk):
        return pl.ds(pl.multiple_of((b0 + blk) * MOE_BLOCK, MOE_BLOCK), MOE_BLOCK)

    def x_copy(blk, slot):
        return pltpu.make_async_copy(x_hbm.at[rows(blk)], xbuf.at[slot], sem_in.at[slot])

    def o_copy(blk, slot):
        return pltpu.make_async_copy(obuf.at[slot], o_hbm.at[rows(blk), pl.ds(col, width)], sem_out.at[slot])

    @pl.when(n > 0)
    def _():
        x_copy(0, 0).start()

        def body(blk, carry):
            slot = lax.rem(blk, 2)
            x_copy(blk, slot).wait()

            @pl.when(blk + 1 < n)
            def _():
                x_copy(blk + 1, 1 - slot).start()

            @pl.when(blk >= 2)
            def _():
                o_copy(blk - 2, slot).wait()
            obuf[slot] = compute(xbuf[slot]).astype(obuf.dtype)
            o_copy(blk, slot).start()
            return carry
        lax.fori_loop(0, n, body, 0)

        @pl.when(n >= 2)
        def _():
            o_copy(n - 2, lax.rem(n, 2)).wait()
        o_copy(n - 1, lax.rem(n - 1, 2)).wait()

    @pl.when(e == N_EXPERTS - 1)
    def _():
        zbuf[...] = jnp.zeros_like(zbuf)
        first, count = tail_ref[0], tail_ref[1]

        def z_copy(t):
            dst = pl.ds(pl.multiple_of((first + t) * MOE_BLOCK, MOE_BLOCK), MOE_BLOCK)
            return pltpu.make_async_copy(zbuf, o_hbm.at[dst, pl.ds(col, width)], sem_tail)
        for t in range(tail_max):
            @pl.when(t < count)
            def _():
                z_copy(t).start()
        for t in range(tail_max):
            @pl.when(t < count)
            def _():
                z_copy(t).wait()


def _expert_up_kernel(blk0_ref, nblk_ref, tail_ref, x_hbm, wg_ref, wu_ref, bg_ref, bu_ref, o_hbm,
                      xbuf, obuf, zbuf, wgb_ref, wub_ref, sem_in, sem_out, sem_tail, *, tail_max):
    j, e = pl.program_id(0), pl.program_id(1)

    @pl.when(nblk_ref[e] > 0)
    def _():
        wgb_ref[...] = wg_ref[...].astype(BF16)
        wub_ref[...] = wu_ref[...].astype(BF16)

    def compute(x):
        xb = x.astype(BF16)
        gate = jnp.minimum(_dot(xb, wgb_ref[...]) + bg_ref[0], SWIGLU_LIMIT)
        up = jnp.clip(_dot(xb, wub_ref[...]) + bu_ref[0], -SWIGLU_LIMIT, SWIGLU_LIMIT)
        return gate * _sigmoid(SWIGLU_ALPHA * gate) * (up + 1.0)
    _expert_rows_pipeline(e, pl.multiple_of(j * EXPERT_COLS, EXPERT_COLS), blk0_ref, nblk_ref, tail_ref,
                          x_hbm, o_hbm, xbuf, obuf, zbuf, sem_in, sem_out, sem_tail, compute, tail_max)


def _expert_scratch(in_width, in_dtype, out_dtype):
    return [pltpu.VMEM((2, MOE_BLOCK, in_width), in_dtype),
            pltpu.VMEM((2, MOE_BLOCK, EXPERT_COLS), out_dtype),
            pltpu.VMEM((MOE_BLOCK, EXPERT_COLS), out_dtype)]


_EXPERT_SEMS = [pltpu.SemaphoreType.DMA((2,)), pltpu.SemaphoreType.DMA((2,)), pltpu.SemaphoreType.DMA]


def expert_up(xs, blk0, nblk, tail, tail_max, w_gate_up, b_gate_up):
    cap = xs.shape[0]
    tf = EXPERT_COLS
    nj = D_FF // tf
    b3 = b_gate_up.reshape(N_EXPERTS, 1, 2 * D_FF)
    grid_spec = pltpu.PrefetchScalarGridSpec(
        num_scalar_prefetch=3,
        grid=(nj, N_EXPERTS),
        in_specs=[pl.BlockSpec(memory_space=pl.ANY),
                  pl.BlockSpec((None, D_MODEL, tf), lambda j, e, *_: (e, 0, j)),
                  pl.BlockSpec((None, D_MODEL, tf), lambda j, e, *_: (e, 0, nj + j)),
                  pl.BlockSpec((None, 1, tf), lambda j, e, *_: (e, 0, j)),
                  pl.BlockSpec((None, 1, tf), lambda j, e, *_: (e, 0, nj + j))],
        out_specs=pl.BlockSpec(memory_space=pl.ANY),
        scratch_shapes=_expert_scratch(D_MODEL, F32, BF16)
        + [pltpu.VMEM((D_MODEL, tf), BF16), pltpu.VMEM((D_MODEL, tf), BF16)] + _EXPERT_SEMS)
    return pl.pallas_call(
        functools.partial(_expert_up_kernel, tail_max=tail_max),
        out_shape=jax.ShapeDtypeStruct((cap, D_FF), BF16),
        grid_spec=grid_spec,
        compiler_params=_params("arbitrary", "arbitrary"),
        name="expert_up",
    )(blk0, nblk, tail, xs, w_gate_up, w_gate_up, b3, b3)


def _expert_down_kernel(blk0_ref, nblk_ref, tail_ref, a_hbm, w_ref, b_ref, o_hbm,
                        xbuf, obuf, zbuf, wb_ref, sem_in, sem_out, sem_tail, *, tail_max):
    j, e = pl.program_id(0), pl.program_id(1)

    @pl.when(nblk_ref[e] > 0)
    def _():
        wb_ref[...] = w_ref[...].astype(BF16)

    def compute(a):
        return _dot(a, wb_ref[...]) + b_ref[0]
    _expert_rows_pipeline(e, pl.multiple_of(j * EXPERT_COLS, EXPERT_COLS), blk0_ref, nblk_ref, tail_ref,
                          a_hbm, o_hbm, xbuf, obuf, zbuf, sem_in, sem_out, sem_tail, compute, tail_max)


def expert_down(act, blk0, nblk, tail, tail_max, w_down, b_down):
    cap = act.shape[0]
    tn = EXPERT_COLS
    grid_spec = pltpu.PrefetchScalarGridSpec(
        num_scalar_prefetch=3,
        grid=(D_MODEL // tn, N_EXPERTS),
        in_specs=[pl.BlockSpec(memory_space=pl.ANY),
                  pl.BlockSpec((None, D_FF, tn), lambda j, e, *_: (e, 0, j)),
                  pl.BlockSpec((None, 1, tn), lambda j, e, *_: (e, 0, j))],
        out_specs=pl.BlockSpec(memory_space=pl.ANY),
        scratch_shapes=_expert_scratch(D_FF, BF16, F32) + [pltpu.VMEM((D_FF, tn), BF16)] + _EXPERT_SEMS)
    return pl.pallas_call(
        functools.partial(_expert_down_kernel, tail_max=tail_max),
        out_shape=jax.ShapeDtypeStruct((cap, D_MODEL), F32),
        grid_spec=grid_spec,
        compiler_params=_params("arbitrary", "arbitrary"),
        name="expert_down",
    )(blk0, nblk, tail, act, w_down, b_down.reshape(N_EXPERTS, 1, D_MODEL))


COMBINE_TILE = 128


def _combine_kernel(dest_ref, yb_hbm, x1_ref, w_ref, mod_ref, o_ref, buf_ref, sem):
    t = COMBINE_TILE

    def start(r, carry):
        for k in range(TOP_K):
            _row_copy(yb_hbm, buf_ref.at[k], sem, dest_ref[0, 0, r * TOP_K + k], r).start()
        return carry
    lax.fori_loop(0, t, start, 0)

    def wait(r, carry):
        for k in range(TOP_K):
            _row_copy(yb_hbm, buf_ref.at[k], sem, 0, r).wait()
        return carry
    lax.fori_loop(0, t, wait, 0)
    acc = w_ref[:, TOP_K:TOP_K + 1] * buf_ref[0]
    for k in range(1, TOP_K):
        acc = acc + w_ref[:, TOP_K + k:TOP_K + k + 1] * buf_ref[k]
    o_ref[...] = x1_ref[...] + mod_ref[0, 5:6, :] * acc


def combine(yb, dest, x1, route_out, mod, n_batch, seq):
    rows = n_batch * seq
    t = COMBINE_TILE
    n_tiles = rows // t
    return pl.pallas_call(
        _combine_kernel,
        out_shape=jax.ShapeDtypeStruct((rows, D_MODEL), F32),
        grid=(n_tiles,),
        in_specs=[pl.BlockSpec((1, 1, t * TOP_K), lambda i: (i, 0, 0), memory_space=pltpu.SMEM),
                  pl.BlockSpec(memory_space=pl.ANY),
                  pl.BlockSpec((t, D_MODEL), lambda i: (i, 0)),
                  pl.BlockSpec((t, LANES), lambda i: (i, 0)),
                  pl.BlockSpec((1, 8, D_MODEL), _mod_row_map(n_tiles, seq // t, n_batch))],
        out_specs=pl.BlockSpec((t, D_MODEL), lambda i: (i, 0)),
        scratch_shapes=[pltpu.VMEM((TOP_K, t, D_MODEL), F32), pltpu.SemaphoreType.DMA],
        compiler_params=_params("arbitrary"),
        name="moe_combine",
    )(dest.reshape(n_tiles, 1, t * TOP_K), yb, x1, route_out, mod)


def moe_layout(route_out, counts):
    n_tok = route_out.shape[0]
    idx = route_out[:, :TOP_K].astype(jnp.int32)
    rank = route_out[:, 2 * TOP_K:3 * TOP_K].astype(jnp.int32)
    cnt = counts[0, :N_EXPERTS].astype(jnp.int32)
    padded = (cnt + MOE_BLOCK - 1) // MOE_BLOCK * MOE_BLOCK
    pad_end = jnp.cumsum(padded)
    pad_start = pad_end - padded
    dest = (pad_start[idx] + rank).reshape(-1)
    n_blocks = -(-(n_tok * TOP_K + N_EXPERTS * (MOE_BLOCK - 1)) // MOE_BLOCK)
    n_used = pad_end[-1] // MOE_BLOCK
    tail_max = n_blocks - (n_tok * TOP_K) // MOE_BLOCK
    tail = jnp.stack([n_used, n_blocks - n_used]).astype(jnp.int32)
    last_blk = jnp.where(cnt > 0, pad_end // MOE_BLOCK - 1, -1)
    tail_blk = n_used + jnp.arange(tail_max, dtype=jnp.int32)
    fill_blocks = jnp.concatenate([last_blk, jnp.where(tail_blk < n_blocks, tail_blk, -1)]).astype(jnp.int32)
    blk0 = (pad_start // MOE_BLOCK).astype(jnp.int32)
    nblk = (padded // MOE_BLOCK).astype(jnp.int32)
    return dest, blk0, nblk, tail, tail_max, fill_blocks, n_blocks * MOE_BLOCK


def _in_proj_weights(w_in):
    sizes = (XBC_W, D_INNER, N_SSM_HEADS, N_SSM_HEADS, Q_W, KV_W, KV_W, D_MODEL, D_MODEL)
    offs = [0]
    for s in sizes:
        offs.append(offs[-1] + s)
    seg = lambda i: w_in[:, offs[i]:offs[i + 1]]
    w_main = jnp.concatenate([seg(1), seg(0), seg(4), seg(5), seg(6), seg(7), seg(8)], axis=1).astype(BF16)
    w_dt = jnp.concatenate([seg(2), seg(3)], axis=1).astype(BF16)
    return w_main, w_dt


def hybrid_layer(x, ctx, c, c_ctx, w_ada, b_ada, norm1_w, norm2_w, w_in, conv_w, conv_b, dt_bias_f, dt_bias_b,
                 a_log_f, a_log_b, d_skip, ssm_norm_w, q_norm_w, k_norm_w, sink, w_ssm_out, w_attn_out, w_o,
                 router_w, router_b, w_gate_up, b_gate_up, w_down, b_down):
    n_batch, seq, _ = x.shape
    ctx_len = ctx.shape[1]
    n_lat = n_batch * seq
    x2 = x.reshape(n_lat, D_MODEL)
    rows = n_lat + n_batch * ctx_len

    cvec = jnp.zeros((8, D_MODEL), F32).at[:n_batch].set(c).at[n_batch].set(c_ctx)
    mod = ada_modulation(cvec, w_ada, b_ada).reshape(8, 6, D_MODEL)
    mod = jnp.concatenate([mod, jnp.zeros((8, 2, D_MODEL), F32)], axis=1)

    hn = norm_modulate(x2, ctx.reshape(n_batch * ctx_len, D_MODEL), norm1_w, mod, n_batch, seq)
    w_main, w_dt = _in_proj_weights(w_in)
    proj = matmul(hn, w_main, 512, 1024, name="in_proj")
    dt_raw = matmul(hn, w_dt, 512, LANES, name="in_proj_dt")

    xbc = conv_silu(proj, conv_w, conv_b, n_batch, seq, ctx_len)
    dt_bias = jnp.concatenate([dt_bias_f, dt_bias_b]).reshape(1, LANES)
    a_neg = -jnp.exp(jnp.concatenate([a_log_f, a_log_b])).reshape(1, LANES)
    dskip_x = jnp.repeat(d_skip, SSM_HEAD_DIM).reshape(1, D_INNER)
    y_f = ssd_scan(xbc, dt_raw, dt_bias, a_neg, dskip_x, n_batch, seq, ctx_len, rev=False)
    y_b = ssd_scan(xbc, dt_raw, dt_bias, a_neg, dskip_x, n_batch, seq, ctx_len, rev=True)

    tables = _rope_tables(n_batch, seq, ctx_len)
    qn = qk_prep(proj, q_norm_w, tables, n_lat, COL_Q, Q_W, ATTN_SCALE, "q_prep")
    kn = qk_prep(proj, k_norm_w, tables, rows, COL_K, KV_W, 1.0, "k_prep")
    vb = cast_cols(proj, COL_V, KV_W, "v_cast")
    y_attn = windowed_attention(qn, kn, vb, sink, n_batch, seq, ctx_len)

    yg = gate_norm(y_f, y_b, proj, ssm_norm_w, n_lat)
    merged = merge_branches(yg, y_attn, w_ssm_out.astype(BF16), w_attn_out.astype(BF16), proj, n_lat)
    x1, h2, logits = out_proj_router(merged, w_o.astype(BF16), x2, mod, norm2_w, router_w, router_b,
                                     n_batch, seq)

    route_out, counts = route(logits)
    dest, blk0, nblk, tail, tail_max, fill_blocks, cap = moe_layout(route_out, counts)
    xs = dispatch_rows(h2, dest, fill_blocks, cap)
    act = expert_up(xs, blk0, nblk, tail, tail_max, w_gate_up, b_gate_up)
    yb = expert_down(act, blk0, nblk, tail, tail_max, w_down, b_down)
    out = combine(yb, dest, x1, route_out, mod, n_batch, seq)
    return out.reshape(n_batch, seq, D_MODEL)


def kernel(x, c, ctx, c_ctx, w_ada, b_ada, norm1_w, norm2_w, w_in, conv_w, conv_b, dt_bias_f, dt_bias_b,
           a_log_f, a_log_b, d_skip, ssm_norm_w, q_norm_w, k_norm_w, sink, w_ssm_out, w_attn_out, w_o,
           router_w, router_b, w_gate_up, b_gate_up, w_down, b_down):
    assert w_ada.shape[0] == 1, "single-layer block"
    return hybrid_layer(x, ctx, c, c_ctx, w_ada[0], b_ada[0], norm1_w[0], norm2_w[0], w_in[0], conv_w[0],
                        conv_b[0], dt_bias_f[0], dt_bias_b[0], a_log_f[0], a_log_b[0], d_skip[0],
                        ssm_norm_w[0], q_norm_w[0], k_norm_w[0], sink[0], w_ssm_out[0], w_attn_out[0], w_o[0],
                        router_w[0], router_b[0], w_gate_up[0], b_gate_up[0], w_down[0], b_down[0])
```

```python
import functools
import math

import jax
import jax.numpy as jnp
from jax import lax
from jax.experimental import pallas as pl
from jax.experimental.pallas import tpu as pltpu

F32 = jnp.float32
BF16 = jnp.bfloat16

D_MODEL = 2048
GRID_W = 64
NORM_EPS = 1e-6
D_INNER = 2 * D_MODEL
SSM_HEAD_DIM = 64
N_SSM_HEADS = D_INNER // SSM_HEAD_DIM
N_GROUPS = 8
HEADS_PER_GROUP = N_SSM_HEADS // N_GROUPS
D_STATE = 128
BC_W = N_GROUPS * D_STATE
XBC_W = D_INNER + 2 * BC_W
CONV_K = 5
SSD_CHUNK = 128
HEAD_DIM = 128
N_Q_HEADS = D_MODEL // HEAD_DIM
N_KV_HEADS = 4
Q_PER_KV = N_Q_HEADS // N_KV_HEADS
Q_W = N_Q_HEADS * HEAD_DIM
KV_W = N_KV_HEADS * HEAD_DIM
WINDOW = 128
ATTN_BLOCK = 128
ATTN_SCALE = HEAD_DIM ** -0.5
ROPE_FREQS = HEAD_DIM // 4
ROPE_BASE = 10000.0
N_EXPERTS = 32
TOP_K = 4
D_FF = D_MODEL
SWIGLU_LIMIT = 7.0
SWIGLU_ALPHA = 1.702
MOE_BLOCK = 512

LANES = 128
ROW_TILE = 256
MAIN_W = D_INNER + XBC_W + Q_W + 2 * KV_W + 2 * D_MODEL
COL_Z, COL_XBC, COL_Q = 0, D_INNER, D_INNER + XBC_W
COL_K, COL_V = COL_Q + Q_W, COL_Q + Q_W + KV_W
COL_GS, COL_GA = COL_V + KV_W, COL_V + KV_W + D_MODEL
VMEM_LIMIT = 56 * 1024 * 1024


def _params(*sem, vmem_limit=VMEM_LIMIT):
    return pltpu.CompilerParams(dimension_semantics=sem, vmem_limit_bytes=vmem_limit)


def _dot(a, b):
    return jnp.dot(a, b, preferred_element_type=F32)


def _dot_nt(a, b):
    return lax.dot_general(a, b, (((1,), (1,)), ((), ())), preferred_element_type=F32)


def _split3(a):
    a1 = a.astype(BF16)
    r = a - a1.astype(F32)
    a2 = r.astype(BF16)
    a3 = (r - a2.astype(F32)).astype(BF16)
    return a1, a2, a3


def _dot_f32(a, b):
    a1, a2, _ = _split3(a)
    b1, b2, _ = _split3(b)
    return _dot(a1, b1) + (_dot(a1, b2) + _dot(a2, b1))


def _dot_sel_rhs(a, sel):
    a1, a2, a3 = _split3(a)
    return _dot(a1, sel) + (_dot(a2, sel) + _dot(a3, sel))


def _dot_sel_lhs(sel, a):
    a1, a2, a3 = _split3(a)
    return _dot(sel, a1) + (_dot(sel, a2) + _dot(sel, a3))


def _sigmoid(x):
    return 1.0 / (1.0 + jnp.exp(-x))


def _ada_kernel(c_ref, w_ref, b_ref, o_ref):
    c = c_ref[...]
    o_ref[...] = _dot_f32(c * _sigmoid(c), w_ref[...]) + b_ref[...]


def ada_modulation(cvec, w_ada, b_ada):
    n = w_ada.shape[1]
    tn = 1024
    return pl.pallas_call(
        _ada_kernel,
        out_shape=jax.ShapeDtypeStruct((8, n), F32),
        grid=(n // tn,),
        in_specs=[pl.BlockSpec((8, D_MODEL), lambda j: (0, 0)),
                  pl.BlockSpec((D_MODEL, tn), lambda j: (0, j)),
                  pl.BlockSpec((1, tn), lambda j: (0, j))],
        out_specs=pl.BlockSpec((8, tn), lambda j: (0, j)),
        compiler_params=_params("arbitrary"),
        name="ada_modulation",
    )(cvec, w_ada, b_ada.reshape(1, n))


def _norm_mod_kernel(x_ref, c_ref, w_ref, mod_ref, o_ref, *, n_lat_tiles):
    def emit(x):
        y = x * lax.rsqrt(jnp.mean(x * x, axis=-1, keepdims=True) + NORM_EPS) * w_ref[...]
        o_ref[...] = (y * (1.0 + mod_ref[0, 1:2, :]) + mod_ref[0, 0:1, :]).astype(o_ref.dtype)

    @pl.when(pl.program_id(0) < n_lat_tiles)
    def _():
        emit(x_ref[...])

    @pl.when(pl.program_id(0) >= n_lat_tiles)
    def _():
        emit(c_ref[...])


def _mod_row_map(n_lat_tiles, tiles_per_batch, n_batch):
    def index_map(i, *_):
        return (jnp.where(i < n_lat_tiles, i // tiles_per_batch, n_batch), 0, 0)
    return index_map


def norm_modulate(x2, ctx2, norm_w, mod, n_batch, seq):
    rows = x2.shape[0] + ctx2.shape[0]
    n_lat_tiles = n_batch * seq // ROW_TILE
    return pl.pallas_call(
        functools.partial(_norm_mod_kernel, n_lat_tiles=n_lat_tiles),
        out_shape=jax.ShapeDtypeStruct((rows, D_MODEL), BF16),
        grid=(rows // ROW_TILE,),
        in_specs=[pl.BlockSpec((ROW_TILE, D_MODEL), lambda i: (jnp.minimum(i, n_lat_tiles - 1), 0)),
                  pl.BlockSpec((ROW_TILE, D_MODEL), lambda i: (jnp.maximum(i - n_lat_tiles, 0), 0)),
                  pl.BlockSpec((1, D_MODEL), lambda i: (0, 0)),
                  pl.BlockSpec((1, 8, D_MODEL), _mod_row_map(n_lat_tiles, seq // ROW_TILE, n_batch))],
        out_specs=pl.BlockSpec((ROW_TILE, D_MODEL), lambda i: (i, 0)),
        compiler_params=_params("arbitrary"),
        name="norm_modulate",
    )(x2, ctx2, norm_w.reshape(1, D_MODEL), mod)


def _mm_kernel(a_ref, b_ref, o_ref):
    o_ref[...] = _dot(a_ref[...], b_ref[...]).astype(o_ref.dtype)


def matmul(a, b, tm, tn, out_dtype=F32, name="matmul"):
    m, k = a.shape
    n = b.shape[1]
    return pl.pallas_call(
        _mm_kernel,
        out_shape=jax.ShapeDtypeStruct((m, n), out_dtype),
        grid=(n // tn, m // tm),
        in_specs=[pl.BlockSpec((tm, k), lambda j, i: (i, 0)),
                  pl.BlockSpec((k, tn), lambda j, i: (0, j))],
        out_specs=pl.BlockSpec((tm, tn), lambda j, i: (i, j)),
        compiler_params=_params("arbitrary", "arbitrary"),
        name=name,
    )(a, b)


CONV_COLS = 2048
HALO = 16


def _conv_kernel(prev_ref, cur_ref, next_ref, w_ref, b_ref, o_ref, *,
                 n_lat_tiles, lat_tiles_per_seq, ctx_tiles_per_seq):
    i = pl.program_id(1)
    in_lat = i < n_lat_tiles
    pos = jnp.where(in_lat, i % lat_tiles_per_seq, (i - n_lat_tiles) % ctx_tiles_per_seq)
    per_seq = jnp.where(in_lat, lat_tiles_per_seq, ctx_tiles_per_seq)
    has_prev = pos > 0
    has_next = pos < per_seq - 1
    t = cur_ref.shape[0]
    n = t + 2 * HALO
    full = jnp.concatenate([jnp.where(has_prev, prev_ref[...].astype(F32), 0.0), cur_ref[...].astype(F32),
                            jnp.where(has_next, next_ref[...].astype(F32), 0.0)], axis=0)
    acc = jnp.broadcast_to(b_ref[...], (t, CONV_COLS))
    for k in range(CONV_K):
        shift = CONV_K // 2 - k
        shifted = full if shift == 0 else pltpu.roll(full, shift % n, 0)
        acc = acc + w_ref[k:k + 1, :] * shifted[HALO:HALO + t, :]
    o_ref[...] = acc * _sigmoid(acc)


def conv_silu(proj, conv_w, conv_b, n_batch, seq, ctx_len):
    rows = proj.shape[0]
    n_lat_tiles = n_batch * seq // ROW_TILE
    col0 = COL_XBC // CONV_COLS
    per = ROW_TILE // HALO
    last_halo = rows // HALO - 1
    kern = functools.partial(_conv_kernel, n_lat_tiles=n_lat_tiles,
                             lat_tiles_per_seq=seq // ROW_TILE, ctx_tiles_per_seq=ctx_len // ROW_TILE)
    w8 = jnp.concatenate([conv_w, jnp.zeros((8 - CONV_K, XBC_W), F32)], axis=0)
    return pl.pallas_call(
        kern,
        out_shape=jax.ShapeDtypeStruct((rows, XBC_W), F32),
        grid=(XBC_W // CONV_COLS, rows // ROW_TILE),
        in_specs=[pl.BlockSpec((HALO, CONV_COLS), lambda j, i: (jnp.maximum(i * per - 1, 0), col0 + j)),
                  pl.BlockSpec((ROW_TILE, CONV_COLS), lambda j, i: (i, col0 + j)),
                  pl.BlockSpec((HALO, CONV_COLS), lambda j, i: (jnp.minimum((i + 1) * per, last_halo), col0 + j)),
                  pl.BlockSpec((8, CONV_COLS), lambda j, i: (0, j)),
                  pl.BlockSpec((1, CONV_COLS), lambda j, i: (0, j))],
        out_specs=pl.BlockSpec((ROW_TILE, CONV_COLS), lambda j, i: (i, j)),
        compiler_params=_params("arbitrary", "arbitrary"),
        name="conv_silu",
    )(proj, proj, proj, w8, conv_b.reshape(1, XBC_W))


def _softplus(x):
    return jnp.maximum(x, 0.0) + jnp.log(1.0 + jnp.exp(-jnp.abs(x)))


def _ssd_kernel(xs_ref, b_ref, c_ref, dt_ref, bias_ref, a_ref, dskip_ref, y_ref, h_ref, *, rev):
    @pl.when(pl.program_id(1) == 0)
    def _():
        h_ref[...] = jnp.zeros_like(h_ref)

    L = SSD_CHUNK
    gw = HEADS_PER_GROUP * SSM_HEAD_DIM
    off = N_SSM_HEADS if rev else 0
    row = lax.broadcasted_iota(jnp.int32, (L, L), 0)
    col = lax.broadcasted_iota(jnp.int32, (L, L), 1)
    causal = (col >= row) if rev else (col <= row)
    first_head = col < SSM_HEAD_DIM
    tmat = jnp.where(causal, 1.0, 0.0).astype(BF16)

    dt = _softplus(dt_ref[...] + bias_ref[...])
    a = dt * a_ref[...]
    acum = _dot_sel_lhs(tmat, a)
    acum_t = acum.T
    dt_t = dt.T
    last = 0 if rev else L - 1
    to_end_t = jnp.exp(acum_t[:, last:last + 1] - acum_t) * dt_t
    src_t = acum_t - jnp.log(dt_t)

    for g in range(N_GROUPS):
        bg = b_ref[:, g * D_STATE:(g + 1) * D_STATE]
        cgb = c_ref[:, g * D_STATE:(g + 1) * D_STATE].astype(BF16)
        cb = _dot_nt(cgb, bg.astype(BF16))
        bg_t = bg.T
        y_off = _dot(cgb, h_ref[:, g * gw:(g + 1) * gw].astype(BF16))
        for pair in range(HEADS_PER_GROUP // 2):
            c0 = g * gw + pair * LANES
            x_f = xs_ref[:, c0:c0 + LANES]
            x2 = jnp.concatenate([jnp.where(first_head, x_f, 0.0).astype(BF16),
                                  jnp.where(first_head, 0.0, x_f).astype(BF16)], axis=0)
            ms, ws, bcs = [], [], []
            for k in range(2):
                hd = off + g * HEADS_PER_GROUP + 2 * pair + k
                bc = jnp.broadcast_to(acum[:, hd:hd + 1], (L, L))
                decay_dt = jnp.exp(jnp.where(causal, bc - src_t[hd:hd + 1, :], -jnp.inf))
                ms.append((cb * decay_dt).astype(BF16))
                ws.append((bg_t * to_end_t[hd:hd + 1, :]).astype(BF16))
                bcs.append(bc)
            lhs = jnp.concatenate([jnp.concatenate(ms, axis=1), jnp.concatenate(ws, axis=1)], axis=0)
            res = _dot(lhs, x2)
            e_t = jnp.exp(jnp.where(first_head, bcs[0], bcs[1]))
            y = res[:L] + y_off[:, pair * LANES:(pair + 1) * LANES] * e_t
            if not rev:
                y = y + dskip_ref[:, c0:c0 + LANES] * x_f
            y_ref[:, c0:c0 + LANES] = y
            h_ref[:, c0:c0 + LANES] = e_t[last:last + 1, :] * h_ref[:, c0:c0 + LANES] + res[L:]


def ssd_scan(xbc, dt_raw, dt_bias, a_neg, dskip_x, n_batch, seq, ctx_len, rev):
    rows = xbc.shape[0]
    nc, ncc = seq // SSD_CHUNK, ctx_len // SSD_CHUNK
    lat_blocks = n_batch * nc

    def blk(b, j):
        cj = (ncc - 1 - j) if rev else j
        lj = (nc - 1 - (j - ncc)) if rev else (j - ncc)
        return jnp.where(j < ncc, lat_blocks + b * ncc + cj, b * nc + lj)

    return pl.pallas_call(
        functools.partial(_ssd_kernel, rev=rev),
        out_shape=jax.ShapeDtypeStruct((rows, D_INNER), F32),
        grid=(n_batch, ncc + nc),
        in_specs=[pl.BlockSpec((SSD_CHUNK, D_INNER), lambda b, j: (blk(b, j), 0)),
                  pl.BlockSpec((SSD_CHUNK, BC_W), lambda b, j: (blk(b, j), D_INNER // BC_W)),
                  pl.BlockSpec((SSD_CHUNK, BC_W), lambda b, j: (blk(b, j), D_INNER // BC_W + 1)),
                  pl.BlockSpec((SSD_CHUNK, LANES), lambda b, j: (blk(b, j), 0)),
                  pl.BlockSpec((1, LANES), lambda b, j: (0, 0)),
                  pl.BlockSpec((1, LANES), lambda b, j: (0, 0)),
                  pl.BlockSpec((1, D_INNER), lambda b, j: (0, 0))],
        out_specs=pl.BlockSpec((SSD_CHUNK, D_INNER), lambda b, j: (blk(b, j), 0)),
        scratch_shapes=[pltpu.VMEM((D_STATE, D_INNER), F32)],
        compiler_params=_params("arbitrary", "arbitrary"),
        name="ssd_scan_bwd" if rev else "ssd_scan_fwd",
    )(xbc, xbc, xbc, dt_raw, dt_bias, a_neg, dskip_x)


def _rope_tables(n_batch, seq, ctx_len):
    inv_freq = ROPE_BASE ** (-jnp.arange(ROPE_FREQS, dtype=F32) / ROPE_FREQS)
    n_rows = seq // GRID_W
    r = jnp.repeat(jnp.arange(n_rows, dtype=F32), GRID_W)
    c = jnp.tile(jnp.arange(GRID_W, dtype=F32), n_rows)
    ar = r[:, None] * inv_freq
    ac = c[:, None] * inv_freq
    ang = jnp.concatenate([ar, ar, ac, ac], axis=-1)
    cos, sin = jnp.cos(ang), jnp.sin(ang)
    first_half = (jnp.arange(HEAD_DIM) % (2 * ROPE_FREQS)) < ROPE_FREQS
    sin_up = jnp.where(first_half, -sin, 0.0)
    sin_dn = jnp.where(first_half, 0.0, sin)
    n_ctx = n_batch * ctx_len

    def rows(t, fill):
        return jnp.concatenate([jnp.tile(t, (n_batch, 1)), jnp.full((n_ctx, HEAD_DIM), fill, F32)], axis=0)
    return rows(cos, 1.0), rows(sin_up, 0.0), rows(sin_dn, 0.0)


def _qk_prep_kernel(x_ref, w_ref, cos_ref, su_ref, sd_ref, o_ref, *, n_heads, scale):
    cos, su, sd = cos_ref[...], su_ref[...], sd_ref[...]
    w = w_ref[...]
    for h in range(n_heads):
        hs = slice(h * HEAD_DIM, (h + 1) * HEAD_DIM)
        x = x_ref[:, hs].astype(F32)
        n = x * lax.rsqrt(jnp.mean(x * x, axis=-1, keepdims=True) + NORM_EPS) * w
        y = n * cos + pltpu.roll(n, HEAD_DIM - ROPE_FREQS, 1) * su + pltpu.roll(n, ROPE_FREQS, 1) * sd
        if scale != 1.0:
            y = y * scale
        o_ref[:, hs] = y.astype(o_ref.dtype)


def qk_prep(proj, norm_w, tables, rows, col, width, scale, name):
    n_heads = width // HEAD_DIM
    tab_spec = pl.BlockSpec((ROW_TILE, HEAD_DIM), lambda i: (i, 0))
    return pl.pallas_call(
        functools.partial(_qk_prep_kernel, n_heads=n_heads, scale=scale),
        out_shape=jax.ShapeDtypeStruct((rows, width), BF16),
        grid=(rows // ROW_TILE,),
        in_specs=[pl.BlockSpec((ROW_TILE, width), lambda i: (i, col // width)),
                  pl.BlockSpec((1, HEAD_DIM), lambda i: (0, 0)),
                  tab_spec, tab_spec, tab_spec],
        out_specs=pl.BlockSpec((ROW_TILE, width), lambda i: (i, 0)),
        compiler_params=_params("arbitrary"),
        name=name,
    )(proj, norm_w.reshape(1, HEAD_DIM), *tables)


def _attn_kernel(q_ref, kp_ref, kc_ref, kn_ref, vp_ref, vc_ref, vn_ref, kx_ref, vx_ref, sink_ref, o_ref, *,
                 n_blocks):
    i = pl.program_id(1)
    T = ATTN_BLOCK
    nq = Q_PER_KV * T
    nk = 3 * T + kx_ref.shape[0]
    qi = lax.broadcasted_iota(jnp.int32, (nq, nk), 0) % T
    kj = lax.broadcasted_iota(jnp.int32, (nq, nk), 1)
    lo = jnp.maximum(qi, jnp.where(i > 0, 0, T))
    hi = jnp.minimum(qi + 2 * WINDOW, jnp.where(i < n_blocks - 1, 3 * T - 1, 2 * T - 1))
    valid = ((kj >= lo) & (kj <= hi)) | (kj >= 3 * T)
    for h in range(N_KV_HEADS):
        hs = slice(h * HEAD_DIM, (h + 1) * HEAD_DIM)
        q = jnp.concatenate([q_ref[:, (h * Q_PER_KV + g) * HEAD_DIM:(h * Q_PER_KV + g + 1) * HEAD_DIM]
                             for g in range(Q_PER_KV)], axis=0)
        kb = jnp.concatenate([kp_ref[:, hs], kc_ref[:, hs], kn_ref[:, hs], kx_ref[:, hs]], axis=0)
        vb = jnp.concatenate([vp_ref[:, hs], vc_ref[:, hs], vn_ref[:, hs], vx_ref[:, hs]], axis=0)
        s = jnp.where(valid, _dot_nt(q, kb), -jnp.inf)
        sk = jnp.concatenate([jnp.broadcast_to(sink_ref[0:1, h * Q_PER_KV + g:h * Q_PER_KV + g + 1], (T, 1))
                              for g in range(Q_PER_KV)], axis=0)
        mx = jnp.maximum(jnp.max(s, axis=-1, keepdims=True), sk)
        p = jnp.exp(s - mx)
        denom = jnp.sum(p, axis=-1, keepdims=True) + jnp.exp(sk - mx)
        o = _dot(p.astype(BF16), vb) * (1.0 / denom)
        for g in range(Q_PER_KV):
            c0 = (h * Q_PER_KV + g) * HEAD_DIM
            o_ref[:, c0:c0 + HEAD_DIM] = o[g * T:(g + 1) * T, :].astype(o_ref.dtype)


def windowed_attention(qn, kn, proj, sink, n_batch, seq, ctx_len):
    nb = seq // ATTN_BLOCK
    ctx0 = n_batch * seq // ctx_len
    sink_row = jnp.zeros((1, LANES), F32).at[0, :N_Q_HEADS].set(sink)
    v_col = COL_V // KV_W

    def kv_spec(d, col):
        return pl.BlockSpec((ATTN_BLOCK, KV_W), lambda b, i: (b * nb + jnp.clip(i + d, 0, nb - 1), col))

    def ctx_spec(col):
        return pl.BlockSpec((ctx_len, KV_W), lambda b, i: (ctx0 + b, col))
    return pl.pallas_call(
        functools.partial(_attn_kernel, n_blocks=nb),
        out_shape=jax.ShapeDtypeStruct((n_batch * seq, Q_W), BF16),
        grid=(n_batch, nb),
        in_specs=[pl.BlockSpec((ATTN_BLOCK, Q_W), lambda b, i: (b * nb + i, 0)),
                  kv_spec(-1, 0), kv_spec(0, 0), kv_spec(1, 0),
                  kv_spec(-1, v_col), kv_spec(0, v_col), kv_spec(1, v_col),
                  ctx_spec(0), ctx_spec(v_col),
                  pl.BlockSpec((1, LANES), lambda b, i: (0, 0))],
        out_specs=pl.BlockSpec((ATTN_BLOCK, Q_W), lambda b, i: (b * nb + i, 0)),
        compiler_params=_params("arbitrary", "arbitrary"),
        name="windowed_attention",
    )(qn, kn, kn, kn, proj, proj, proj, kn, proj, sink_row)


def _gate_norm_kernel(yf_ref, yb_ref, z_ref, w_ref, o_ref):
    gw = D_INNER // N_GROUPS
    for g in range(N_GROUPS):
        gs = slice(g * gw, (g + 1) * gw)
        z = z_ref[:, gs].astype(F32)
        s = (yf_ref[:, gs] + yb_ref[:, gs]) * (z * _sigmoid(z))
        n = s * lax.rsqrt(jnp.mean(s * s, axis=-1, keepdims=True) + NORM_EPS) * w_ref[:, gs]
        o_ref[:, gs] = n.astype(o_ref.dtype)


def gate_norm(y_f, y_b, proj, ssm_norm_w, rows):
    spec = pl.BlockSpec((ROW_TILE, D_INNER), lambda i: (i, 0))
    return pl.pallas_call(
        _gate_norm_kernel,
        out_shape=jax.ShapeDtypeStruct((rows, D_INNER), BF16),
        grid=(rows // ROW_TILE,),
        in_specs=[spec, spec, pl.BlockSpec((ROW_TILE, D_INNER), lambda i: (i, COL_Z // D_INNER)),
                  pl.BlockSpec((1, D_INNER), lambda i: (0, 0))],
        out_specs=spec,
        compiler_params=_params("arbitrary"),
        name="gate_norm",
    )(y_f, y_b, proj, ssm_norm_w.reshape(1, D_INNER))


def _merge_kernel(yg_ref, ya_ref, ws_ref, wa_ref, gs_ref, ga_ref, o_ref):
    o = (_sigmoid(gs_ref[...].astype(F32)) * _dot(yg_ref[...], ws_ref[...])
         + _sigmoid(ga_ref[...].astype(F32)) * _dot(ya_ref[...], wa_ref[...]))
    o_ref[...] = o.astype(o_ref.dtype)


def merge_branches(yg, ya, w_ssm_out, w_attn_out, proj, rows):
    tm, tn = 512, 512
    return pl.pallas_call(
        _merge_kernel,
        out_shape=jax.ShapeDtypeStruct((rows, D_MODEL), BF16),
        grid=(D_MODEL // tn, rows // tm),
        in_specs=[pl.BlockSpec((tm, D_INNER), lambda j, i: (i, 0)),
                  pl.BlockSpec((tm, Q_W), lambda j, i: (i, 0)),
                  pl.BlockSpec((D_INNER, tn), lambda j, i: (0, j)),
                  pl.BlockSpec((Q_W, tn), lambda j, i: (0, j)),
                  pl.BlockSpec((tm, tn), lambda j, i: (i, COL_GS // tn + j)),
                  pl.BlockSpec((tm, tn), lambda j, i: (i, COL_GA // tn + j))],
        out_specs=pl.BlockSpec((tm, tn), lambda j, i: (i, j)),
        compiler_params=_params("arbitrary", "arbitrary"),
        name="merge_branches",
    )(yg, ya, w_ssm_out, w_attn_out, proj, proj)


def _out_proj_kernel(m_ref, wo_ref, x_ref, mod_ref, nw_ref, rw_ref, rb_ref, x1_ref, h2_ref, lg_ref):
    x1 = x_ref[...] + mod_ref[0, 2:3, :] * _dot(m_ref[...], wo_ref[...])
    x1_ref[...] = x1
    n = x1 * lax.rsqrt(jnp.mean(x1 * x1, axis=-1, keepdims=True) + NORM_EPS) * nw_ref[...]
    h2 = n * (1.0 + mod_ref[0, 4:5, :]) + mod_ref[0, 3:4, :]
    h2_ref[...] = h2
    lg_ref[...] = _dot_f32(h2, rw_ref[...]) + rb_ref[...]


def out_proj_router(merged, w_o, xa, mod, norm2_w, router_w, router_b, n_batch, seq):
    rows = n_batch * seq
    n_tiles = rows // ROW_TILE
    rw = jnp.zeros((D_MODEL, LANES), F32).at[:, :N_EXPERTS].set(router_w)
    rb = jnp.zeros((1, LANES), F32).at[0, :N_EXPERTS].set(router_b)
    tile = pl.BlockSpec((ROW_TILE, D_MODEL), lambda i: (i, 0))
    return pl.pallas_call(
        _out_proj_kernel,
        out_shape=(jax.ShapeDtypeStruct((rows, D_MODEL), F32),
                   jax.ShapeDtypeStruct((rows, D_MODEL), F32),
                   jax.ShapeDtypeStruct((rows, LANES), F32)),
        grid=(n_tiles,),
        in_specs=[tile,
                  pl.BlockSpec((D_MODEL, D_MODEL), lambda i: (0, 0)),
                  tile,
                  pl.BlockSpec((1, 8, D_MODEL), _mod_row_map(n_tiles, seq // ROW_TILE, n_batch)),
                  pl.BlockSpec((1, D_MODEL), lambda i: (0, 0)),
                  pl.BlockSpec((D_MODEL, LANES), lambda i: (0, 0)),
                  pl.BlockSpec((1, LANES), lambda i: (0, 0))],
        out_specs=(tile, tile, pl.BlockSpec((ROW_TILE, LANES), lambda i: (i, 0))),
        compiler_params=_params("arbitrary"),
        name="out_proj_router",
    )(merged, w_o, xa, mod, norm2_w.reshape(1, D_MODEL), rw, rb)


def _route_kernel(lg_ref, o_ref, cnt_ref, carry_ref):
    @pl.when(pl.program_id(0) == 0)
    def _():
        carry_ref[...] = jnp.zeros_like(carry_ref)

    t = lg_ref.shape[0]
    lane = lax.broadcasted_iota(jnp.int32, (t, LANES), 1).astype(F32)
    work = jnp.where(lane < N_EXPERTS, lg_ref[...], -jnp.inf)
    vals, idxs = [], []
    for _ in range(TOP_K):
        m = jnp.max(work, axis=-1, keepdims=True)
        idx = jnp.min(jnp.where(work == m, lane, float(LANES)), axis=-1, keepdims=True)
        vals.append(m)
        idxs.append(idx)
        work = jnp.where(lane == idx, -jnp.inf, work)
    es = [jnp.exp(v - vals[0]) for v in vals]
    inv = 1.0 / (es[0] + es[1] + es[2] + es[3])
    onehot = jnp.zeros((t, LANES), F32)
    for idx in idxs:
        onehot = onehot + jnp.where(lane == idx, 1.0, 0.0)
    r = lax.broadcasted_iota(jnp.int32, (t, t), 0)
    c = lax.broadcasted_iota(jnp.int32, (t, t), 1)
    before = jnp.where(c < r, 1.0, 0.0).astype(BF16)
    excl = _dot(before, onehot.astype(BF16)) + carry_ref[...]
    out = jnp.zeros((t, LANES), F32)
    for k in range(TOP_K):
        rank = jnp.sum(jnp.where(lane == idxs[k], excl, 0.0), axis=-1, keepdims=True)
        out = jnp.where(lane == k, idxs[k], out)
        out = jnp.where(lane == TOP_K + k, es[k] * inv, out)
        out = jnp.where(lane == 2 * TOP_K + k, rank, out)
    o_ref[...] = out
    carry_ref[...] = carry_ref[...] + jnp.sum(onehot, axis=0, keepdims=True)
    cnt_ref[...] = carry_ref[...]


def route(logits):
    n_tok = logits.shape[0]
    return pl.pallas_call(
        _route_kernel,
        out_shape=(jax.ShapeDtypeStruct((n_tok, LANES), F32), jax.ShapeDtypeStruct((1, LANES), F32)),
        grid=(n_tok // ROW_TILE,),
        in_specs=[pl.BlockSpec((ROW_TILE, LANES), lambda i: (i, 0))],
        out_specs=(pl.BlockSpec((ROW_TILE, LANES), lambda i: (i, 0)),
                   pl.BlockSpec((1, LANES), lambda i: (0, 0))),
        scratch_shapes=[pltpu.VMEM((1, LANES), F32)],
        compiler_params=_params("arbitrary"),
        name="route",
    )(logits)


def _row_copy(src_ref, dst_ref, sem, src_row, dst_row):
    return pltpu.make_async_copy(src_ref.at[pl.ds(src_row, 1)], dst_ref.at[pl.ds(dst_row, 1)], sem)


DISPATCH_TILE = 128


def _dispatch_kernel(fill_ref, dest_ref, h_ref, o_hbm, zero_ref, fill_sem, sem):
    def fill_copy(blk):
        return pltpu.make_async_copy(zero_ref, o_hbm.at[pl.ds(blk * MOE_BLOCK, MOE_BLOCK)], fill_sem)

    @pl.when(pl.program_id(0) == 0)
    def _():
        zero_ref[...] = jnp.zeros_like(zero_ref)
        for s in range(fill_ref.shape[0]):
            @pl.when(fill_ref[s] >= 0)
            def _():
                fill_copy(fill_ref[s]).start()
        for s in range(fill_ref.shape[0]):
            @pl.when(fill_ref[s] >= 0)
            def _():
                fill_copy(0).wait()

    t = h_ref.shape[0]

    def start(r, carry):
        for k in range(TOP_K):
            _row_copy(h_ref, o_hbm, sem, r, dest_ref[0, 0, r * TOP_K + k]).start()
        return carry
    lax.fori_loop(0, t, start, 0)

    def wait(r, carry):
        for k in range(TOP_K):
            _row_copy(h_ref, o_hbm, sem, r, 0).wait()
        return carry
    lax.fori_loop(0, t, wait, 0)


def dispatch_rows(h2, dest, fill_blocks, cap):
    n_tok = h2.shape[0]
    t = DISPATCH_TILE
    n_tiles = n_tok // t
    grid_spec = pltpu.PrefetchScalarGridSpec(
        num_scalar_prefetch=1,
        grid=(n_tiles,),
        in_specs=[pl.BlockSpec((1, 1, t * TOP_K), lambda i, fb: (i, 0, 0), memory_space=pltpu.SMEM),
                  pl.BlockSpec((t, D_MODEL), lambda i, fb: (i, 0))],
        out_specs=pl.BlockSpec(memory_space=pl.ANY),
        scratch_shapes=[pltpu.VMEM((MOE_BLOCK, D_MODEL), F32), pltpu.SemaphoreType.DMA,
                        pltpu.SemaphoreType.DMA])
    return pl.pallas_call(
        _dispatch_kernel,
        out_shape=jax.ShapeDtypeStruct((cap, D_MODEL), F32),
        grid_spec=grid_spec,
        compiler_params=_params("arbitrary"),
        name="moe_dispatch",
    )(fill_blocks, dest.reshape(n_tiles, 1, t * TOP_K), h2)


EXPERT_COLS = 1024
DMA_CHUNKS = 4


def _expert_rows_pipeline(e, col, blk0_ref, nblk_ref, tail_ref, x_hbm, o_hbm, xbuf, obuf, zbuf,
                          sem_in, sem_out, sem_tail, compute, tail_max):
    n = nblk_ref[e]
    b0 = blk0_ref[e]
    width = obuf.shape[2]

    chunk = MOE_BLOCK // DMA_CHUNKS

    class _Copies:
        def __init__(self, copies):
            self.copies = copies

        def start(self):
            for c in self.copies:
                c.start()

        def wait(self):
            for c in self.copies:
                c.wait()

    def rows(blk, c):
        return pl.ds(pl.multiple_of((b0 + blk) * MOE_BLOCK + c * chunk, chunk), chunk)

    def x_copy(blk, slot):
        return _Copies([pltpu.make_async_copy(x_hbm.at[rows(blk, c)], xbuf.at[slot, pl.ds(c * chunk, chunk)],
                                              sem_in.at[slot]) for c in range(DMA_CHUNKS)])

    def o_copy(blk, slot):
        return _Copies([pltpu.make_async_copy(obuf.at[slot, pl.ds(c * chunk, chunk)],
                                              o_hbm.at[rows(blk, c), pl.ds(col, width)], sem_out.at[slot])
                        for c in range(DMA_CHUNKS)])

    @pl.when(n > 0)
    def _():
        x_copy(0, 0).start()

        def body(blk, carry):
            slot = lax.rem(blk, 2)
            x_copy(blk, slot).wait()

            @pl.when(blk + 1 < n)
            def _():
                x_copy(blk + 1, 1 - slot).start()

            @pl.when(blk >= 2)
            def _():
                o_copy(blk - 2, slot).wait()
            obuf[slot] = compute(xbuf[slot]).astype(obuf.dtype)
            o_copy(blk, slot).start()
            return carry
        lax.fori_loop(0, n, body, 0)

        @pl.when(n >= 2)
        def _():
            o_copy(n - 2, lax.rem(n, 2)).wait()
        o_copy(n - 1, lax.rem(n - 1, 2)).wait()

    @pl.when(e == N_EXPERTS - 1)
    def _():
        zbuf[...] = jnp.zeros_like(zbuf)
        first, count = tail_ref[0], tail_ref[1]

        def z_copy(t):
            dst = pl.ds(pl.multiple_of((first + t) * MOE_BLOCK, MOE_BLOCK), MOE_BLOCK)
            return pltpu.make_async_copy(zbuf, o_hbm.at[dst, pl.ds(col, width)], sem_tail)
        for t in range(tail_max):
            @pl.when(t < count)
            def _():
                z_copy(t).start()
        for t in range(tail_max):
            @pl.when(t < count)
            def _():
                z_copy(t).wait()


def _expert_up_kernel(blk0_ref, nblk_ref, tail_ref, x_hbm, wg_ref, wu_ref, bg_ref, bu_ref, o_hbm,
                      xbuf, obuf, zbuf, wgb_ref, wub_ref, sem_in, sem_out, sem_tail, *, tail_max):
    j, e = pl.program_id(0), pl.program_id(1)

    @pl.when(nblk_ref[e] > 0)
    def _():
        wgb_ref[...] = wg_ref[...].astype(BF16)
        wub_ref[...] = wu_ref[...].astype(BF16)

    def compute(x):
        xb = x.astype(BF16)
        gate = jnp.minimum(_dot(xb, wgb_ref[...]) + bg_ref[0], SWIGLU_LIMIT)
        up = jnp.clip(_dot(xb, wub_ref[...]) + bu_ref[0], -SWIGLU_LIMIT, SWIGLU_LIMIT)
        return gate * _sigmoid(SWIGLU_ALPHA * gate) * (up + 1.0)
    _expert_rows_pipeline(e, pl.multiple_of(j * EXPERT_COLS, EXPERT_COLS), blk0_ref, nblk_ref, tail_ref,
                          x_hbm, o_hbm, xbuf, obuf, zbuf, sem_in, sem_out, sem_tail, compute, tail_max)


def _expert_scratch(in_width, in_dtype, out_dtype):
    return [pltpu.VMEM((2, MOE_BLOCK, in_width), in_dtype),
            pltpu.VMEM((2, MOE_BLOCK, EXPERT_COLS), out_dtype),
            pltpu.VMEM((MOE_BLOCK, EXPERT_COLS), out_dtype)]


_EXPERT_SEMS = [pltpu.SemaphoreType.DMA((2,)), pltpu.SemaphoreType.DMA((2,)), pltpu.SemaphoreType.DMA]


def expert_up(xs, blk0, nblk, tail, tail_max, w_gate_up, b_gate_up):
    cap = xs.shape[0]
    tf = EXPERT_COLS
    nj = D_FF // tf
    b3 = b_gate_up.reshape(N_EXPERTS, 1, 2 * D_FF)
    grid_spec = pltpu.PrefetchScalarGridSpec(
        num_scalar_prefetch=3,
        grid=(nj, N_EXPERTS),
        in_specs=[pl.BlockSpec(memory_space=pl.ANY),
                  pl.BlockSpec((None, D_MODEL, tf), lambda j, e, *_: (e, 0, j)),
                  pl.BlockSpec((None, D_MODEL, tf), lambda j, e, *_: (e, 0, nj + j)),
                  pl.BlockSpec((None, 1, tf), lambda j, e, *_: (e, 0, j)),
                  pl.BlockSpec((None, 1, tf), lambda j, e, *_: (e, 0, nj + j))],
        out_specs=pl.BlockSpec(memory_space=pl.ANY),
        scratch_shapes=_expert_scratch(D_MODEL, F32, BF16)
        + [pltpu.VMEM((D_MODEL, tf), BF16), pltpu.VMEM((D_MODEL, tf), BF16)] + _EXPERT_SEMS)
    return pl.pallas_call(
        functools.partial(_expert_up_kernel, tail_max=tail_max),
        out_shape=jax.ShapeDtypeStruct((cap, D_FF), BF16),
        grid_spec=grid_spec,
        compiler_params=_params("arbitrary", "arbitrary"),
        name="expert_up",
    )(blk0, nblk, tail, xs, w_gate_up, w_gate_up, b3, b3)


def _expert_down_kernel(blk0_ref, nblk_ref, tail_ref, a_hbm, w_ref, b_ref, o_hbm,
                        xbuf, obuf, zbuf, wb_ref, sem_in, sem_out, sem_tail, *, tail_max):
    j, e = pl.program_id(0), pl.program_id(1)

    @pl.when(nblk_ref[e] > 0)
    def _():
        wb_ref[...] = w_ref[...].astype(BF16)

    def compute(a):
        return _dot(a, wb_ref[...]) + b_ref[0]
    _expert_rows_pipeline(e, pl.multiple_of(j * EXPERT_COLS, EXPERT_COLS), blk0_ref, nblk_ref, tail_ref,
                          a_hbm, o_hbm, xbuf, obuf, zbuf, sem_in, sem_out, sem_tail, compute, tail_max)


def expert_down(act, blk0, nblk, tail, tail_max, w_down, b_down):
    cap = act.shape[0]
    tn = EXPERT_COLS
    grid_spec = pltpu.PrefetchScalarGridSpec(
        num_scalar_prefetch=3,
        grid=(D_MODEL // tn, N_EXPERTS),
        in_specs=[pl.BlockSpec(memory_space=pl.ANY),
                  pl.BlockSpec((None, D_FF, tn), lambda j, e, *_: (e, 0, j)),
                  pl.BlockSpec((None, 1, tn), lambda j, e, *_: (e, 0, j))],
        out_specs=pl.BlockSpec(memory_space=pl.ANY),
        scratch_shapes=_expert_scratch(D_FF, BF16, F32) + [pltpu.VMEM((D_FF, tn), BF16)] + _EXPERT_SEMS)
    return pl.pallas_call(
        functools.partial(_expert_down_kernel, tail_max=tail_max),
        out_shape=jax.ShapeDtypeStruct((cap, D_MODEL), F32),
        grid_spec=grid_spec,
        compiler_params=_params("arbitrary", "arbitrary"),
        name="expert_down",
    )(blk0, nblk, tail, act, w_down, b_down.reshape(N_EXPERTS, 1, D_MODEL))


COMBINE_TILE = 128


def _combine_kernel(dest_ref, yb_hbm, x1_ref, w_ref, mod_ref, o_ref, buf_ref, sem):
    t = COMBINE_TILE

    def start(r, carry):
        for k in range(TOP_K):
            _row_copy(yb_hbm, buf_ref.at[k], sem, dest_ref[0, 0, r * TOP_K + k], r).start()
        return carry
    lax.fori_loop(0, t, start, 0)

    def wait(r, carry):
        for k in range(TOP_K):
            _row_copy(yb_hbm, buf_ref.at[k], sem, 0, r).wait()
        return carry
    lax.fori_loop(0, t, wait, 0)
    acc = w_ref[:, TOP_K:TOP_K + 1] * buf_ref[0]
    for k in range(1, TOP_K):
        acc = acc + w_ref[:, TOP_K + k:TOP_K + k + 1] * buf_ref[k]
    o_ref[...] = x1_ref[...] + mod_ref[0, 5:6, :] * acc


def combine(yb, dest, x1, route_out, mod, n_batch, seq):
    rows = n_batch * seq
    t = COMBINE_TILE
    n_tiles = rows // t
    return pl.pallas_call(
        _combine_kernel,
        out_shape=jax.ShapeDtypeStruct((rows, D_MODEL), F32),
        grid=(n_tiles,),
        in_specs=[pl.BlockSpec((1, 1, t * TOP_K), lambda i: (i, 0, 0), memory_space=pltpu.SMEM),
                  pl.BlockSpec(memory_space=pl.ANY),
                  pl.BlockSpec((t, D_MODEL), lambda i: (i, 0)),
                  pl.BlockSpec((t, LANES), lambda i: (i, 0)),
                  pl.BlockSpec((1, 8, D_MODEL), _mod_row_map(n_tiles, seq // t, n_batch))],
        out_specs=pl.BlockSpec((t, D_MODEL), lambda i: (i, 0)),
        scratch_shapes=[pltpu.VMEM((TOP_K, t, D_MODEL), F32), pltpu.SemaphoreType.DMA],
        compiler_params=_params("arbitrary"),
        name="moe_combine",
    )(dest.reshape(n_tiles, 1, t * TOP_K), yb, x1, route_out, mod)


def moe_layout(route_out, counts):
    n_tok = route_out.shape[0]
    idx = route_out[:, :TOP_K].astype(jnp.int32)
    rank = route_out[:, 2 * TOP_K:3 * TOP_K].astype(jnp.int32)
    cnt = counts[0, :N_EXPERTS].astype(jnp.int32)
    padded = (cnt + MOE_BLOCK - 1) // MOE_BLOCK * MOE_BLOCK
    pad_end = jnp.cumsum(padded)
    pad_start = pad_end - padded
    dest = (pad_start[idx] + rank).reshape(-1)
    n_blocks = -(-(n_tok * TOP_K + N_EXPERTS * (MOE_BLOCK - 1)) // MOE_BLOCK)
    n_used = pad_end[-1] // MOE_BLOCK
    tail_max = n_blocks - (n_tok * TOP_K) // MOE_BLOCK
    tail = jnp.stack([n_used, n_blocks - n_used]).astype(jnp.int32)
    last_blk = jnp.where(cnt > 0, pad_end // MOE_BLOCK - 1, -1)
    tail_blk = n_used + jnp.arange(tail_max, dtype=jnp.int32)
    fill_blocks = jnp.concatenate([last_blk, jnp.where(tail_blk < n_blocks, tail_blk, -1)]).astype(jnp.int32)
    blk0 = (pad_start // MOE_BLOCK).astype(jnp.int32)
    nblk = (padded // MOE_BLOCK).astype(jnp.int32)
    return dest, blk0, nblk, tail, tail_max, fill_blocks, n_blocks * MOE_BLOCK


def _in_proj_weights(w_in):
    sizes = (XBC_W, D_INNER, N_SSM_HEADS, N_SSM_HEADS, Q_W, KV_W, KV_W, D_MODEL, D_MODEL)
    offs = [0]
    for s in sizes:
        offs.append(offs[-1] + s)
    seg = lambda i: w_in[:, offs[i]:offs[i + 1]]
    w_main = jnp.concatenate([seg(1), seg(0), seg(4), seg(5), seg(6), seg(7), seg(8)], axis=1).astype(BF16)
    w_dt = jnp.concatenate([seg(2), seg(3)], axis=1).astype(BF16)
    return w_main, w_dt


def hybrid_layer(x, ctx, c, c_ctx, w_ada, b_ada, norm1_w, norm2_w, w_in, conv_w, conv_b, dt_bias_f, dt_bias_b,
                 a_log_f, a_log_b, d_skip, ssm_norm_w, q_norm_w, k_norm_w, sink, w_ssm_out, w_attn_out, w_o,
                 router_w, router_b, w_gate_up, b_gate_up, w_down, b_down):
    n_batch, seq, _ = x.shape
    ctx_len = ctx.shape[1]
    n_lat = n_batch * seq
    x2 = x.reshape(n_lat, D_MODEL)
    rows = n_lat + n_batch * ctx_len

    cvec = jnp.zeros((8, D_MODEL), F32).at[:n_batch].set(c).at[n_batch].set(c_ctx)
    mod = ada_modulation(cvec, w_ada, b_ada).reshape(8, 6, D_MODEL)
    mod = jnp.concatenate([mod, jnp.zeros((8, 2, D_MODEL), F32)], axis=1)

    hn = norm_modulate(x2, ctx.reshape(n_batch * ctx_len, D_MODEL), norm1_w, mod, n_batch, seq)
    w_main, w_dt = _in_proj_weights(w_in)
    proj = matmul(hn, w_main, 512, 1024, out_dtype=BF16, name="in_proj")
    dt_raw = matmul(hn, w_dt, 512, LANES, name="in_proj_dt")

    xbc = conv_silu(proj, conv_w, conv_b, n_batch, seq, ctx_len)
    dt_bias = jnp.concatenate([dt_bias_f, dt_bias_b]).reshape(1, LANES)
    a_neg = -jnp.exp(jnp.concatenate([a_log_f, a_log_b])).reshape(1, LANES)
    dskip_x = jnp.repeat(d_skip, SSM_HEAD_DIM).reshape(1, D_INNER)
    y_f = ssd_scan(xbc, dt_raw, dt_bias, a_neg, dskip_x, n_batch, seq, ctx_len, rev=False)
    y_b = ssd_scan(xbc, dt_raw, dt_bias, a_neg, dskip_x, n_batch, seq, ctx_len, rev=True)

    tables = _rope_tables(n_batch, seq, ctx_len)
    qn = qk_prep(proj, q_norm_w, tables, n_lat, COL_Q, Q_W, ATTN_SCALE, "q_prep")
    kn = qk_prep(proj, k_norm_w, tables, rows, COL_K, KV_W, 1.0, "k_prep")
    y_attn = windowed_attention(qn, kn, proj, sink, n_batch, seq, ctx_len)

    yg = gate_norm(y_f, y_b, proj, ssm_norm_w, n_lat)
    merged = merge_branches(yg, y_attn, w_ssm_out.astype(BF16), w_attn_out.astype(BF16), proj, n_lat)
    x1, h2, logits = out_proj_router(merged, w_o.astype(BF16), x2, mod, norm2_w, router_w, router_b,
                                     n_batch, seq)

    route_out, counts = route(logits)
    dest, blk0, nblk, tail, tail_max, fill_blocks, cap = moe_layout(route_out, counts)
    xs = dispatch_rows(h2, dest, fill_blocks, cap)
    act = expert_up(xs, blk0, nblk, tail, tail_max, w_gate_up, b_gate_up)
    yb = expert_down(act, blk0, nblk, tail, tail_max, w_down, b_down)
    out = combine(yb, dest, x1, route_out, mod, n_batch, seq)
    return out.reshape(n_batch, seq, D_MODEL)


def kernel(x, c, ctx, c_ctx, w_ada, b_ada, norm1_w, norm2_w, w_in, conv_w, conv_b, dt_bias_f, dt_bias_b,
           a_log_f, a_log_b, d_skip, ssm_norm_w, q_norm_w, k_norm_w, sink, w_ssm_out, w_attn_out, w_o,
           router_w, router_b, w_gate_up, b_gate_up, w_down, b_down):
    assert w_ada.shape[0] == 1, "single-layer block"
    return hybrid_layer(x, ctx, c, c_ctx, w_ada[0], b_ada[0], norm1_w[0], norm2_w[0], w_in[0], conv_w[0],
                        conv_b[0], dt_bias_f[0], dt_bias_b[0], a_log_f[0], a_log_b[0], d_skip[0],
                        ssm_norm_w[0], q_norm_w[0], k_norm_w[0], sink[0], w_ssm_out[0], w_attn_out[0], w_o[0],
                        router_w[0], router_b[0], w_gate_up[0], b_gate_up[0], w_down[0], b_down[0])
```

```python
import functools
import math

import jax
import jax.numpy as jnp
from jax import lax
from jax.experimental import pallas as pl
from jax.experimental.pallas import tpu as pltpu

F32 = jnp.float32
BF16 = jnp.bfloat16

D_MODEL = 2048
GRID_W = 64
NORM_EPS = 1e-6
D_INNER = 2 * D_MODEL
SSM_HEAD_DIM = 64
N_SSM_HEADS = D_INNER // SSM_HEAD_DIM
N_GROUPS = 8
HEADS_PER_GROUP = N_SSM_HEADS // N_GROUPS
D_STATE = 128
BC_W = N_GROUPS * D_STATE
XBC_W = D_INNER + 2 * BC_W
CONV_K = 5
SSD_CHUNK = 128
HEAD_DIM = 128
N_Q_HEADS = D_MODEL // HEAD_DIM
N_KV_HEADS = 4
Q_PER_KV = N_Q_HEADS // N_KV_HEADS
Q_W = N_Q_HEADS * HEAD_DIM
KV_W = N_KV_HEADS * HEAD_DIM
WINDOW = 128
ATTN_BLOCK = 128
ATTN_SCALE = HEAD_DIM ** -0.5
ROPE_FREQS = HEAD_DIM // 4
ROPE_BASE = 10000.0
N_EXPERTS = 32
TOP_K = 4
D_FF = D_MODEL
SWIGLU_LIMIT = 7.0
SWIGLU_ALPHA = 1.702
MOE_BLOCK = 512

LOG2_E = math.log2(math.e)
LANES = 128
ROW_TILE = 256
MAIN_W = D_INNER + XBC_W + Q_W + 2 * KV_W + 2 * D_MODEL
COL_Z, COL_XBC, COL_Q = 0, D_INNER, D_INNER + XBC_W
COL_K, COL_V = COL_Q + Q_W, COL_Q + Q_W + KV_W
COL_GS, COL_GA = COL_V + KV_W, COL_V + KV_W + D_MODEL
VMEM_LIMIT = 56 * 1024 * 1024


def _params(*sem, vmem_limit=VMEM_LIMIT):
    return pltpu.CompilerParams(dimension_semantics=sem, vmem_limit_bytes=vmem_limit)


def _dot(a, b):
    return jnp.dot(a, b, preferred_element_type=F32)


def _dot_nt(a, b):
    return lax.dot_general(a, b, (((1,), (1,)), ((), ())), preferred_element_type=F32)


def _split3(a):
    a1 = a.astype(BF16)
    r = a - a1.astype(F32)
    a2 = r.astype(BF16)
    a3 = (r - a2.astype(F32)).astype(BF16)
    return a1, a2, a3


def _dot_f32(a, b):
    a1, a2, _ = _split3(a)
    b1, b2, _ = _split3(b)
    return _dot(a1, b1) + (_dot(a1, b2) + _dot(a2, b1))


def _dot_sel_rhs(a, sel):
    a1, a2, a3 = _split3(a)
    return _dot(a1, sel) + (_dot(a2, sel) + _dot(a3, sel))


def _dot_sel_lhs(sel, a):
    a1, a2, a3 = _split3(a)
    return _dot(sel, a1) + (_dot(sel, a2) + _dot(sel, a3))


def _sigmoid(x):
    return 1.0 / (1.0 + jnp.exp(-x))


def _ada_kernel(c_ref, w_ref, b_ref, o_ref):
    c = c_ref[...]
    o_ref[...] = _dot_f32(c * _sigmoid(c), w_ref[...]) + b_ref[...]


def ada_modulation(cvec, w_ada, b_ada):
    n = w_ada.shape[1]
    tn = 1024
    return pl.pallas_call(
        _ada_kernel,
        out_shape=jax.ShapeDtypeStruct((8, n), F32),
        grid=(n // tn,),
        in_specs=[pl.BlockSpec((8, D_MODEL), lambda j: (0, 0)),
                  pl.BlockSpec((D_MODEL, tn), lambda j: (0, j)),
                  pl.BlockSpec((1, tn), lambda j: (0, j))],
        out_specs=pl.BlockSpec((8, tn), lambda j: (0, j)),
        compiler_params=_params("arbitrary"),
        name="ada_modulation",
    )(cvec, w_ada, b_ada.reshape(1, n))


def _norm_mod_kernel(x_ref, c_ref, w_ref, mod_ref, o_ref, *, n_lat_tiles):
    def emit(x):
        y = x * lax.rsqrt(jnp.mean(x * x, axis=-1, keepdims=True) + NORM_EPS) * w_ref[...]
        o_ref[...] = (y * (1.0 + mod_ref[0, 1:2, :]) + mod_ref[0, 0:1, :]).astype(o_ref.dtype)

    @pl.when(pl.program_id(0) < n_lat_tiles)
    def _():
        emit(x_ref[...])

    @pl.when(pl.program_id(0) >= n_lat_tiles)
    def _():
        emit(c_ref[...])


def _mod_row_map(n_lat_tiles, tiles_per_batch, n_batch):
    def index_map(i, *_):
        return (jnp.where(i < n_lat_tiles, i // tiles_per_batch, n_batch), 0, 0)
    return index_map


def norm_modulate(x2, ctx2, norm_w, mod, n_batch, seq):
    rows = x2.shape[0] + ctx2.shape[0]
    n_lat_tiles = n_batch * seq // ROW_TILE
    return pl.pallas_call(
        functools.partial(_norm_mod_kernel, n_lat_tiles=n_lat_tiles),
        out_shape=jax.ShapeDtypeStruct((rows, D_MODEL), BF16),
        grid=(rows // ROW_TILE,),
        in_specs=[pl.BlockSpec((ROW_TILE, D_MODEL), lambda i: (jnp.minimum(i, n_lat_tiles - 1), 0)),
                  pl.BlockSpec((ROW_TILE, D_MODEL), lambda i: (jnp.maximum(i - n_lat_tiles, 0), 0)),
                  pl.BlockSpec((1, D_MODEL), lambda i: (0, 0)),
                  pl.BlockSpec((1, 8, D_MODEL), _mod_row_map(n_lat_tiles, seq // ROW_TILE, n_batch))],
        out_specs=pl.BlockSpec((ROW_TILE, D_MODEL), lambda i: (i, 0)),
        compiler_params=_params("arbitrary"),
        name="norm_modulate",
    )(x2, ctx2, norm_w.reshape(1, D_MODEL), mod)


def _largest_tile(n, cap, multiple):
    return max(d for d in range(multiple, cap + 1, multiple) if n % d == 0)


def _in_proj_kernel(a_ref, b_ref, bdt_ref, o_ref, odt_ref):
    a = a_ref[...]
    o_ref[...] = _dot(a, b_ref[...]).astype(o_ref.dtype)

    @pl.when(pl.program_id(1) == 0)
    def _():
        odt_ref[...] = _dot(a, bdt_ref[...])


def in_proj(hn, w_main, w_dt):
    m, k = hn.shape
    n = w_main.shape[1]
    tm = _largest_tile(m, 2112, 16)
    tn = 1024
    return pl.pallas_call(
        _in_proj_kernel,
        out_shape=(jax.ShapeDtypeStruct((m, n), BF16), jax.ShapeDtypeStruct((m, LANES), F32)),
        grid=(m // tm, n // tn),
        in_specs=[pl.BlockSpec((tm, k), lambda i, j: (i, 0)),
                  pl.BlockSpec((k, tn), lambda i, j: (0, j)),
                  pl.BlockSpec((k, LANES), lambda i, j: (0, 0))],
        out_specs=(pl.BlockSpec((tm, tn), lambda i, j: (i, j)),
                   pl.BlockSpec((tm, LANES), lambda i, j: (i, 0))),
        compiler_params=_params("arbitrary", "arbitrary"),
        name="in_proj",
    )(hn, w_main, w_dt)


CONV_COLS = 2048
HALO = 16


def _conv_kernel(prev_ref, cur_ref, next_ref, w_ref, b_ref, o_ref, *,
                 n_lat_tiles, lat_tiles_per_seq, ctx_tiles_per_seq):
    i = pl.program_id(1)
    in_lat = i < n_lat_tiles
    pos = jnp.where(in_lat, i % lat_tiles_per_seq, (i - n_lat_tiles) % ctx_tiles_per_seq)
    per_seq = jnp.where(in_lat, lat_tiles_per_seq, ctx_tiles_per_seq)
    has_prev = pos > 0
    has_next = pos < per_seq - 1
    t = cur_ref.shape[0]
    n = t + 2 * HALO
    full = jnp.concatenate([jnp.where(has_prev, prev_ref[...].astype(F32), 0.0), cur_ref[...].astype(F32),
                            jnp.where(has_next, next_ref[...].astype(F32), 0.0)], axis=0)
    acc = jnp.broadcast_to(b_ref[...], (t, CONV_COLS))
    for k in range(CONV_K):
        shift = CONV_K // 2 - k
        shifted = full if shift == 0 else pltpu.roll(full, shift % n, 0)
        acc = acc + w_ref[k:k + 1, :] * shifted[HALO:HALO + t, :]
    o_ref[...] = acc * _sigmoid(acc)


def conv_silu(proj, conv_w, conv_b, n_batch, seq, ctx_len):
    rows = proj.shape[0]
    n_lat_tiles = n_batch * seq // ROW_TILE
    col0 = COL_XBC // CONV_COLS
    per = ROW_TILE // HALO
    last_halo = rows // HALO - 1
    kern = functools.partial(_conv_kernel, n_lat_tiles=n_lat_tiles,
                             lat_tiles_per_seq=seq // ROW_TILE, ctx_tiles_per_seq=ctx_len // ROW_TILE)
    w8 = jnp.concatenate([conv_w, jnp.zeros((8 - CONV_K, XBC_W), F32)], axis=0)
    return pl.pallas_call(
        kern,
        out_shape=jax.ShapeDtypeStruct((rows, XBC_W), F32),
        grid=(XBC_W // CONV_COLS, rows // ROW_TILE),
        in_specs=[pl.BlockSpec((HALO, CONV_COLS), lambda j, i: (jnp.maximum(i * per - 1, 0), col0 + j)),
                  pl.BlockSpec((ROW_TILE, CONV_COLS), lambda j, i: (i, col0 + j)),
                  pl.BlockSpec((HALO, CONV_COLS), lambda j, i: (jnp.minimum((i + 1) * per, last_halo), col0 + j)),
                  pl.BlockSpec((8, CONV_COLS), lambda j, i: (0, j)),
                  pl.BlockSpec((1, CONV_COLS), lambda j, i: (0, j))],
        out_specs=pl.BlockSpec((ROW_TILE, CONV_COLS), lambda j, i: (i, j)),
        compiler_params=_params("arbitrary", "arbitrary"),
        name="conv_silu",
    )(proj, proj, proj, w8, conv_b.reshape(1, XBC_W))


def _softplus(x):
    return jnp.maximum(x, 0.0) + jnp.log(1.0 + jnp.exp(-jnp.abs(x)))


def _ssd_kernel(xs_ref, b_ref, c_ref, dt_ref, bias_ref, a_ref, dskip_ref, y_ref, h_ref, *, rev):
    @pl.when(pl.program_id(1) == 0)
    def _():
        h_ref[...] = jnp.zeros_like(h_ref)

    L = SSD_CHUNK
    gw = HEADS_PER_GROUP * SSM_HEAD_DIM
    off = N_SSM_HEADS if rev else 0
    row = lax.broadcasted_iota(jnp.int32, (L, L), 0)
    col = lax.broadcasted_iota(jnp.int32, (L, L), 1)
    causal = (col >= row) if rev else (col <= row)
    first_head = col < SSM_HEAD_DIM
    tmat = jnp.where(causal, 1.0, 0.0).astype(BF16)

    dt = _softplus(dt_ref[...] + bias_ref[...])
    a = dt * (a_ref[...] * LOG2_E)
    acum = _dot_sel_lhs(tmat, a)
    acum_t = acum.T
    dt_t = dt.T
    last = 0 if rev else L - 1
    to_end_t = jnp.exp2(acum_t[:, last:last + 1] - acum_t) * dt_t
    src_t = acum_t - jnp.log2(dt_t)

    for g in range(N_GROUPS):
        bg = b_ref[:, g * D_STATE:(g + 1) * D_STATE]
        cgb = c_ref[:, g * D_STATE:(g + 1) * D_STATE].astype(BF16)
        cb = _dot_nt(cgb, bg.astype(BF16))
        bg_t = bg.T
        y_off = _dot(cgb, h_ref[:, g * gw:(g + 1) * gw].astype(BF16))
        for pair in range(HEADS_PER_GROUP // 2):
            c0 = g * gw + pair * LANES
            x_f = xs_ref[:, c0:c0 + LANES]
            x2 = jnp.concatenate([jnp.where(first_head, x_f, 0.0).astype(BF16),
                                  jnp.where(first_head, 0.0, x_f).astype(BF16)], axis=0)
            ms, ws, bcs = [], [], []
            for k in range(2):
                hd = off + g * HEADS_PER_GROUP + 2 * pair + k
                bc = jnp.broadcast_to(acum[:, hd:hd + 1], (L, L))
                decay_dt = jnp.exp2(jnp.where(causal, bc - src_t[hd:hd + 1, :], -jnp.inf))
                ms.append((cb * decay_dt).astype(BF16))
                ws.append((bg_t * to_end_t[hd:hd + 1, :]).astype(BF16))
                bcs.append(bc)
            lhs = jnp.concatenate([jnp.concatenate(ms, axis=1), jnp.concatenate(ws, axis=1)], axis=0)
            res = _dot(lhs, x2)
            e_t = jnp.exp2(jnp.where(first_head, bcs[0], bcs[1]))
            y = res[:L] + y_off[:, pair * LANES:(pair + 1) * LANES] * e_t
            if not rev:
                y = y + dskip_ref[:, c0:c0 + LANES] * x_f
            y_ref[:, c0:c0 + LANES] = y
            h_ref[:, c0:c0 + LANES] = e_t[last:last + 1, :] * h_ref[:, c0:c0 + LANES] + res[L:]


def ssd_scan(xbc, dt_raw, dt_bias, a_neg, dskip_x, n_batch, seq, ctx_len, rev):
    rows = xbc.shape[0]
    nc, ncc = seq // SSD_CHUNK, ctx_len // SSD_CHUNK
    lat_blocks = n_batch * nc

    def blk(b, j):
        cj = (ncc - 1 - j) if rev else j
        lj = (nc - 1 - (j - ncc)) if rev else (j - ncc)
        return jnp.where(j < ncc, lat_blocks + b * ncc + cj, b * nc + lj)

    return pl.pallas_call(
        functools.partial(_ssd_kernel, rev=rev),
        out_shape=jax.ShapeDtypeStruct((rows, D_INNER), F32),
        grid=(n_batch, ncc + nc),
        in_specs=[pl.BlockSpec((SSD_CHUNK, D_INNER), lambda b, j: (blk(b, j), 0)),
                  pl.BlockSpec((SSD_CHUNK, BC_W), lambda b, j: (blk(b, j), D_INNER // BC_W)),
                  pl.BlockSpec((SSD_CHUNK, BC_W), lambda b, j: (blk(b, j), D_INNER // BC_W + 1)),
                  pl.BlockSpec((SSD_CHUNK, LANES), lambda b, j: (blk(b, j), 0)),
                  pl.BlockSpec((1, LANES), lambda b, j: (0, 0)),
                  pl.BlockSpec((1, LANES), lambda b, j: (0, 0)),
                  pl.BlockSpec((1, D_INNER), lambda b, j: (0, 0))],
        out_specs=pl.BlockSpec((SSD_CHUNK, D_INNER), lambda b, j: (blk(b, j), 0)),
        scratch_shapes=[pltpu.VMEM((D_STATE, D_INNER), F32)],
        compiler_params=_params("arbitrary", "arbitrary"),
        name="ssd_scan_bwd" if rev else "ssd_scan_fwd",
    )(xbc, xbc, xbc, dt_raw, dt_bias, a_neg, dskip_x)


def _rope_tables(seq):
    inv_freq = ROPE_BASE ** (-jnp.arange(ROPE_FREQS, dtype=F32) / ROPE_FREQS)
    n_rows = seq // GRID_W
    r = jnp.repeat(jnp.arange(n_rows, dtype=F32), GRID_W)
    c = jnp.tile(jnp.arange(GRID_W, dtype=F32), n_rows)
    ar = r[:, None] * inv_freq
    ac = c[:, None] * inv_freq
    ang = jnp.concatenate([ar, ar, ac, ac], axis=-1)
    cos, sin = jnp.cos(ang), jnp.sin(ang)
    first_half = (jnp.arange(HEAD_DIM) % (2 * ROPE_FREQS)) < ROPE_FREQS
    sin_up = jnp.where(first_half, -sin, 0.0)
    sin_dn = jnp.where(first_half, 0.0, sin)
    return cos, sin_up, sin_dn


def _qk_prep_kernel(x_ref, w_ref, cos_ref, su_ref, sd_ref, o_ref, *, n_heads, scale, n_lat_tiles):
    rotate = pl.program_id(0) < n_lat_tiles
    cos = jnp.where(rotate, cos_ref[...], 1.0)
    su = jnp.where(rotate, su_ref[...], 0.0)
    sd = jnp.where(rotate, sd_ref[...], 0.0)
    w = w_ref[...]
    for h in range(n_heads):
        hs = slice(h * HEAD_DIM, (h + 1) * HEAD_DIM)
        x = x_ref[:, hs].astype(F32)
        n = x * lax.rsqrt(jnp.mean(x * x, axis=-1, keepdims=True) + NORM_EPS) * w
        y = n * cos + pltpu.roll(n, HEAD_DIM - ROPE_FREQS, 1) * su + pltpu.roll(n, ROPE_FREQS, 1) * sd
        if scale != 1.0:
            y = y * scale
        o_ref[:, hs] = y.astype(o_ref.dtype)


def qk_prep(proj, norm_w, tables, rows, n_lat, seq, col, width, scale, name):
    n_heads = width // HEAD_DIM
    tiles_per_seq = seq // ROW_TILE
    tab_spec = pl.BlockSpec((ROW_TILE, HEAD_DIM), lambda i: (i % tiles_per_seq, 0))
    return pl.pallas_call(
        functools.partial(_qk_prep_kernel, n_heads=n_heads, scale=scale, n_lat_tiles=n_lat // ROW_TILE),
        out_shape=jax.ShapeDtypeStruct((rows, width), BF16),
        grid=(rows // ROW_TILE,),
        in_specs=[pl.BlockSpec((ROW_TILE, width), lambda i: (i, col // width)),
                  pl.BlockSpec((1, HEAD_DIM), lambda i: (0, 0)),
                  tab_spec, tab_spec, tab_spec],
        out_specs=pl.BlockSpec((ROW_TILE, width), lambda i: (i, 0)),
        compiler_params=_params("arbitrary"),
        name=name,
    )(proj, norm_w.reshape(1, HEAD_DIM), *tables)


def _attn_kernel(q_ref, kp_ref, kc_ref, kn_ref, vp_ref, vc_ref, vn_ref, kx_ref, vx_ref, sink_ref, o_ref, *,
                 n_blocks):
    i = pl.program_id(1)
    T = ATTN_BLOCK
    nq = Q_PER_KV * T
    nk = 3 * T + kx_ref.shape[0]
    qi = lax.broadcasted_iota(jnp.int32, (nq, nk), 0) % T
    kj = lax.broadcasted_iota(jnp.int32, (nq, nk), 1)
    lo = jnp.maximum(qi, jnp.where(i > 0, 0, T))
    hi = jnp.minimum(qi + 2 * WINDOW, jnp.where(i < n_blocks - 1, 3 * T - 1, 2 * T - 1))
    valid = ((kj >= lo) & (kj <= hi)) | (kj >= 3 * T)
    for h in range(N_KV_HEADS):
        hs = slice(h * HEAD_DIM, (h + 1) * HEAD_DIM)
        q = jnp.concatenate([q_ref[:, (h * Q_PER_KV + g) * HEAD_DIM:(h * Q_PER_KV + g + 1) * HEAD_DIM]
                             for g in range(Q_PER_KV)], axis=0)
        kb = jnp.concatenate([kp_ref[:, hs], kc_ref[:, hs], kn_ref[:, hs], kx_ref[:, hs]], axis=0)
        vb = jnp.concatenate([vp_ref[:, hs], vc_ref[:, hs], vn_ref[:, hs], vx_ref[:, hs]], axis=0)
        s = jnp.where(valid, _dot_nt(q, kb), -jnp.inf)
        sk = jnp.concatenate([jnp.broadcast_to(sink_ref[0:1, h * Q_PER_KV + g:h * Q_PER_KV + g + 1], (T, 1))
                              for g in range(Q_PER_KV)], axis=0)
        mx = jnp.maximum(jnp.max(s, axis=-1, keepdims=True), sk)
        p = jnp.exp(s - mx)
        denom = jnp.sum(p, axis=-1, keepdims=True) + jnp.exp(sk - mx)
        o = _dot(p.astype(BF16), vb) * (1.0 / denom)
        for g in range(Q_PER_KV):
            c0 = (h * Q_PER_KV + g) * HEAD_DIM
            o_ref[:, c0:c0 + HEAD_DIM] = o[g * T:(g + 1) * T, :].astype(o_ref.dtype)


def windowed_attention(qn, kn, proj, sink, n_batch, seq, ctx_len):
    nb = seq // ATTN_BLOCK
    ctx0 = n_batch * seq // ctx_len
    sink_row = jnp.zeros((1, LANES), F32).at[0, :N_Q_HEADS].set(sink)
    v_col = COL_V // KV_W

    def kv_spec(d, col):
        return pl.BlockSpec((ATTN_BLOCK, KV_W), lambda b, i: (b * nb + jnp.clip(i + d, 0, nb - 1), col))

    def ctx_spec(col):
        return pl.BlockSpec((ctx_len, KV_W), lambda b, i: (ctx0 + b, col))
    return pl.pallas_call(
        functools.partial(_attn_kernel, n_blocks=nb),
        out_shape=jax.ShapeDtypeStruct((n_batch * seq, Q_W), BF16),
        grid=(n_batch, nb),
        in_specs=[pl.BlockSpec((ATTN_BLOCK, Q_W), lambda b, i: (b * nb + i, 0)),
                  kv_spec(-1, 0), kv_spec(0, 0), kv_spec(1, 0),
                  kv_spec(-1, v_col), kv_spec(0, v_col), kv_spec(1, v_col),
                  ctx_spec(0), ctx_spec(v_col),
                  pl.BlockSpec((1, LANES), lambda b, i: (0, 0))],
        out_specs=pl.BlockSpec((ATTN_BLOCK, Q_W), lambda b, i: (b * nb + i, 0)),
        compiler_params=_params("arbitrary", "arbitrary"),
        name="windowed_attention",
    )(qn, kn, kn, kn, proj, proj, proj, kn, proj, sink_row)


def _gate_norm_kernel(yf_ref, yb_ref, z_ref, w_ref, o_ref):
    gw = D_INNER // N_GROUPS
    for g in range(N_GROUPS):
        gs = slice(g * gw, (g + 1) * gw)
        z = z_ref[:, gs].astype(F32)
        s = (yf_ref[:, gs] + yb_ref[:, gs]) * (z * _sigmoid(z))
        n = s * lax.rsqrt(jnp.mean(s * s, axis=-1, keepdims=True) + NORM_EPS) * w_ref[:, gs]
        o_ref[:, gs] = n.astype(o_ref.dtype)


def gate_norm(y_f, y_b, proj, ssm_norm_w, rows):
    spec = pl.BlockSpec((ROW_TILE, D_INNER), lambda i: (i, 0))
    return pl.pallas_call(
        _gate_norm_kernel,
        out_shape=jax.ShapeDtypeStruct((rows, D_INNER), BF16),
        grid=(rows // ROW_TILE,),
        in_specs=[spec, spec, pl.BlockSpec((ROW_TILE, D_INNER), lambda i: (i, COL_Z // D_INNER)),
                  pl.BlockSpec((1, D_INNER), lambda i: (0, 0))],
        out_specs=spec,
        compiler_params=_params("arbitrary"),
        name="gate_norm",
    )(y_f, y_b, proj, ssm_norm_w.reshape(1, D_INNER))


def _merge_kernel(yg_ref, ya_ref, ws_ref, wa_ref, gs_ref, ga_ref, o_ref):
    o = (_sigmoid(gs_ref[...].astype(F32)) * _dot(yg_ref[...], ws_ref[...])
         + _sigmoid(ga_ref[...].astype(F32)) * _dot(ya_ref[...], wa_ref[...]))
    o_ref[...] = o.astype(o_ref.dtype)


def merge_branches(yg, ya, w_ssm_out, w_attn_out, proj, rows):
    tm, tn = 512, 1024
    return pl.pallas_call(
        _merge_kernel,
        out_shape=jax.ShapeDtypeStruct((rows, D_MODEL), BF16),
        grid=(D_MODEL // tn, rows // tm),
        in_specs=[pl.BlockSpec((tm, D_INNER), lambda j, i: (i, 0)),
                  pl.BlockSpec((tm, Q_W), lambda j, i: (i, 0)),
                  pl.BlockSpec((D_INNER, tn), lambda j, i: (0, j)),
                  pl.BlockSpec((Q_W, tn), lambda j, i: (0, j)),
                  pl.BlockSpec((tm, tn), lambda j, i: (i, COL_GS // tn + j)),
                  pl.BlockSpec((tm, tn), lambda j, i: (i, COL_GA // tn + j))],
        out_specs=pl.BlockSpec((tm, tn), lambda j, i: (i, j)),
        compiler_params=_params("arbitrary", "arbitrary"),
        name="merge_branches",
    )(yg, ya, w_ssm_out, w_attn_out, proj, proj)


def _out_proj_kernel(m_ref, wo_ref, x_ref, mod_ref, nw_ref, rw_ref, rb_ref, x1_ref, h2_ref, lg_ref):
    x1 = x_ref[...] + mod_ref[0, 2:3, :] * _dot(m_ref[...], wo_ref[...])
    x1_ref[...] = x1
    n = x1 * lax.rsqrt(jnp.mean(x1 * x1, axis=-1, keepdims=True) + NORM_EPS) * nw_ref[...]
    h2 = n * (1.0 + mod_ref[0, 4:5, :]) + mod_ref[0, 3:4, :]
    h2_ref[...] = h2
    lg_ref[...] = _dot_f32(h2, rw_ref[...]) + rb_ref[...]


def out_proj_router(merged, w_o, xa, mod, norm2_w, router_w, router_b, n_batch, seq):
    rows = n_batch * seq
    n_tiles = rows // ROW_TILE
    rw = jnp.zeros((D_MODEL, LANES), F32).at[:, :N_EXPERTS].set(router_w)
    rb = jnp.zeros((1, LANES), F32).at[0, :N_EXPERTS].set(router_b)
    tile = pl.BlockSpec((ROW_TILE, D_MODEL), lambda i: (i, 0))
    return pl.pallas_call(
        _out_proj_kernel,
        out_shape=(jax.ShapeDtypeStruct((rows, D_MODEL), F32),
                   jax.ShapeDtypeStruct((rows, D_MODEL), F32),
                   jax.ShapeDtypeStruct((rows, LANES), F32)),
        grid=(n_tiles,),
        in_specs=[tile,
                  pl.BlockSpec((D_MODEL, D_MODEL), lambda i: (0, 0)),
                  tile,
                  pl.BlockSpec((1, 8, D_MODEL), _mod_row_map(n_tiles, seq // ROW_TILE, n_batch)),
                  pl.BlockSpec((1, D_MODEL), lambda i: (0, 0)),
                  pl.BlockSpec((D_MODEL, LANES), lambda i: (0, 0)),
                  pl.BlockSpec((1, LANES), lambda i: (0, 0))],
        out_specs=(tile, tile, pl.BlockSpec((ROW_TILE, LANES), lambda i: (i, 0))),
        compiler_params=_params("arbitrary"),
        name="out_proj_router",
    )(merged, w_o, xa, mod, norm2_w.reshape(1, D_MODEL), rw, rb)


def _route_kernel(lg_ref, o_ref, cnt_ref, carry_ref):
    @pl.when(pl.program_id(0) == 0)
    def _():
        carry_ref[...] = jnp.zeros_like(carry_ref)

    t = lg_ref.shape[0]
    lane = lax.broadcasted_iota(jnp.int32, (t, LANES), 1).astype(F32)
    work = jnp.where(lane < N_EXPERTS, lg_ref[...], -jnp.inf)
    vals, idxs = [], []
    for _ in range(TOP_K):
        m = jnp.max(work, axis=-1, keepdims=True)
        idx = jnp.min(jnp.where(work == m, lane, float(LANES)), axis=-1, keepdims=True)
        vals.append(m)
        idxs.append(idx)
        work = jnp.where(lane == idx, -jnp.inf, work)
    es = [jnp.exp(v - vals[0]) for v in vals]
    inv = 1.0 / (es[0] + es[1] + es[2] + es[3])
    onehot = jnp.zeros((t, LANES), F32)
    for idx in idxs:
        onehot = onehot + jnp.where(lane == idx, 1.0, 0.0)
    r = lax.broadcasted_iota(jnp.int32, (t, t), 0)
    c = lax.broadcasted_iota(jnp.int32, (t, t), 1)
    before = jnp.where(c < r, 1.0, 0.0).astype(BF16)
    excl = _dot(before, onehot.astype(BF16)) + carry_ref[...]
    out = jnp.zeros((t, LANES), F32)
    for k in range(TOP_K):
        rank = jnp.sum(jnp.where(lane == idxs[k], excl, 0.0), axis=-1, keepdims=True)
        out = jnp.where(lane == k, idxs[k], out)
        out = jnp.where(lane == TOP_K + k, es[k] * inv, out)
        out = jnp.where(lane == 2 * TOP_K + k, rank, out)
    o_ref[...] = out
    carry_ref[...] = carry_ref[...] + jnp.sum(onehot, axis=0, keepdims=True)
    cnt_ref[...] = carry_ref[...]


def route(logits):
    n_tok = logits.shape[0]
    return pl.pallas_call(
        _route_kernel,
        out_shape=(jax.ShapeDtypeStruct((n_tok, LANES), F32), jax.ShapeDtypeStruct((1, LANES), F32)),
        grid=(n_tok // ROW_TILE,),
        in_specs=[pl.BlockSpec((ROW_TILE, LANES), lambda i: (i, 0))],
        out_specs=(pl.BlockSpec((ROW_TILE, LANES), lambda i: (i, 0)),
                   pl.BlockSpec((1, LANES), lambda i: (0, 0))),
        scratch_shapes=[pltpu.VMEM((1, LANES), F32)],
        compiler_params=_params("arbitrary"),
        name="route",
    )(logits)


def _row_copy(src_ref, dst_ref, sem, src_row, dst_row):
    return pltpu.make_async_copy(src_ref.at[pl.ds(src_row, 1)], dst_ref.at[pl.ds(dst_row, 1)], sem)


DISPATCH_TILE = 128


def _dispatch_kernel(fill_ref, dest_ref, h_ref, o_hbm, zero_ref, fill_sem, sem):
    def fill_copy(blk):
        return pltpu.make_async_copy(zero_ref, o_hbm.at[pl.ds(blk * MOE_BLOCK, MOE_BLOCK)], fill_sem)

    @pl.when(pl.program_id(0) == 0)
    def _():
        zero_ref[...] = jnp.zeros_like(zero_ref)
        for s in range(fill_ref.shape[0]):
            @pl.when(fill_ref[s] >= 0)
            def _():
                fill_copy(fill_ref[s]).start()
        for s in range(fill_ref.shape[0]):
            @pl.when(fill_ref[s] >= 0)
            def _():
                fill_copy(0).wait()

    t = h_ref.shape[0]

    def start(r, carry):
        for k in range(TOP_K):
            _row_copy(h_ref, o_hbm, sem, r, dest_ref[0, 0, r * TOP_K + k]).start()
        return carry
    lax.fori_loop(0, t, start, 0, unroll=4)

    def wait(r, carry):
        for k in range(TOP_K):
            _row_copy(h_ref, o_hbm, sem, r, 0).wait()
        return carry
    lax.fori_loop(0, t, wait, 0, unroll=8)


def dispatch_rows(h2, dest, fill_blocks, cap):
    n_tok = h2.shape[0]
    t = DISPATCH_TILE
    n_tiles = n_tok // t
    grid_spec = pltpu.PrefetchScalarGridSpec(
        num_scalar_prefetch=1,
        grid=(n_tiles,),
        in_specs=[pl.BlockSpec((1, 1, t * TOP_K), lambda i, fb: (i, 0, 0), memory_space=pltpu.SMEM),
                  pl.BlockSpec((t, D_MODEL), lambda i, fb: (i, 0))],
        out_specs=pl.BlockSpec(memory_space=pl.ANY),
        scratch_shapes=[pltpu.VMEM((MOE_BLOCK, D_MODEL), F32), pltpu.SemaphoreType.DMA,
                        pltpu.SemaphoreType.DMA])
    return pl.pallas_call(
        _dispatch_kernel,
        out_shape=jax.ShapeDtypeStruct((cap, D_MODEL), F32),
        grid_spec=grid_spec,
        compiler_params=_params("arbitrary"),
        name="moe_dispatch",
    )(fill_blocks, dest.reshape(n_tiles, 1, t * TOP_K), h2)


EXPERT_COLS = 1024
DMA_CHUNKS = 4


def _expert_rows_pipeline(e, col, blk0_ref, nblk_ref, tail_ref, x_hbm, o_hbm, xbuf, obuf, zbuf,
                          sem_in, sem_out, sem_tail, compute, tail_max):
    n = nblk_ref[e]
    b0 = blk0_ref[e]
    width = obuf.shape[2]

    chunk = MOE_BLOCK // DMA_CHUNKS

    class _Copies:
        def __init__(self, copies):
            self.copies = copies

        def start(self):
            for c in self.copies:
                c.start()

        def wait(self):
            for c in self.copies:
                c.wait()

    def rows(blk, c):
        return pl.ds(pl.multiple_of((b0 + blk) * MOE_BLOCK + c * chunk, chunk), chunk)

    def x_copy(blk, slot):
        return _Copies([pltpu.make_async_copy(x_hbm.at[rows(blk, c)], xbuf.at[slot, pl.ds(c * chunk, chunk)],
                                              sem_in.at[slot]) for c in range(DMA_CHUNKS)])

    def o_copy(blk, slot):
        return _Copies([pltpu.make_async_copy(obuf.at[slot, pl.ds(c * chunk, chunk)],
                                              o_hbm.at[rows(blk, c), pl.ds(col, width)], sem_out.at[slot])
                        for c in range(DMA_CHUNKS)])

    @pl.when(n > 0)
    def _():
        x_copy(0, 0).start()

        def body(blk, carry):
            slot = lax.rem(blk, 2)
            x_copy(blk, slot).wait()

            @pl.when(blk + 1 < n)
            def _():
                x_copy(blk + 1, 1 - slot).start()

            @pl.when(blk >= 2)
            def _():
                o_copy(blk - 2, slot).wait()
            obuf[slot] = compute(xbuf[slot]).astype(obuf.dtype)
            o_copy(blk, slot).start()
            return carry
        lax.fori_loop(0, n, body, 0)

        @pl.when(n >= 2)
        def _():
            o_copy(n - 2, lax.rem(n, 2)).wait()
        o_copy(n - 1, lax.rem(n - 1, 2)).wait()

    @pl.when(e == N_EXPERTS - 1)
    def _():
        zbuf[...] = jnp.zeros_like(zbuf)
        first, count = tail_ref[0], tail_ref[1]

        def z_copy(t):
            dst = pl.ds(pl.multiple_of((first + t) * MOE_BLOCK, MOE_BLOCK), MOE_BLOCK)
            return pltpu.make_async_copy(zbuf, o_hbm.at[dst, pl.ds(col, width)], sem_tail)
        for t in range(tail_max):
            @pl.when(t < count)
            def _():
                z_copy(t).start()
        for t in range(tail_max):
            @pl.when(t < count)
            def _():
                z_copy(t).wait()


def _expert_up_kernel(blk0_ref, nblk_ref, tail_ref, x_hbm, wg_ref, wu_ref, bg_ref, bu_ref, o_hbm,
                      xbuf, obuf, zbuf, wgb_ref, wub_ref, sem_in, sem_out, sem_tail, *, tail_max):
    j, e = pl.program_id(0), pl.program_id(1)

    @pl.when(nblk_ref[e] > 0)
    def _():
        wgb_ref[...] = wg_ref[...].astype(BF16)
        wub_ref[...] = wu_ref[...].astype(BF16)

    def compute(x):
        xb = x.astype(BF16)
        gate = jnp.minimum(_dot(xb, wgb_ref[...]) + bg_ref[0], SWIGLU_LIMIT)
        up = jnp.clip(_dot(xb, wub_ref[...]) + bu_ref[0], -SWIGLU_LIMIT, SWIGLU_LIMIT)
        return gate * _sigmoid(SWIGLU_ALPHA * gate) * (up + 1.0)
    _expert_rows_pipeline(e, pl.multiple_of(j * EXPERT_COLS, EXPERT_COLS), blk0_ref, nblk_ref, tail_ref,
                          x_hbm, o_hbm, xbuf, obuf, zbuf, sem_in, sem_out, sem_tail, compute, tail_max)


def _expert_scratch(in_width, in_dtype, out_dtype):
    return [pltpu.VMEM((2, MOE_BLOCK, in_width), in_dtype),
            pltpu.VMEM((2, MOE_BLOCK, EXPERT_COLS), out_dtype),
            pltpu.VMEM((MOE_BLOCK, EXPERT_COLS), out_dtype)]


_EXPERT_SEMS = [pltpu.SemaphoreType.DMA((2,)), pltpu.SemaphoreType.DMA((2,)), pltpu.SemaphoreType.DMA]


def expert_up(xs, blk0, nblk, tail, tail_max, w_gate_up, b_gate_up):
    cap = xs.shape[0]
    tf = EXPERT_COLS
    nj = D_FF // tf
    b3 = b_gate_up.reshape(N_EXPERTS, 1, 2 * D_FF)
    grid_spec = pltpu.PrefetchScalarGridSpec(
        num_scalar_prefetch=3,
        grid=(nj, N_EXPERTS),
        in_specs=[pl.BlockSpec(memory_space=pl.ANY),
                  pl.BlockSpec((None, D_MODEL, tf), lambda j, e, *_: (e, 0, j)),
                  pl.BlockSpec((None, D_MODEL, tf), lambda j, e, *_: (e, 0, nj + j)),
                  pl.BlockSpec((None, 1, tf), lambda j, e, *_: (e, 0, j)),
                  pl.BlockSpec((None, 1, tf), lambda j, e, *_: (e, 0, nj + j))],
        out_specs=pl.BlockSpec(memory_space=pl.ANY),
        scratch_shapes=_expert_scratch(D_MODEL, F32, BF16)
        + [pltpu.VMEM((D_MODEL, tf), BF16), pltpu.VMEM((D_MODEL, tf), BF16)] + _EXPERT_SEMS)
    return pl.pallas_call(
        functools.partial(_expert_up_kernel, tail_max=tail_max),
        out_shape=jax.ShapeDtypeStruct((cap, D_FF), BF16),
        grid_spec=grid_spec,
        compiler_params=_params("arbitrary", "arbitrary"),
        name="expert_up",
    )(blk0, nblk, tail, xs, w_gate_up, w_gate_up, b3, b3)


def _expert_down_kernel(blk0_ref, nblk_ref, tail_ref, a_hbm, w_ref, b_ref, o_hbm,
                        xbuf, obuf, zbuf, wb_ref, sem_in, sem_out, sem_tail, *, tail_max):
    j, e = pl.program_id(0), pl.program_id(1)

    @pl.when(nblk_ref[e] > 0)
    def _():
        wb_ref[...] = w_ref[...].astype(BF16)

    def compute(a):
        return _dot(a, wb_ref[...]) + b_ref[0]
    _expert_rows_pipeline(e, pl.multiple_of(j * EXPERT_COLS, EXPERT_COLS), blk0_ref, nblk_ref, tail_ref,
                          a_hbm, o_hbm, xbuf, obuf, zbuf, sem_in, sem_out, sem_tail, compute, tail_max)


def expert_down(act, blk0, nblk, tail, tail_max, w_down, b_down):
    cap = act.shape[0]
    tn = EXPERT_COLS
    grid_spec = pltpu.PrefetchScalarGridSpec(
        num_scalar_prefetch=3,
        grid=(D_MODEL // tn, N_EXPERTS),
        in_specs=[pl.BlockSpec(memory_space=pl.ANY),
                  pl.BlockSpec((None, D_FF, tn), lambda j, e, *_: (e, 0, j)),
                  pl.BlockSpec((None, 1, tn), lambda j, e, *_: (e, 0, j))],
        out_specs=pl.BlockSpec(memory_space=pl.ANY),
        scratch_shapes=_expert_scratch(D_FF, BF16, F32) + [pltpu.VMEM((D_FF, tn), BF16)] + _EXPERT_SEMS)
    return pl.pallas_call(
        functools.partial(_expert_down_kernel, tail_max=tail_max),
        out_shape=jax.ShapeDtypeStruct((cap, D_MODEL), F32),
        grid_spec=grid_spec,
        compiler_params=_params("arbitrary", "arbitrary"),
        name="expert_down",
    )(blk0, nblk, tail, act, w_down, b_down.reshape(N_EXPERTS, 1, D_MODEL))


COMBINE_TILE = 128


def _combine_kernel(dest_ref, yb_hbm, x1_ref, w_ref, mod_ref, o_ref, buf_ref, sem):
    t = COMBINE_TILE

    def start(r, carry):
        for k in range(TOP_K):
            _row_copy(yb_hbm, buf_ref.at[k], sem, dest_ref[0, 0, r * TOP_K + k], r).start()
        return carry
    lax.fori_loop(0, t, start, 0, unroll=4)

    def wait(r, carry):
        for k in range(TOP_K):
            _row_copy(yb_hbm, buf_ref.at[k], sem, 0, r).wait()
        return carry
    lax.fori_loop(0, t, wait, 0, unroll=8)
    acc = w_ref[:, TOP_K:TOP_K + 1] * buf_ref[0]
    for k in range(1, TOP_K):
        acc = acc + w_ref[:, TOP_K + k:TOP_K + k + 1] * buf_ref[k]
    o_ref[...] = x1_ref[...] + mod_ref[0, 5:6, :] * acc


def combine(yb, dest, x1, route_out, mod, n_batch, seq):
    rows = n_batch * seq
    t = COMBINE_TILE
    n_tiles = rows // t
    return pl.pallas_call(
        _combine_kernel,
        out_shape=jax.ShapeDtypeStruct((rows, D_MODEL), F32),
        grid=(n_tiles,),
        in_specs=[pl.BlockSpec((1, 1, t * TOP_K), lambda i: (i, 0, 0), memory_space=pltpu.SMEM),
                  pl.BlockSpec(memory_space=pl.ANY),
                  pl.BlockSpec((t, D_MODEL), lambda i: (i, 0)),
                  pl.BlockSpec((t, LANES), lambda i: (i, 0)),
                  pl.BlockSpec((1, 8, D_MODEL), _mod_row_map(n_tiles, seq // t, n_batch))],
        out_specs=pl.BlockSpec((t, D_MODEL), lambda i: (i, 0)),
        scratch_shapes=[pltpu.VMEM((TOP_K, t, D_MODEL), F32), pltpu.SemaphoreType.DMA],
        compiler_params=_params("arbitrary"),
        name="moe_combine",
    )(dest.reshape(n_tiles, 1, t * TOP_K), yb, x1, route_out, mod)


def moe_layout(route_out, counts):
    n_tok = route_out.shape[0]
    idx = route_out[:, :TOP_K].astype(jnp.int32)
    rank = route_out[:, 2 * TOP_K:3 * TOP_K].astype(jnp.int32)
    cnt = counts[0, :N_EXPERTS].astype(jnp.int32)
    padded = (cnt + MOE_BLOCK - 1) // MOE_BLOCK * MOE_BLOCK
    pad_end = jnp.cumsum(padded)
    pad_start = pad_end - padded
    dest = (pad_start[idx] + rank).reshape(-1)
    n_blocks = -(-(n_tok * TOP_K + N_EXPERTS * (MOE_BLOCK - 1)) // MOE_BLOCK)
    n_used = pad_end[-1] // MOE_BLOCK
    tail_max = n_blocks - (n_tok * TOP_K) // MOE_BLOCK
    tail = jnp.stack([n_used, n_blocks - n_used]).astype(jnp.int32)
    last_blk = jnp.where(cnt > 0, pad_end // MOE_BLOCK - 1, -1)
    tail_blk = n_used + jnp.arange(tail_max, dtype=jnp.int32)
    fill_blocks = jnp.concatenate([last_blk, jnp.where(tail_blk < n_blocks, tail_blk, -1)]).astype(jnp.int32)
    blk0 = (pad_start // MOE_BLOCK).astype(jnp.int32)
    nblk = (padded // MOE_BLOCK).astype(jnp.int32)
    return dest, blk0, nblk, tail, tail_max, fill_blocks, n_blocks * MOE_BLOCK


def _in_proj_weights(w_in):
    sizes = (XBC_W, D_INNER, N_SSM_HEADS, N_SSM_HEADS, Q_W, KV_W, KV_W, D_MODEL, D_MODEL)
    offs = [0]
    for s in sizes:
        offs.append(offs[-1] + s)
    seg = lambda i: w_in[:, offs[i]:offs[i + 1]]
    w_main = jnp.concatenate([seg(1), seg(0), seg(4), seg(5), seg(6), seg(7), seg(8)], axis=1).astype(BF16)
    w_dt = jnp.concatenate([seg(2), seg(3)], axis=1).astype(BF16)
    return w_main, w_dt


def hybrid_layer(x, ctx, c, c_ctx, w_ada, b_ada, norm1_w, norm2_w, w_in, conv_w, conv_b, dt_bias_f, dt_bias_b,
                 a_log_f, a_log_b, d_skip, ssm_norm_w, q_norm_w, k_norm_w, sink, w_ssm_out, w_attn_out, w_o,
                 router_w, router_b, w_gate_up, b_gate_up, w_down, b_down):
    n_batch, seq, _ = x.shape
    ctx_len = ctx.shape[1]
    n_lat = n_batch * seq
    x2 = x.reshape(n_lat, D_MODEL)
    rows = n_lat + n_batch * ctx_len

    cvec = jnp.zeros((8, D_MODEL), F32).at[:n_batch].set(c).at[n_batch].set(c_ctx)
    mod = ada_modulation(cvec, w_ada, b_ada).reshape(8, 6, D_MODEL)
    mod = jnp.concatenate([mod, jnp.zeros((8, 2, D_MODEL), F32)], axis=1)

    hn = norm_modulate(x2, ctx.reshape(n_batch * ctx_len, D_MODEL), norm1_w, mod, n_batch, seq)
    w_main, w_dt = _in_proj_weights(w_in)
    proj, dt_raw = in_proj(hn, w_main, w_dt)

    xbc = conv_silu(proj, conv_w, conv_b, n_batch, seq, ctx_len)
    dt_bias = jnp.concatenate([dt_bias_f, dt_bias_b]).reshape(1, LANES)
    a_neg = -jnp.exp(jnp.concatenate([a_log_f, a_log_b])).reshape(1, LANES)
    dskip_x = jnp.repeat(d_skip, SSM_HEAD_DIM).reshape(1, D_INNER)
    y_f = ssd_scan(xbc, dt_raw, dt_bias, a_neg, dskip_x, n_batch, seq, ctx_len, rev=False)
    y_b = ssd_scan(xbc, dt_raw, dt_bias, a_neg, dskip_x, n_batch, seq, ctx_len, rev=True)

    tables = _rope_tables(seq)
    qn = qk_prep(proj, q_norm_w, tables, n_lat, n_lat, seq, COL_Q, Q_W, ATTN_SCALE, "q_prep")
    kn = qk_prep(proj, k_norm_w, tables, rows, n_lat, seq, COL_K, KV_W, 1.0, "k_prep")
    y_attn = windowed_attention(qn, kn, proj, sink, n_batch, seq, ctx_len)

    yg = gate_norm(y_f, y_b, proj, ssm_norm_w, n_lat)
    merged = merge_branches(yg, y_attn, w_ssm_out.astype(BF16), w_attn_out.astype(BF16), proj, n_lat)
    x1, h2, logits = out_proj_router(merged, w_o.astype(BF16), x2, mod, norm2_w, router_w, router_b,
                                     n_batch, seq)

    route_out, counts = route(logits)
    dest, blk0, nblk, tail, tail_max, fill_blocks, cap = moe_layout(route_out, counts)
    xs = dispatch_rows(h2, dest, fill_blocks, cap)
    act = expert_up(xs, blk0, nblk, tail, tail_max, w_gate_up, b_gate_up)
    yb = expert_down(act, blk0, nblk, tail, tail_max, w_down, b_down)
    out = combine(yb, dest, x1, route_out, mod, n_batch, seq)
    return out.reshape(n_batch, seq, D_MODEL)


def kernel(x, c, ctx, c_ctx, w_ada, b_ada, norm1_w, norm2_w, w_in, conv_w, conv_b, dt_bias_f, dt_bias_b,
           a_log_f, a_log_b, d_skip, ssm_norm_w, q_norm_w, k_norm_w, sink, w_ssm_out, w_attn_out, w_o,
           router_w, router_b, w_gate_up, b_gate_up, w_down, b_down):
    assert w_ada.shape[0] == 1, "single-layer block"
    return hybrid_layer(x, ctx, c, c_ctx, w_ada[0], b_ada[0], norm1_w[0], norm2_w[0], w_in[0], conv_w[0],
                        conv_b[0], dt_bias_f[0], dt_bias_b[0], a_log_f[0], a_log_b[0], d_skip[0],
                        ssm_norm_w[0], q_norm_w[0], k_norm_w[0], sink[0], w_ssm_out[0], w_attn_out[0], w_o[0],
                        router_w[0], router_b[0], w_gate_up[0], b_gate_up[0], w_down[0], b_down[0])
```

```python
import functools
import math

import jax
import jax.numpy as jnp
from jax import lax
from jax.experimental import pallas as pl
from jax.experimental.pallas import tpu as pltpu

F32 = jnp.float32
BF16 = jnp.bfloat16

D_MODEL = 2048
GRID_W = 64
NORM_EPS = 1e-6
D_INNER = 2 * D_MODEL
SSM_HEAD_DIM = 64
N_SSM_HEADS = D_INNER // SSM_HEAD_DIM
N_GROUPS = 8
HEADS_PER_GROUP = N_SSM_HEADS // N_GROUPS
D_STATE = 128
BC_W = N_GROUPS * D_STATE
XBC_W = D_INNER + 2 * BC_W
CONV_K = 5
SSD_CHUNK = 128
HEAD_DIM = 128
N_Q_HEADS = D_MODEL // HEAD_DIM
N_KV_HEADS = 4
Q_PER_KV = N_Q_HEADS // N_KV_HEADS
Q_W = N_Q_HEADS * HEAD_DIM
KV_W = N_KV_HEADS * HEAD_DIM
WINDOW = 128
ATTN_BLOCK = 128
ATTN_SCALE = HEAD_DIM ** -0.5
ROPE_FREQS = HEAD_DIM // 4
ROPE_BASE = 10000.0
N_EXPERTS = 32
TOP_K = 4
D_FF = D_MODEL
SWIGLU_LIMIT = 7.0
SWIGLU_ALPHA = 1.702
MOE_BLOCK = 512

LOG2_E = math.log2(math.e)
LANES = 128
ROW_TILE = 256
MAIN_W = D_INNER + XBC_W + Q_W + 2 * KV_W + 2 * D_MODEL
COL_Z, COL_XBC, COL_Q = 0, D_INNER, D_INNER + XBC_W
COL_K, COL_V = COL_Q + Q_W, COL_Q + Q_W + KV_W
COL_GS, COL_GA = COL_V + KV_W, COL_V + KV_W + D_MODEL
VMEM_LIMIT = 56 * 1024 * 1024


def _params(*sem, vmem_limit=VMEM_LIMIT):
    return pltpu.CompilerParams(dimension_semantics=sem, vmem_limit_bytes=vmem_limit)


def _dot(a, b):
    return jnp.dot(a, b, preferred_element_type=F32)


def _dot_nt(a, b):
    return lax.dot_general(a, b, (((1,), (1,)), ((), ())), preferred_element_type=F32)


def _split3(a):
    a1 = a.astype(BF16)
    r = a - a1.astype(F32)
    a2 = r.astype(BF16)
    a3 = (r - a2.astype(F32)).astype(BF16)
    return a1, a2, a3


def _dot_f32(a, b):
    a1, a2, _ = _split3(a)
    b1, b2, _ = _split3(b)
    return _dot(a1, b1) + (_dot(a1, b2) + _dot(a2, b1))


def _dot_sel_rhs(a, sel):
    a1, a2, a3 = _split3(a)
    return _dot(a1, sel) + (_dot(a2, sel) + _dot(a3, sel))


def _dot_sel_lhs(sel, a):
    a1, a2, a3 = _split3(a)
    return _dot(sel, a1) + (_dot(sel, a2) + _dot(sel, a3))


def _sigmoid(x):
    return 1.0 / (1.0 + jnp.exp(-x))


U32 = jnp.uint32
PACK_W = D_MODEL // 2


def _pack_bf16_pairs(lo, hi):
    lo_bits = lax.bitcast_convert_type(lo.astype(BF16).astype(F32), U32)
    hi_bits = lax.bitcast_convert_type(hi.astype(BF16).astype(F32), U32)
    return (lo_bits >> 16) | (hi_bits & U32(0xFFFF0000))


def _unpack_bf16_pairs(words):
    lo = lax.bitcast_convert_type(words << 16, F32)
    hi = lax.bitcast_convert_type(words & U32(0xFFFF0000), F32)
    return lo, hi


def _ada_kernel(c_ref, w_ref, b_ref, o_ref):
    c = c_ref[...]
    o_ref[...] = _dot_f32(c * _sigmoid(c), w_ref[...]) + b_ref[...]


def ada_modulation(cvec, w_ada, b_ada):
    n = w_ada.shape[1]
    tn = 1024
    return pl.pallas_call(
        _ada_kernel,
        out_shape=jax.ShapeDtypeStruct((8, n), F32),
        grid=(n // tn,),
        in_specs=[pl.BlockSpec((8, D_MODEL), lambda j: (0, 0)),
                  pl.BlockSpec((D_MODEL, tn), lambda j: (0, j)),
                  pl.BlockSpec((1, tn), lambda j: (0, j))],
        out_specs=pl.BlockSpec((8, tn), lambda j: (0, j)),
        compiler_params=_params("arbitrary"),
        name="ada_modulation",
    )(cvec, w_ada, b_ada.reshape(1, n))


def _norm_mod_kernel(x_ref, c_ref, w_ref, mod_ref, o_ref, *, n_lat_tiles):
    def emit(x):
        y = x * lax.rsqrt(jnp.mean(x * x, axis=-1, keepdims=True) + NORM_EPS) * w_ref[...]
        o_ref[...] = (y * (1.0 + mod_ref[0, 1:2, :]) + mod_ref[0, 0:1, :]).astype(o_ref.dtype)

    @pl.when(pl.program_id(0) < n_lat_tiles)
    def _():
        emit(x_ref[...])

    @pl.when(pl.program_id(0) >= n_lat_tiles)
    def _():
        emit(c_ref[...])


def _mod_row_map(n_lat_tiles, tiles_per_batch, n_batch):
    def index_map(i, *_):
        return (jnp.where(i < n_lat_tiles, i // tiles_per_batch, n_batch), 0, 0)
    return index_map


def norm_modulate(x2, ctx2, norm_w, mod, n_batch, seq):
    rows = x2.shape[0] + ctx2.shape[0]
    n_lat_tiles = n_batch * seq // ROW_TILE
    return pl.pallas_call(
        functools.partial(_norm_mod_kernel, n_lat_tiles=n_lat_tiles),
        out_shape=jax.ShapeDtypeStruct((rows, D_MODEL), BF16),
        grid=(rows // ROW_TILE,),
        in_specs=[pl.BlockSpec((ROW_TILE, D_MODEL), lambda i: (jnp.minimum(i, n_lat_tiles - 1), 0)),
                  pl.BlockSpec((ROW_TILE, D_MODEL), lambda i: (jnp.maximum(i - n_lat_tiles, 0), 0)),
                  pl.BlockSpec((1, D_MODEL), lambda i: (0, 0)),
                  pl.BlockSpec((1, 8, D_MODEL), _mod_row_map(n_lat_tiles, seq // ROW_TILE, n_batch))],
        out_specs=pl.BlockSpec((ROW_TILE, D_MODEL), lambda i: (i, 0)),
        compiler_params=_params("arbitrary"),
        name="norm_modulate",
    )(x2, ctx2, norm_w.reshape(1, D_MODEL), mod)


def _largest_tile(n, cap, multiple):
    return max(d for d in range(multiple, cap + 1, multiple) if n % d == 0)


def _in_proj_kernel(a_ref, b_ref, bdt_ref, o_ref, odt_ref):
    a = a_ref[...]
    o_ref[...] = _dot(a, b_ref[...]).astype(o_ref.dtype)

    @pl.when(pl.program_id(1) == 0)
    def _():
        odt_ref[...] = _dot(a, bdt_ref[...])


def in_proj(hn, w_main, w_dt):
    m, k = hn.shape
    n = w_main.shape[1]
    tm = _largest_tile(m, 2112, 16)
    tn = 1024
    return pl.pallas_call(
        _in_proj_kernel,
        out_shape=(jax.ShapeDtypeStruct((m, n), BF16), jax.ShapeDtypeStruct((m, LANES), F32)),
        grid=(m // tm, n // tn),
        in_specs=[pl.BlockSpec((tm, k), lambda i, j: (i, 0)),
                  pl.BlockSpec((k, tn), lambda i, j: (0, j)),
                  pl.BlockSpec((k, LANES), lambda i, j: (0, 0))],
        out_specs=(pl.BlockSpec((tm, tn), lambda i, j: (i, j)),
                   pl.BlockSpec((tm, LANES), lambda i, j: (i, 0))),
        compiler_params=_params("arbitrary", "arbitrary"),
        name="in_proj",
    )(hn, w_main, w_dt)


CONV_COLS = 2048
HALO = 16


def _conv_kernel(prev_ref, cur_ref, next_ref, w_ref, b_ref, o_ref, *,
                 n_lat_tiles, lat_tiles_per_seq, ctx_tiles_per_seq):
    i = pl.program_id(1)
    in_lat = i < n_lat_tiles
    pos = jnp.where(in_lat, i % lat_tiles_per_seq, (i - n_lat_tiles) % ctx_tiles_per_seq)
    per_seq = jnp.where(in_lat, lat_tiles_per_seq, ctx_tiles_per_seq)
    has_prev = pos > 0
    has_next = pos < per_seq - 1
    t = cur_ref.shape[0]
    n = t + 2 * HALO
    full = jnp.concatenate([jnp.where(has_prev, prev_ref[...].astype(F32), 0.0), cur_ref[...].astype(F32),
                            jnp.where(has_next, next_ref[...].astype(F32), 0.0)], axis=0)
    acc = jnp.broadcast_to(b_ref[...], (t, CONV_COLS))
    for k in range(CONV_K):
        shift = CONV_K // 2 - k
        shifted = full if shift == 0 else pltpu.roll(full, shift % n, 0)
        acc = acc + w_ref[k:k + 1, :] * shifted[HALO:HALO + t, :]
    o_ref[...] = acc * _sigmoid(acc)


def conv_silu(proj, conv_w, conv_b, n_batch, seq, ctx_len):
    rows = proj.shape[0]
    n_lat_tiles = n_batch * seq // ROW_TILE
    col0 = COL_XBC // CONV_COLS
    per = ROW_TILE // HALO
    last_halo = rows // HALO - 1
    kern = functools.partial(_conv_kernel, n_lat_tiles=n_lat_tiles,
                             lat_tiles_per_seq=seq // ROW_TILE, ctx_tiles_per_seq=ctx_len // ROW_TILE)
    w8 = jnp.concatenate([conv_w, jnp.zeros((8 - CONV_K, XBC_W), F32)], axis=0)
    return pl.pallas_call(
        kern,
        out_shape=jax.ShapeDtypeStruct((rows, XBC_W), F32),
        grid=(XBC_W // CONV_COLS, rows // ROW_TILE),
        in_specs=[pl.BlockSpec((HALO, CONV_COLS), lambda j, i: (jnp.maximum(i * per - 1, 0), col0 + j)),
                  pl.BlockSpec((ROW_TILE, CONV_COLS), lambda j, i: (i, col0 + j)),
                  pl.BlockSpec((HALO, CONV_COLS), lambda j, i: (jnp.minimum((i + 1) * per, last_halo), col0 + j)),
                  pl.BlockSpec((8, CONV_COLS), lambda j, i: (0, j)),
                  pl.BlockSpec((1, CONV_COLS), lambda j, i: (0, j))],
        out_specs=pl.BlockSpec((ROW_TILE, CONV_COLS), lambda j, i: (i, j)),
        compiler_params=_params("arbitrary", "arbitrary"),
        name="conv_silu",
    )(proj, proj, proj, w8, conv_b.reshape(1, XBC_W))


def _softplus(x):
    return jnp.maximum(x, 0.0) + jnp.log(1.0 + jnp.exp(-jnp.abs(x)))


def _ssd_kernel(xs_ref, b_ref, c_ref, dt_ref, bias_ref, a_ref, dskip_ref, y_ref, h_ref, *, rev):
    @pl.when(pl.program_id(1) == 0)
    def _():
        h_ref[...] = jnp.zeros_like(h_ref)

    L = SSD_CHUNK
    gw = HEADS_PER_GROUP * SSM_HEAD_DIM
    off = N_SSM_HEADS if rev else 0
    row = lax.broadcasted_iota(jnp.int32, (L, L), 0)
    col = lax.broadcasted_iota(jnp.int32, (L, L), 1)
    causal = (col >= row) if rev else (col <= row)
    first_head = col < SSM_HEAD_DIM
    tmat = jnp.where(causal, 1.0, 0.0).astype(BF16)

    dt = _softplus(dt_ref[...] + bias_ref[...])
    a = dt * (a_ref[...] * LOG2_E)
    acum = _dot_sel_lhs(tmat, a)
    acum_t = acum.T
    dt_t = dt.T
    last = 0 if rev else L - 1
    to_end_t = jnp.exp2(acum_t[:, last:last + 1] - acum_t) * dt_t
    src_t = acum_t - jnp.log2(dt_t)

    for g in range(N_GROUPS):
        bg = b_ref[:, g * D_STATE:(g + 1) * D_STATE]
        cgb = c_ref[:, g * D_STATE:(g + 1) * D_STATE].astype(BF16)
        cb = _dot_nt(cgb, bg.astype(BF16))
        bg_t = bg.T
        y_off = _dot(cgb, h_ref[:, g * gw:(g + 1) * gw].astype(BF16))
        for pair in range(HEADS_PER_GROUP // 2):
            c0 = g * gw + pair * LANES
            x_f = xs_ref[:, c0:c0 + LANES]
            x2 = jnp.concatenate([jnp.where(first_head, x_f, 0.0).astype(BF16),
                                  jnp.where(first_head, 0.0, x_f).astype(BF16)], axis=0)
            ms, ws, bcs = [], [], []
            for k in range(2):
                hd = off + g * HEADS_PER_GROUP + 2 * pair + k
                bc = jnp.broadcast_to(acum[:, hd:hd + 1], (L, L))
                decay_dt = jnp.exp2(jnp.where(causal, bc - src_t[hd:hd + 1, :], -jnp.inf))
                ms.append((cb * decay_dt).astype(BF16))
                ws.append((bg_t * to_end_t[hd:hd + 1, :]).astype(BF16))
                bcs.append(bc)
            lhs = jnp.concatenate([jnp.concatenate(ms, axis=1), jnp.concatenate(ws, axis=1)], axis=0)
            res = _dot(lhs, x2)
            e_t = jnp.exp2(jnp.where(first_head, bcs[0], bcs[1]))
            y = res[:L] + y_off[:, pair * LANES:(pair + 1) * LANES] * e_t
            if not rev:
                y = y + dskip_ref[:, c0:c0 + LANES] * x_f
            y_ref[:, c0:c0 + LANES] = y
            h_ref[:, c0:c0 + LANES] = e_t[last:last + 1, :] * h_ref[:, c0:c0 + LANES] + res[L:]


def ssd_scan(xbc, dt_raw, dt_bias, a_neg, dskip_x, n_batch, seq, ctx_len, rev):
    rows = xbc.shape[0]
    nc, ncc = seq // SSD_CHUNK, ctx_len // SSD_CHUNK
    lat_blocks = n_batch * nc

    def blk(b, j):
        cj = (ncc - 1 - j) if rev else j
        lj = (nc - 1 - (j - ncc)) if rev else (j - ncc)
        return jnp.where(j < ncc, lat_blocks + b * ncc + cj, b * nc + lj)

    return pl.pallas_call(
        functools.partial(_ssd_kernel, rev=rev),
        out_shape=jax.ShapeDtypeStruct((rows, D_INNER), F32),
        grid=(n_batch, ncc + nc),
        in_specs=[pl.BlockSpec((SSD_CHUNK, D_INNER), lambda b, j: (blk(b, j), 0)),
                  pl.BlockSpec((SSD_CHUNK, BC_W), lambda b, j: (blk(b, j), D_INNER // BC_W)),
                  pl.BlockSpec((SSD_CHUNK, BC_W), lambda b, j: (blk(b, j), D_INNER // BC_W + 1)),
                  pl.BlockSpec((SSD_CHUNK, LANES), lambda b, j: (blk(b, j), 0)),
                  pl.BlockSpec((1, LANES), lambda b, j: (0, 0)),
                  pl.BlockSpec((1, LANES), lambda b, j: (0, 0)),
                  pl.BlockSpec((1, D_INNER), lambda b, j: (0, 0))],
        out_specs=pl.BlockSpec((SSD_CHUNK, D_INNER), lambda b, j: (blk(b, j), 0)),
        scratch_shapes=[pltpu.VMEM((D_STATE, D_INNER), F32)],
        compiler_params=_params("arbitrary", "arbitrary"),
        name="ssd_scan_bwd" if rev else "ssd_scan_fwd",
    )(xbc, xbc, xbc, dt_raw, dt_bias, a_neg, dskip_x)


def _rope_tables(seq):
    inv_freq = ROPE_BASE ** (-jnp.arange(ROPE_FREQS, dtype=F32) / ROPE_FREQS)
    n_rows = seq // GRID_W
    r = jnp.repeat(jnp.arange(n_rows, dtype=F32), GRID_W)
    c = jnp.tile(jnp.arange(GRID_W, dtype=F32), n_rows)
    ar = r[:, None] * inv_freq
    ac = c[:, None] * inv_freq
    ang = jnp.concatenate([ar, ar, ac, ac], axis=-1)
    cos, sin = jnp.cos(ang), jnp.sin(ang)
    first_half = (jnp.arange(HEAD_DIM) % (2 * ROPE_FREQS)) < ROPE_FREQS
    sin_up = jnp.where(first_half, -sin, 0.0)
    sin_dn = jnp.where(first_half, 0.0, sin)
    return cos, sin_up, sin_dn


def _qk_prep_kernel(x_ref, w_ref, cos_ref, su_ref, sd_ref, o_ref, *, n_heads, scale, n_lat_tiles):
    rotate = pl.program_id(0) < n_lat_tiles
    cos = jnp.where(rotate, cos_ref[...], 1.0)
    su = jnp.where(rotate, su_ref[...], 0.0)
    sd = jnp.where(rotate, sd_ref[...], 0.0)
    w = w_ref[...]
    for h in range(n_heads):
        hs = slice(h * HEAD_DIM, (h + 1) * HEAD_DIM)
        x = x_ref[:, hs].astype(F32)
        n = x * lax.rsqrt(jnp.mean(x * x, axis=-1, keepdims=True) + NORM_EPS) * w
        y = n * cos + pltpu.roll(n, HEAD_DIM - ROPE_FREQS, 1) * su + pltpu.roll(n, ROPE_FREQS, 1) * sd
        if scale != 1.0:
            y = y * scale
        o_ref[:, hs] = y.astype(o_ref.dtype)


def qk_prep(proj, norm_w, tables, rows, n_lat, seq, col, width, scale, name):
    n_heads = width // HEAD_DIM
    tiles_per_seq = seq // ROW_TILE
    tab_spec = pl.BlockSpec((ROW_TILE, HEAD_DIM), lambda i: (i % tiles_per_seq, 0))
    return pl.pallas_call(
        functools.partial(_qk_prep_kernel, n_heads=n_heads, scale=scale, n_lat_tiles=n_lat // ROW_TILE),
        out_shape=jax.ShapeDtypeStruct((rows, width), BF16),
        grid=(rows // ROW_TILE,),
        in_specs=[pl.BlockSpec((ROW_TILE, width), lambda i: (i, col // width)),
                  pl.BlockSpec((1, HEAD_DIM), lambda i: (0, 0)),
                  tab_spec, tab_spec, tab_spec],
        out_specs=pl.BlockSpec((ROW_TILE, width), lambda i: (i, 0)),
        compiler_params=_params("arbitrary"),
        name=name,
    )(proj, norm_w.reshape(1, HEAD_DIM), *tables)


def _attn_kernel(q_ref, kp_ref, kc_ref, kn_ref, vp_ref, vc_ref, vn_ref, kx_ref, vx_ref, sink_ref, o_ref, *,
                 n_blocks):
    i = pl.program_id(1)
    T = ATTN_BLOCK
    nq = Q_PER_KV * T
    nk = 3 * T + kx_ref.shape[0]
    qi = lax.broadcasted_iota(jnp.int32, (nq, nk), 0) % T
    kj = lax.broadcasted_iota(jnp.int32, (nq, nk), 1)
    lo = jnp.maximum(qi, jnp.where(i > 0, 0, T))
    hi = jnp.minimum(qi + 2 * WINDOW, jnp.where(i < n_blocks - 1, 3 * T - 1, 2 * T - 1))
    valid = ((kj >= lo) & (kj <= hi)) | (kj >= 3 * T)
    for h in range(N_KV_HEADS):
        hs = slice(h * HEAD_DIM, (h + 1) * HEAD_DIM)
        q = jnp.concatenate([q_ref[:, (h * Q_PER_KV + g) * HEAD_DIM:(h * Q_PER_KV + g + 1) * HEAD_DIM]
                             for g in range(Q_PER_KV)], axis=0)
        kb = jnp.concatenate([kp_ref[:, hs], kc_ref[:, hs], kn_ref[:, hs], kx_ref[:, hs]], axis=0)
        vb = jnp.concatenate([vp_ref[:, hs], vc_ref[:, hs], vn_ref[:, hs], vx_ref[:, hs]], axis=0)
        s = jnp.where(valid, _dot_nt(q, kb), -jnp.inf)
        sk = jnp.concatenate([jnp.broadcast_to(sink_ref[0:1, h * Q_PER_KV + g:h * Q_PER_KV + g + 1], (T, 1))
                              for g in range(Q_PER_KV)], axis=0)
        mx = jnp.maximum(jnp.max(s, axis=-1, keepdims=True), sk)
        p = jnp.exp(s - mx)
        denom = jnp.sum(p, axis=-1, keepdims=True) + jnp.exp(sk - mx)
        o = _dot(p.astype(BF16), vb) * (1.0 / denom)
        for g in range(Q_PER_KV):
            c0 = (h * Q_PER_KV + g) * HEAD_DIM
            o_ref[:, c0:c0 + HEAD_DIM] = o[g * T:(g + 1) * T, :].astype(o_ref.dtype)


def windowed_attention(qn, kn, proj, sink, n_batch, seq, ctx_len):
    nb = seq // ATTN_BLOCK
    ctx0 = n_batch * seq // ctx_len
    sink_row = jnp.zeros((1, LANES), F32).at[0, :N_Q_HEADS].set(sink)
    v_col = COL_V // KV_W

    def kv_spec(d, col):
        return pl.BlockSpec((ATTN_BLOCK, KV_W), lambda b, i: (b * nb + jnp.clip(i + d, 0, nb - 1), col))

    def ctx_spec(col):
        return pl.BlockSpec((ctx_len, KV_W), lambda b, i: (ctx0 + b, col))
    return pl.pallas_call(
        functools.partial(_attn_kernel, n_blocks=nb),
        out_shape=jax.ShapeDtypeStruct((n_batch * seq, Q_W), BF16),
        grid=(n_batch, nb),
        in_specs=[pl.BlockSpec((ATTN_BLOCK, Q_W), lambda b, i: (b * nb + i, 0)),
                  kv_spec(-1, 0), kv_spec(0, 0), kv_spec(1, 0),
                  kv_spec(-1, v_col), kv_spec(0, v_col), kv_spec(1, v_col),
                  ctx_spec(0), ctx_spec(v_col),
                  pl.BlockSpec((1, LANES), lambda b, i: (0, 0))],
        out_specs=pl.BlockSpec((ATTN_BLOCK, Q_W), lambda b, i: (b * nb + i, 0)),
        compiler_params=_params("arbitrary", "arbitrary"),
        name="windowed_attention",
    )(qn, kn, kn, kn, proj, proj, proj, kn, proj, sink_row)


def _gate_norm_kernel(yf_ref, yb_ref, z_ref, w_ref, o_ref):
    gw = D_INNER // N_GROUPS
    for g in range(N_GROUPS):
        gs = slice(g * gw, (g + 1) * gw)
        z = z_ref[:, gs].astype(F32)
        s = (yf_ref[:, gs] + yb_ref[:, gs]) * (z * _sigmoid(z))
        n = s * lax.rsqrt(jnp.mean(s * s, axis=-1, keepdims=True) + NORM_EPS) * w_ref[:, gs]
        o_ref[:, gs] = n.astype(o_ref.dtype)


def gate_norm(y_f, y_b, proj, ssm_norm_w, rows):
    spec = pl.BlockSpec((ROW_TILE, D_INNER), lambda i: (i, 0))
    return pl.pallas_call(
        _gate_norm_kernel,
        out_shape=jax.ShapeDtypeStruct((rows, D_INNER), BF16),
        grid=(rows // ROW_TILE,),
        in_specs=[spec, spec, pl.BlockSpec((ROW_TILE, D_INNER), lambda i: (i, COL_Z // D_INNER)),
                  pl.BlockSpec((1, D_INNER), lambda i: (0, 0))],
        out_specs=spec,
        compiler_params=_params("arbitrary"),
        name="gate_norm",
    )(y_f, y_b, proj, ssm_norm_w.reshape(1, D_INNER))


def _merge_kernel(yg_ref, ya_ref, ws_ref, wa_ref, gs_ref, ga_ref, o_ref):
    o = (_sigmoid(gs_ref[...].astype(F32)) * _dot(yg_ref[...], ws_ref[...])
         + _sigmoid(ga_ref[...].astype(F32)) * _dot(ya_ref[...], wa_ref[...]))
    o_ref[...] = o.astype(o_ref.dtype)


def merge_branches(yg, ya, w_ssm_out, w_attn_out, proj, rows):
    tm, tn = 512, 1024
    return pl.pallas_call(
        _merge_kernel,
        out_shape=jax.ShapeDtypeStruct((rows, D_MODEL), BF16),
        grid=(D_MODEL // tn, rows // tm),
        in_specs=[pl.BlockSpec((tm, D_INNER), lambda j, i: (i, 0)),
                  pl.BlockSpec((tm, Q_W), lambda j, i: (i, 0)),
                  pl.BlockSpec((D_INNER, tn), lambda j, i: (0, j)),
                  pl.BlockSpec((Q_W, tn), lambda j, i: (0, j)),
                  pl.BlockSpec((tm, tn), lambda j, i: (i, COL_GS // tn + j)),
                  pl.BlockSpec((tm, tn), lambda j, i: (i, COL_GA // tn + j))],
        out_specs=pl.BlockSpec((tm, tn), lambda j, i: (i, j)),
        compiler_params=_params("arbitrary", "arbitrary"),
        name="merge_branches",
    )(yg, ya, w_ssm_out, w_attn_out, proj, proj)


def _out_proj_kernel(m_ref, wo_ref, x_ref, mod_ref, nw_ref, rw_ref, rb_ref, x1_ref, h2_ref, lg_ref):
    x1 = x_ref[...] + mod_ref[0, 2:3, :] * _dot(m_ref[...], wo_ref[...])
    x1_ref[...] = x1
    n = x1 * lax.rsqrt(jnp.mean(x1 * x1, axis=-1, keepdims=True) + NORM_EPS) * nw_ref[...]
    h2 = n * (1.0 + mod_ref[0, 4:5, :]) + mod_ref[0, 3:4, :]
    h2_ref[...] = _pack_bf16_pairs(h2[:, :PACK_W], h2[:, PACK_W:])
    lg_ref[...] = _dot_f32(h2, rw_ref[...]) + rb_ref[...]


def out_proj_router(merged, w_o, xa, mod, norm2_w, router_w, router_b, n_batch, seq):
    rows = n_batch * seq
    n_tiles = rows // ROW_TILE
    rw = jnp.zeros((D_MODEL, LANES), F32).at[:, :N_EXPERTS].set(router_w)
    rb = jnp.zeros((1, LANES), F32).at[0, :N_EXPERTS].set(router_b)
    tile = pl.BlockSpec((ROW_TILE, D_MODEL), lambda i: (i, 0))
    return pl.pallas_call(
        _out_proj_kernel,
        out_shape=(jax.ShapeDtypeStruct((rows, D_MODEL), F32),
                   jax.ShapeDtypeStruct((rows, PACK_W), U32),
                   jax.ShapeDtypeStruct((rows, LANES), F32)),
        grid=(n_tiles,),
        in_specs=[tile,
                  pl.BlockSpec((D_MODEL, D_MODEL), lambda i: (0, 0)),
                  tile,
                  pl.BlockSpec((1, 8, D_MODEL), _mod_row_map(n_tiles, seq // ROW_TILE, n_batch)),
                  pl.BlockSpec((1, D_MODEL), lambda i: (0, 0)),
                  pl.BlockSpec((D_MODEL, LANES), lambda i: (0, 0)),
                  pl.BlockSpec((1, LANES), lambda i: (0, 0))],
        out_specs=(tile, pl.BlockSpec((ROW_TILE, PACK_W), lambda i: (i, 0)),
                   pl.BlockSpec((ROW_TILE, LANES), lambda i: (i, 0))),
        compiler_params=_params("arbitrary"),
        name="out_proj_router",
    )(merged, w_o, xa, mod, norm2_w.reshape(1, D_MODEL), rw, rb)


def _route_kernel(lg_ref, o_ref, cnt_ref, carry_ref):
    @pl.when(pl.program_id(0) == 0)
    def _():
        carry_ref[...] = jnp.zeros_like(carry_ref)

    t = lg_ref.shape[0]
    lane = lax.broadcasted_iota(jnp.int32, (t, LANES), 1).astype(F32)
    work = jnp.where(lane < N_EXPERTS, lg_ref[...], -jnp.inf)
    vals, idxs = [], []
    for _ in range(TOP_K):
        m = jnp.max(work, axis=-1, keepdims=True)
        idx = jnp.min(jnp.where(work == m, lane, float(LANES)), axis=-1, keepdims=True)
        vals.append(m)
        idxs.append(idx)
        work = jnp.where(lane == idx, -jnp.inf, work)
    es = [jnp.exp(v - vals[0]) for v in vals]
    inv = 1.0 / (es[0] + es[1] + es[2] + es[3])
    onehot = jnp.zeros((t, LANES), F32)
    for idx in idxs:
        onehot = onehot + jnp.where(lane == idx, 1.0, 0.0)
    r = lax.broadcasted_iota(jnp.int32, (t, t), 0)
    c = lax.broadcasted_iota(jnp.int32, (t, t), 1)
    before = jnp.where(c < r, 1.0, 0.0).astype(BF16)
    excl = _dot(before, onehot.astype(BF16)) + carry_ref[...]
    out = jnp.zeros((t, LANES), F32)
    for k in range(TOP_K):
        rank = jnp.sum(jnp.where(lane == idxs[k], excl, 0.0), axis=-1, keepdims=True)
        out = jnp.where(lane == k, idxs[k], out)
        out = jnp.where(lane == TOP_K + k, es[k] * inv, out)
        out = jnp.where(lane == 2 * TOP_K + k, rank, out)
    o_ref[...] = out
    carry_ref[...] = carry_ref[...] + jnp.sum(onehot, axis=0, keepdims=True)
    cnt_ref[...] = carry_ref[...]


def route(logits):
    n_tok = logits.shape[0]
    return pl.pallas_call(
        _route_kernel,
        out_shape=(jax.ShapeDtypeStruct((n_tok, LANES), F32), jax.ShapeDtypeStruct((1, LANES), F32)),
        grid=(n_tok // ROW_TILE,),
        in_specs=[pl.BlockSpec((ROW_TILE, LANES), lambda i: (i, 0))],
        out_specs=(pl.BlockSpec((ROW_TILE, LANES), lambda i: (i, 0)),
                   pl.BlockSpec((1, LANES), lambda i: (0, 0))),
        scratch_shapes=[pltpu.VMEM((1, LANES), F32)],
        compiler_params=_params("arbitrary"),
        name="route",
    )(logits)


def _row_copy(src_ref, dst_ref, sem, src_row, dst_row):
    return pltpu.make_async_copy(src_ref.at[pl.ds(src_row, 1)], dst_ref.at[pl.ds(dst_row, 1)], sem)


DISPATCH_TILE = 128


def _dispatch_kernel(fill_ref, dest_ref, h_ref, o_hbm, zero_ref, fill_sem, sem):
    def fill_copy(blk):
        return pltpu.make_async_copy(zero_ref, o_hbm.at[pl.ds(blk * MOE_BLOCK, MOE_BLOCK)], fill_sem)

    @pl.when(pl.program_id(0) == 0)
    def _():
        zero_ref[...] = jnp.zeros_like(zero_ref)
        for s in range(fill_ref.shape[0]):
            @pl.when(fill_ref[s] >= 0)
            def _():
                fill_copy(fill_ref[s]).start()
        for s in range(fill_ref.shape[0]):
            @pl.when(fill_ref[s] >= 0)
            def _():
                fill_copy(0).wait()

    t = h_ref.shape[0]

    def start(r, carry):
        for k in range(TOP_K):
            _row_copy(h_ref, o_hbm, sem, r, dest_ref[0, 0, r * TOP_K + k]).start()
        return carry
    lax.fori_loop(0, t, start, 0, unroll=4)

    def wait(r, carry):
        for k in range(TOP_K):
            _row_copy(h_ref, o_hbm, sem, r, 0).wait()
        return carry
    lax.fori_loop(0, t, wait, 0, unroll=8)


def dispatch_rows(h2, dest, fill_blocks, cap):
    n_tok = h2.shape[0]
    t = DISPATCH_TILE
    n_tiles = n_tok // t
    grid_spec = pltpu.PrefetchScalarGridSpec(
        num_scalar_prefetch=1,
        grid=(n_tiles,),
        in_specs=[pl.BlockSpec((1, 1, t * TOP_K), lambda i, fb: (i, 0, 0), memory_space=pltpu.SMEM),
                  pl.BlockSpec((t, PACK_W), lambda i, fb: (i, 0))],
        out_specs=pl.BlockSpec(memory_space=pl.ANY),
        scratch_shapes=[pltpu.VMEM((MOE_BLOCK, PACK_W), U32), pltpu.SemaphoreType.DMA,
                        pltpu.SemaphoreType.DMA])
    return pl.pallas_call(
        _dispatch_kernel,
        out_shape=jax.ShapeDtypeStruct((cap, PACK_W), U32),
        grid_spec=grid_spec,
        compiler_params=_params("arbitrary"),
        name="moe_dispatch",
    )(fill_blocks, dest.reshape(n_tiles, 1, t * TOP_K), h2)


EXPERT_COLS = 1024
DMA_CHUNKS = 4


def _expert_rows_pipeline(e, col, blk0_ref, nblk_ref, tail_ref, x_hbm, o_hbm, xbuf, obuf, zbuf,
                          sem_in, sem_out, sem_tail, compute, tail_max):
    n = nblk_ref[e]
    b0 = blk0_ref[e]
    width = obuf.shape[2]

    chunk = MOE_BLOCK // DMA_CHUNKS

    class _Copies:
        def __init__(self, copies):
            self.copies = copies

        def start(self):
            for c in self.copies:
                c.start()

        def wait(self):
            for c in self.copies:
                c.wait()

    def rows(blk, c):
        return pl.ds(pl.multiple_of((b0 + blk) * MOE_BLOCK + c * chunk, chunk), chunk)

    def x_copy(blk, slot):
        return _Copies([pltpu.make_async_copy(x_hbm.at[rows(blk, c)], xbuf.at[slot, pl.ds(c * chunk, chunk)],
                                              sem_in.at[slot]) for c in range(DMA_CHUNKS)])

    def o_copy(blk, slot):
        return _Copies([pltpu.make_async_copy(obuf.at[slot, pl.ds(c * chunk, chunk)],
                                              o_hbm.at[rows(blk, c), pl.ds(col, width)], sem_out.at[slot])
                        for c in range(DMA_CHUNKS)])

    @pl.when(n > 0)
    def _():
        x_copy(0, 0).start()

        def body(blk, carry):
            slot = lax.rem(blk, 2)
            x_copy(blk, slot).wait()

            @pl.when(blk + 1 < n)
            def _():
                x_copy(blk + 1, 1 - slot).start()

            @pl.when(blk >= 2)
            def _():
                o_copy(blk - 2, slot).wait()
            obuf[slot] = compute(xbuf[slot]).astype(obuf.dtype)
            o_copy(blk, slot).start()
            return carry
        lax.fori_loop(0, n, body, 0)

        @pl.when(n >= 2)
        def _():
            o_copy(n - 2, lax.rem(n, 2)).wait()
        o_copy(n - 1, lax.rem(n - 1, 2)).wait()

    @pl.when(e == N_EXPERTS - 1)
    def _():
        zbuf[...] = jnp.zeros_like(zbuf)
        first, count = tail_ref[0], tail_ref[1]

        def z_copy(t):
            dst = pl.ds(pl.multiple_of((first + t) * MOE_BLOCK, MOE_BLOCK), MOE_BLOCK)
            return pltpu.make_async_copy(zbuf, o_hbm.at[dst, pl.ds(col, width)], sem_tail)
        for t in range(tail_max):
            @pl.when(t < count)
            def _():
                z_copy(t).start()
        for t in range(tail_max):
            @pl.when(t < count)
            def _():
                z_copy(t).wait()


def _expert_up_kernel(blk0_ref, nblk_ref, tail_ref, x_hbm, wg_ref, wu_ref, bg_ref, bu_ref, o_hbm,
                      xbuf, obuf, zbuf, wgb_ref, wub_ref, sem_in, sem_out, sem_tail, *, tail_max):
    j, e = pl.program_id(0), pl.program_id(1)

    @pl.when(nblk_ref[e] > 0)
    def _():
        wgb_ref[...] = wg_ref[...].astype(BF16)
        wub_ref[...] = wu_ref[...].astype(BF16)

    def compute(words):
        xb = jnp.concatenate(_unpack_bf16_pairs(words), axis=1).astype(BF16)
        gate = jnp.minimum(_dot(xb, wgb_ref[...]) + bg_ref[0], SWIGLU_LIMIT)
        up = jnp.clip(_dot(xb, wub_ref[...]) + bu_ref[0], -SWIGLU_LIMIT, SWIGLU_LIMIT)
        return gate * _sigmoid(SWIGLU_ALPHA * gate) * (up + 1.0)
    _expert_rows_pipeline(e, pl.multiple_of(j * EXPERT_COLS, EXPERT_COLS), blk0_ref, nblk_ref, tail_ref,
                          x_hbm, o_hbm, xbuf, obuf, zbuf, sem_in, sem_out, sem_tail, compute, tail_max)


def _expert_scratch(in_width, in_dtype, out_width, out_dtype):
    return [pltpu.VMEM((2, MOE_BLOCK, in_width), in_dtype),
            pltpu.VMEM((2, MOE_BLOCK, out_width), out_dtype),
            pltpu.VMEM((MOE_BLOCK, out_width), out_dtype)]


_EXPERT_SEMS = [pltpu.SemaphoreType.DMA((2,)), pltpu.SemaphoreType.DMA((2,)), pltpu.SemaphoreType.DMA]


def expert_up(xs, blk0, nblk, tail, tail_max, w_gate_up, b_gate_up):
    cap = xs.shape[0]
    tf = EXPERT_COLS
    nj = D_FF // tf
    b3 = b_gate_up.reshape(N_EXPERTS, 1, 2 * D_FF)
    grid_spec = pltpu.PrefetchScalarGridSpec(
        num_scalar_prefetch=3,
        grid=(nj, N_EXPERTS),
        in_specs=[pl.BlockSpec(memory_space=pl.ANY),
                  pl.BlockSpec((None, D_MODEL, tf), lambda j, e, *_: (e, 0, j)),
                  pl.BlockSpec((None, D_MODEL, tf), lambda j, e, *_: (e, 0, nj + j)),
                  pl.BlockSpec((None, 1, tf), lambda j, e, *_: (e, 0, j)),
                  pl.BlockSpec((None, 1, tf), lambda j, e, *_: (e, 0, nj + j))],
        out_specs=pl.BlockSpec(memory_space=pl.ANY),
        scratch_shapes=_expert_scratch(PACK_W, U32, EXPERT_COLS, BF16)
        + [pltpu.VMEM((D_MODEL, tf), BF16), pltpu.VMEM((D_MODEL, tf), BF16)] + _EXPERT_SEMS)
    return pl.pallas_call(
        functools.partial(_expert_up_kernel, tail_max=tail_max),
        out_shape=jax.ShapeDtypeStruct((cap, D_FF), BF16),
        grid_spec=grid_spec,
        compiler_params=_params("arbitrary", "arbitrary"),
        name="expert_up",
    )(blk0, nblk, tail, xs, w_gate_up, w_gate_up, b3, b3)


def _expert_down_kernel(blk0_ref, nblk_ref, tail_ref, a_hbm, w_ref, b_ref, o_hbm,
                        xbuf, obuf, zbuf, wb_ref, sem_in, sem_out, sem_tail, *, tail_max):
    j, e = pl.program_id(0), pl.program_id(1)

    @pl.when(nblk_ref[e] > 0)
    def _():
        wb_ref[...] = w_ref[...].astype(BF16)

    def compute(a):
        y = _dot(a, wb_ref[...]) + b_ref[0]
        return _pack_bf16_pairs(y[:, :EXPERT_COLS // 2], y[:, EXPERT_COLS // 2:])
    _expert_rows_pipeline(e, pl.multiple_of(j * (EXPERT_COLS // 2), EXPERT_COLS // 2), blk0_ref, nblk_ref, tail_ref,
                          a_hbm, o_hbm, xbuf, obuf, zbuf, sem_in, sem_out, sem_tail, compute, tail_max)


def expert_down(act, blk0, nblk, tail, tail_max, w_down, b_down):
    cap = act.shape[0]
    tn = EXPERT_COLS
    grid_spec = pltpu.PrefetchScalarGridSpec(
        num_scalar_prefetch=3,
        grid=(D_MODEL // tn, N_EXPERTS),
        in_specs=[pl.BlockSpec(memory_space=pl.ANY),
                  pl.BlockSpec((None, D_FF, tn), lambda j, e, *_: (e, 0, j)),
                  pl.BlockSpec((None, 1, tn), lambda j, e, *_: (e, 0, j))],
        out_specs=pl.BlockSpec(memory_space=pl.ANY),
        scratch_shapes=_expert_scratch(D_FF, BF16, tn // 2, U32) + [pltpu.VMEM((D_FF, tn), BF16)] + _EXPERT_SEMS)
    return pl.pallas_call(
        functools.partial(_expert_down_kernel, tail_max=tail_max),
        out_shape=jax.ShapeDtypeStruct((cap, PACK_W), U32),
        grid_spec=grid_spec,
        compiler_params=_params("arbitrary", "arbitrary"),
        name="expert_down",
    )(blk0, nblk, tail, act, w_down, b_down.reshape(N_EXPERTS, 1, D_MODEL))


COMBINE_TILE = 128


def _combine_kernel(dest_ref, yb_hbm, x1_ref, w_ref, mod_ref, o_ref, buf_ref, sem):
    t = COMBINE_TILE

    def start(r, carry):
        for k in range(TOP_K):
            _row_copy(yb_hbm, buf_ref.at[k], sem, dest_ref[0, 0, r * TOP_K + k], r).start()
        return carry
    lax.fori_loop(0, t, start, 0, unroll=4)

    def wait(r, carry):
        for k in range(TOP_K):
            _row_copy(yb_hbm, buf_ref.at[k], sem, 0, r).wait()
        return carry
    lax.fori_loop(0, t, wait, 0, unroll=8)
    half = EXPERT_COLS // 2
    for j in range(D_MODEL // EXPERT_COLS):
        accs = None
        for k in range(TOP_K):
            w = w_ref[:, TOP_K + k:TOP_K + k + 1]
            parts = [w * p for p in _unpack_bf16_pairs(buf_ref[k, :, j * half:(j + 1) * half])]
            accs = parts if accs is None else [a + p for a, p in zip(accs, parts)]
        for h, acc in enumerate(accs):
            cols = slice(j * EXPERT_COLS + h * half, j * EXPERT_COLS + (h + 1) * half)
            o_ref[:, cols] = x1_ref[:, cols] + mod_ref[0, 5:6, cols] * acc


def combine(yb, dest, x1, route_out, mod, n_batch, seq):
    rows = n_batch * seq
    t = COMBINE_TILE
    n_tiles = rows // t
    return pl.pallas_call(
        _combine_kernel,
        out_shape=jax.ShapeDtypeStruct((rows, D_MODEL), F32),
        grid=(n_tiles,),
        in_specs=[pl.BlockSpec((1, 1, t * TOP_K), lambda i: (i, 0, 0), memory_space=pltpu.SMEM),
                  pl.BlockSpec(memory_space=pl.ANY),
                  pl.BlockSpec((t, D_MODEL), lambda i: (i, 0)),
                  pl.BlockSpec((t, LANES), lambda i: (i, 0)),
                  pl.BlockSpec((1, 8, D_MODEL), _mod_row_map(n_tiles, seq // t, n_batch))],
        out_specs=pl.BlockSpec((t, D_MODEL), lambda i: (i, 0)),
        scratch_shapes=[pltpu.VMEM((TOP_K, t, PACK_W), U32), pltpu.SemaphoreType.DMA],
        compiler_params=_params("arbitrary"),
        name="moe_combine",
    )(dest.reshape(n_tiles, 1, t * TOP_K), yb, x1, route_out, mod)


def moe_layout(route_out, counts):
    n_tok = route_out.shape[0]
    idx = route_out[:, :TOP_K].astype(jnp.int32)
    rank = route_out[:, 2 * TOP_K:3 * TOP_K].astype(jnp.int32)
    cnt = counts[0, :N_EXPERTS].astype(jnp.int32)
    padded = (cnt + MOE_BLOCK - 1) // MOE_BLOCK * MOE_BLOCK
    pad_end = jnp.cumsum(padded)
    pad_start = pad_end - padded
    dest = (pad_start[idx] + rank).reshape(-1)
    n_blocks = -(-(n_tok * TOP_K + N_EXPERTS * (MOE_BLOCK - 1)) // MOE_BLOCK)
    n_used = pad_end[-1] // MOE_BLOCK
    tail_max = n_blocks - (n_tok * TOP_K) // MOE_BLOCK
    tail = jnp.stack([n_used, n_blocks - n_used]).astype(jnp.int32)
    last_blk = jnp.where(cnt > 0, pad_end // MOE_BLOCK - 1, -1)
    tail_blk = n_used + jnp.arange(tail_max, dtype=jnp.int32)
    fill_blocks = jnp.concatenate([last_blk, jnp.where(tail_blk < n_blocks, tail_blk, -1)]).astype(jnp.int32)
    blk0 = (pad_start // MOE_BLOCK).astype(jnp.int32)
    nblk = (padded // MOE_BLOCK).astype(jnp.int32)
    return dest, blk0, nblk, tail, tail_max, fill_blocks, n_blocks * MOE_BLOCK


def _in_proj_weights(w_in):
    sizes = (XBC_W, D_INNER, N_SSM_HEADS, N_SSM_HEADS, Q_W, KV_W, KV_W, D_MODEL, D_MODEL)
    offs = [0]
    for s in sizes:
        offs.append(offs[-1] + s)
    seg = lambda i: w_in[:, offs[i]:offs[i + 1]]
    w_main = jnp.concatenate([seg(1), seg(0), seg(4), seg(5), seg(6), seg(7), seg(8)], axis=1).astype(BF16)
    w_dt = jnp.concatenate([seg(2), seg(3)], axis=1).astype(BF16)
    return w_main, w_dt


def hybrid_layer(x, ctx, c, c_ctx, w_ada, b_ada, norm1_w, norm2_w, w_in, conv_w, conv_b, dt_bias_f, dt_bias_b,
                 a_log_f, a_log_b, d_skip, ssm_norm_w, q_norm_w, k_norm_w, sink, w_ssm_out, w_attn_out, w_o,
                 router_w, router_b, w_gate_up, b_gate_up, w_down, b_down):
    n_batch, seq, _ = x.shape
    ctx_len = ctx.shape[1]
    n_lat = n_batch * seq
    x2 = x.reshape(n_lat, D_MODEL)
    rows = n_lat + n_batch * ctx_len

    cvec = jnp.zeros((8, D_MODEL), F32).at[:n_batch].set(c).at[n_batch].set(c_ctx)
    mod = ada_modulation(cvec, w_ada, b_ada).reshape(8, 6, D_MODEL)
    mod = jnp.concatenate([mod, jnp.zeros((8, 2, D_MODEL), F32)], axis=1)

    hn = norm_modulate(x2, ctx.reshape(n_batch * ctx_len, D_MODEL), norm1_w, mod, n_batch, seq)
    w_main, w_dt = _in_proj_weights(w_in)
    proj, dt_raw = in_proj(hn, w_main, w_dt)

    xbc = conv_silu(proj, conv_w, conv_b, n_batch, seq, ctx_len)
    dt_bias = jnp.concatenate([dt_bias_f, dt_bias_b]).reshape(1, LANES)
    a_neg = -jnp.exp(jnp.concatenate([a_log_f, a_log_b])).reshape(1, LANES)
    dskip_x = jnp.repeat(d_skip, SSM_HEAD_DIM).reshape(1, D_INNER)
    y_f = ssd_scan(xbc, dt_raw, dt_bias, a_neg, dskip_x, n_batch, seq, ctx_len, rev=False)
    y_b = ssd_scan(xbc, dt_raw, dt_bias, a_neg, dskip_x, n_batch, seq, ctx_len, rev=True)

    tables = _rope_tables(seq)
    qn = qk_prep(proj, q_norm_w, tables, n_lat, n_lat, seq, COL_Q, Q_W, ATTN_SCALE, "q_prep")
    kn = qk_prep(proj, k_norm_w, tables, rows, n_lat, seq, COL_K, KV_W, 1.0, "k_prep")
    y_attn = windowed_attention(qn, kn, proj, sink, n_batch, seq, ctx_len)

    yg = gate_norm(y_f, y_b, proj, ssm_norm_w, n_lat)
    merged = merge_branches(yg, y_attn, w_ssm_out.astype(BF16), w_attn_out.astype(BF16), proj, n_lat)
    x1, h2, logits = out_proj_router(merged, w_o.astype(BF16), x2, mod, norm2_w, router_w, router_b,
                                     n_batch, seq)

    route_out, counts = route(logits)
    dest, blk0, nblk, tail, tail_max, fill_blocks, cap = moe_layout(route_out, counts)
    xs = dispatch_rows(h2, dest, fill_blocks, cap)
    act = expert_up(xs, blk0, nblk, tail, tail_max, w_gate_up, b_gate_up)
    yb = expert_down(act, blk0, nblk, tail, tail_max, w_down, b_down)
    out = combine(yb, dest, x1, route_out, mod, n_batch, seq)
    return out.reshape(n_batch, seq, D_MODEL)


def kernel(x, c, ctx, c_ctx, w_ada, b_ada, norm1_w, norm2_w, w_in, conv_w, conv_b, dt_bias_f, dt_bias_b,
           a_log_f, a_log_b, d_skip, ssm_norm_w, q_norm_w, k_norm_w, sink, w_ssm_out, w_attn_out, w_o,
           router_w, router_b, w_gate_up, b_gate_up, w_down, b_down):
    assert w_ada.shape[0] == 1, "single-layer block"
    return hybrid_layer(x, ctx, c, c_ctx, w_ada[0], b_ada[0], norm1_w[0], norm2_w[0], w_in[0], conv_w[0],
                        conv_b[0], dt_bias_f[0], dt_bias_b[0], a_log_f[0], a_log_b[0], d_skip[0],
                        ssm_norm_w[0], q_norm_w[0], k_norm_w[0], sink[0], w_ssm_out[0], w_attn_out[0], w_o[0],
                        router_w[0], router_b[0], w_gate_up[0], b_gate_up[0], w_down[0], b_down[0])
```

```python
import functools
import math

import jax
import jax.numpy as jnp
from jax import lax
from jax.experimental import pallas as pl
from jax.experimental.pallas import tpu as pltpu

F32 = jnp.float32
BF16 = jnp.bfloat16

D_MODEL = 2048
GRID_W = 64
NORM_EPS = 1e-6
D_INNER = 2 * D_MODEL
SSM_HEAD_DIM = 64
N_SSM_HEADS = D_INNER // SSM_HEAD_DIM
N_GROUPS = 8
HEADS_PER_GROUP = N_SSM_HEADS // N_GROUPS
D_STATE = 128
BC_W = N_GROUPS * D_STATE
XBC_W = D_INNER + 2 * BC_W
CONV_K = 5
SSD_CHUNK = 128
HEAD_DIM = 128
N_Q_HEADS = D_MODEL // HEAD_DIM
N_KV_HEADS = 4
Q_PER_KV = N_Q_HEADS // N_KV_HEADS
Q_W = N_Q_HEADS * HEAD_DIM
KV_W = N_KV_HEADS * HEAD_DIM
WINDOW = 128
ATTN_BLOCK = 128
ATTN_SCALE = HEAD_DIM ** -0.5
ROPE_FREQS = HEAD_DIM // 4
ROPE_BASE = 10000.0
N_EXPERTS = 32
TOP_K = 4
D_FF = D_MODEL
SWIGLU_LIMIT = 7.0
SWIGLU_ALPHA = 1.702
MOE_BLOCK = 512

LOG2_E = math.log2(math.e)
LANES = 128
ROW_TILE = 256
MAIN_W = D_INNER + XBC_W + Q_W + 2 * KV_W + 2 * D_MODEL
COL_Z, COL_XBC, COL_Q = 0, D_INNER, D_INNER + XBC_W
COL_K, COL_V = COL_Q + Q_W, COL_Q + Q_W + KV_W
COL_GS, COL_GA = COL_V + KV_W, COL_V + KV_W + D_MODEL
VMEM_LIMIT = 56 * 1024 * 1024


def _params(*sem, vmem_limit=VMEM_LIMIT):
    return pltpu.CompilerParams(dimension_semantics=sem, vmem_limit_bytes=vmem_limit)


def _dot(a, b):
    return jnp.dot(a, b, preferred_element_type=F32)


def _dot_nt(a, b):
    return lax.dot_general(a, b, (((1,), (1,)), ((), ())), preferred_element_type=F32)


def _split3(a):
    a1 = a.astype(BF16)
    r = a - a1.astype(F32)
    a2 = r.astype(BF16)
    a3 = (r - a2.astype(F32)).astype(BF16)
    return a1, a2, a3


def _dot_f32(a, b):
    a1, a2, _ = _split3(a)
    b1, b2, _ = _split3(b)
    return _dot(a1, b1) + (_dot(a1, b2) + _dot(a2, b1))


def _dot_sel_rhs(a, sel):
    a1, a2, a3 = _split3(a)
    return _dot(a1, sel) + (_dot(a2, sel) + _dot(a3, sel))


def _dot_sel_lhs(sel, a):
    a1, a2, a3 = _split3(a)
    return _dot(sel, a1) + (_dot(sel, a2) + _dot(sel, a3))


def _sigmoid(x):
    return 1.0 / (1.0 + jnp.exp(-x))


U32 = jnp.uint32
PACK_W = D_MODEL // 2


def _pack_bf16_pairs(lo, hi):
    lo_bits = lax.bitcast_convert_type(lo.astype(BF16).astype(F32), U32)
    hi_bits = lax.bitcast_convert_type(hi.astype(BF16).astype(F32), U32)
    return (lo_bits >> 16) | (hi_bits & U32(0xFFFF0000))


def _unpack_bf16_pairs(words):
    lo = lax.bitcast_convert_type(words << 16, F32)
    hi = lax.bitcast_convert_type(words & U32(0xFFFF0000), F32)
    return lo, hi


def _ada_kernel(c_ref, w_ref, b_ref, o_ref):
    c = c_ref[...]
    o_ref[...] = _dot_f32(c * _sigmoid(c), w_ref[...]) + b_ref[...]


def ada_modulation(cvec, w_ada, b_ada):
    n = w_ada.shape[1]
    tn = 1024
    return pl.pallas_call(
        _ada_kernel,
        out_shape=jax.ShapeDtypeStruct((8, n), F32),
        grid=(n // tn,),
        in_specs=[pl.BlockSpec((8, D_MODEL), lambda j: (0, 0)),
                  pl.BlockSpec((D_MODEL, tn), lambda j: (0, j)),
                  pl.BlockSpec((1, tn), lambda j: (0, j))],
        out_specs=pl.BlockSpec((8, tn), lambda j: (0, j)),
        compiler_params=_params("arbitrary"),
        name="ada_modulation",
    )(cvec, w_ada, b_ada.reshape(1, n))


def _norm_mod_kernel(x_ref, c_ref, w_ref, mod_ref, o_ref, *, n_lat_tiles):
    def emit(x):
        y = x * lax.rsqrt(jnp.mean(x * x, axis=-1, keepdims=True) + NORM_EPS) * w_ref[...]
        o_ref[...] = (y * (1.0 + mod_ref[0, 1:2, :]) + mod_ref[0, 0:1, :]).astype(o_ref.dtype)

    @pl.when(pl.program_id(0) < n_lat_tiles)
    def _():
        emit(x_ref[...])

    @pl.when(pl.program_id(0) >= n_lat_tiles)
    def _():
        emit(c_ref[...])


def _mod_row_map(n_lat_tiles, tiles_per_batch, n_batch):
    def index_map(i, *_):
        return (jnp.where(i < n_lat_tiles, i // tiles_per_batch, n_batch), 0, 0)
    return index_map


def norm_modulate(x2, ctx2, norm_w, mod, n_batch, seq):
    rows = x2.shape[0] + ctx2.shape[0]
    n_lat_tiles = n_batch * seq // ROW_TILE
    return pl.pallas_call(
        functools.partial(_norm_mod_kernel, n_lat_tiles=n_lat_tiles),
        out_shape=jax.ShapeDtypeStruct((rows, D_MODEL), BF16),
        grid=(rows // ROW_TILE,),
        in_specs=[pl.BlockSpec((ROW_TILE, D_MODEL), lambda i: (jnp.minimum(i, n_lat_tiles - 1), 0)),
                  pl.BlockSpec((ROW_TILE, D_MODEL), lambda i: (jnp.maximum(i - n_lat_tiles, 0), 0)),
                  pl.BlockSpec((1, D_MODEL), lambda i: (0, 0)),
                  pl.BlockSpec((1, 8, D_MODEL), _mod_row_map(n_lat_tiles, seq // ROW_TILE, n_batch))],
        out_specs=pl.BlockSpec((ROW_TILE, D_MODEL), lambda i: (i, 0)),
        compiler_params=_params("arbitrary"),
        name="norm_modulate",
    )(x2, ctx2, norm_w.reshape(1, D_MODEL), mod)


def _largest_tile(n, cap, multiple):
    return max(d for d in range(multiple, cap + 1, multiple) if n % d == 0)


def _in_proj_kernel(a_ref, b_ref, bdt_ref, o_ref, odt_ref):
    a = a_ref[...]
    o_ref[...] = _dot(a, b_ref[...]).astype(o_ref.dtype)

    @pl.when(pl.program_id(1) == 0)
    def _():
        odt_ref[...] = _dot(a, bdt_ref[...])


def in_proj(hn, w_main, w_dt):
    m, k = hn.shape
    n = w_main.shape[1]
    tm = _largest_tile(m, 2112, 16)
    tn = 1024
    return pl.pallas_call(
        _in_proj_kernel,
        out_shape=(jax.ShapeDtypeStruct((m, n), BF16), jax.ShapeDtypeStruct((m, LANES), F32)),
        grid=(m // tm, n // tn),
        in_specs=[pl.BlockSpec((tm, k), lambda i, j: (i, 0)),
                  pl.BlockSpec((k, tn), lambda i, j: (0, j)),
                  pl.BlockSpec((k, LANES), lambda i, j: (0, 0))],
        out_specs=(pl.BlockSpec((tm, tn), lambda i, j: (i, j)),
                   pl.BlockSpec((tm, LANES), lambda i, j: (i, 0))),
        compiler_params=_params("arbitrary", "arbitrary"),
        name="in_proj",
    )(hn, w_main, w_dt)


CONV_COLS = 2048
HALO = 16


def _conv_kernel(prev_ref, cur_ref, next_ref, w_ref, b_ref, o_ref, *,
                 n_lat_tiles, lat_tiles_per_seq, ctx_tiles_per_seq):
    i = pl.program_id(1)
    in_lat = i < n_lat_tiles
    pos = jnp.where(in_lat, i % lat_tiles_per_seq, (i - n_lat_tiles) % ctx_tiles_per_seq)
    per_seq = jnp.where(in_lat, lat_tiles_per_seq, ctx_tiles_per_seq)
    has_prev = pos > 0
    has_next = pos < per_seq - 1
    t = cur_ref.shape[0]
    n = t + 2 * HALO
    full = jnp.concatenate([jnp.where(has_prev, prev_ref[...].astype(F32), 0.0), cur_ref[...].astype(F32),
                            jnp.where(has_next, next_ref[...].astype(F32), 0.0)], axis=0)
    acc = jnp.broadcast_to(b_ref[...], (t, CONV_COLS))
    for k in range(CONV_K):
        shift = CONV_K // 2 - k
        shifted = full if shift == 0 else pltpu.roll(full, shift % n, 0)
        acc = acc + w_ref[k:k + 1, :] * shifted[HALO:HALO + t, :]
    o_ref[...] = acc * _sigmoid(acc)


def conv_silu(proj, conv_w, conv_b, n_batch, seq, ctx_len):
    rows = proj.shape[0]
    n_lat_tiles = n_batch * seq // ROW_TILE
    col0 = COL_XBC // CONV_COLS
    per = ROW_TILE // HALO
    last_halo = rows // HALO - 1
    kern = functools.partial(_conv_kernel, n_lat_tiles=n_lat_tiles,
                             lat_tiles_per_seq=seq // ROW_TILE, ctx_tiles_per_seq=ctx_len // ROW_TILE)
    w8 = jnp.concatenate([conv_w, jnp.zeros((8 - CONV_K, XBC_W), F32)], axis=0)
    return pl.pallas_call(
        kern,
        out_shape=jax.ShapeDtypeStruct((rows, XBC_W), F32),
        grid=(XBC_W // CONV_COLS, rows // ROW_TILE),
        in_specs=[pl.BlockSpec((HALO, CONV_COLS), lambda j, i: (jnp.maximum(i * per - 1, 0), col0 + j)),
                  pl.BlockSpec((ROW_TILE, CONV_COLS), lambda j, i: (i, col0 + j)),
                  pl.BlockSpec((HALO, CONV_COLS), lambda j, i: (jnp.minimum((i + 1) * per, last_halo), col0 + j)),
                  pl.BlockSpec((8, CONV_COLS), lambda j, i: (0, j)),
                  pl.BlockSpec((1, CONV_COLS), lambda j, i: (0, j))],
        out_specs=pl.BlockSpec((ROW_TILE, CONV_COLS), lambda j, i: (i, j)),
        compiler_params=_params("arbitrary", "arbitrary"),
        name="conv_silu",
    )(proj, proj, proj, w8, conv_b.reshape(1, XBC_W))


def _softplus(x):
    return jnp.maximum(x, 0.0) + jnp.log(1.0 + jnp.exp(-jnp.abs(x)))


def _ssd_kernel(xs_ref, b_ref, c_ref, dt_ref, bias_ref, a_ref, dskip_ref, y_ref, h_ref, *, rev):
    @pl.when(pl.program_id(1) == 0)
    def _():
        h_ref[...] = jnp.zeros_like(h_ref)

    L = SSD_CHUNK
    gw = HEADS_PER_GROUP * SSM_HEAD_DIM
    off = N_SSM_HEADS if rev else 0
    row = lax.broadcasted_iota(jnp.int32, (L, L), 0)
    col = lax.broadcasted_iota(jnp.int32, (L, L), 1)
    causal = (col >= row) if rev else (col <= row)
    first_head = col < SSM_HEAD_DIM
    tmat = jnp.where(causal, 1.0, 0.0).astype(BF16)

    dt = _softplus(dt_ref[...] + bias_ref[...])
    a = dt * (a_ref[...] * LOG2_E)
    acum = _dot_sel_lhs(tmat, a)
    acum_t = acum.T
    dt_t = dt.T
    last = 0 if rev else L - 1
    to_end_t = jnp.exp2(acum_t[:, last:last + 1] - acum_t) * dt_t
    src_t = acum_t - jnp.log2(dt_t)

    for g in range(N_GROUPS):
        bg = b_ref[:, g * D_STATE:(g + 1) * D_STATE]
        cgb = c_ref[:, g * D_STATE:(g + 1) * D_STATE].astype(BF16)
        cb = _dot_nt(cgb, bg.astype(BF16))
        bg_t = bg.T
        y_off = _dot(cgb, h_ref[:, g * gw:(g + 1) * gw].astype(BF16))
        for pair in range(HEADS_PER_GROUP // 2):
            c0 = g * gw + pair * LANES
            x_f = xs_ref[:, c0:c0 + LANES]
            x2 = jnp.concatenate([jnp.where(first_head, x_f, 0.0).astype(BF16),
                                  jnp.where(first_head, 0.0, x_f).astype(BF16)], axis=0)
            ms, ws, bcs = [], [], []
            for k in range(2):
                hd = off + g * HEADS_PER_GROUP + 2 * pair + k
                bc = jnp.broadcast_to(acum[:, hd:hd + 1], (L, L))
                decay_dt = jnp.exp2(jnp.where(causal, bc - src_t[hd:hd + 1, :], -jnp.inf))
                ms.append((cb * decay_dt).astype(BF16))
                ws.append((bg_t * to_end_t[hd:hd + 1, :]).astype(BF16))
                bcs.append(bc)
            lhs = jnp.concatenate([jnp.concatenate(ms, axis=1), jnp.concatenate(ws, axis=1)], axis=0)
            res = _dot(lhs, x2)
            e_t = jnp.exp2(jnp.where(first_head, bcs[0], bcs[1]))
            y = res[:L] + y_off[:, pair * LANES:(pair + 1) * LANES] * e_t
            if not rev:
                y = y + dskip_ref[:, c0:c0 + LANES] * x_f
            y_ref[:, c0:c0 + LANES] = y
            h_ref[:, c0:c0 + LANES] = e_t[last:last + 1, :] * h_ref[:, c0:c0 + LANES] + res[L:]


def ssd_scan(xbc, dt_raw, dt_bias, a_neg, dskip_x, n_batch, seq, ctx_len, rev):
    rows = xbc.shape[0]
    nc, ncc = seq // SSD_CHUNK, ctx_len // SSD_CHUNK
    lat_blocks = n_batch * nc

    def blk(b, j):
        cj = (ncc - 1 - j) if rev else j
        lj = (nc - 1 - (j - ncc)) if rev else (j - ncc)
        return jnp.where(j < ncc, lat_blocks + b * ncc + cj, b * nc + lj)

    return pl.pallas_call(
        functools.partial(_ssd_kernel, rev=rev),
        out_shape=jax.ShapeDtypeStruct((rows, D_INNER), F32),
        grid=(n_batch, ncc + nc),
        in_specs=[pl.BlockSpec((SSD_CHUNK, D_INNER), lambda b, j: (blk(b, j), 0)),
                  pl.BlockSpec((SSD_CHUNK, BC_W), lambda b, j: (blk(b, j), D_INNER // BC_W)),
                  pl.BlockSpec((SSD_CHUNK, BC_W), lambda b, j: (blk(b, j), D_INNER // BC_W + 1)),
                  pl.BlockSpec((SSD_CHUNK, LANES), lambda b, j: (blk(b, j), 0)),
                  pl.BlockSpec((1, LANES), lambda b, j: (0, 0)),
                  pl.BlockSpec((1, LANES), lambda b, j: (0, 0)),
                  pl.BlockSpec((1, D_INNER), lambda b, j: (0, 0))],
        out_specs=pl.BlockSpec((SSD_CHUNK, D_INNER), lambda b, j: (blk(b, j), 0)),
        scratch_shapes=[pltpu.VMEM((D_STATE, D_INNER), F32)],
        compiler_params=_params("arbitrary", "arbitrary"),
        name="ssd_scan_bwd" if rev else "ssd_scan_fwd",
    )(xbc, xbc, xbc, dt_raw, dt_bias, a_neg, dskip_x)


def _rope_tables(seq):
    inv_freq = ROPE_BASE ** (-jnp.arange(ROPE_FREQS, dtype=F32) / ROPE_FREQS)
    n_rows = seq // GRID_W
    r = jnp.repeat(jnp.arange(n_rows, dtype=F32), GRID_W)
    c = jnp.tile(jnp.arange(GRID_W, dtype=F32), n_rows)
    ar = r[:, None] * inv_freq
    ac = c[:, None] * inv_freq
    ang = jnp.concatenate([ar, ar, ac, ac], axis=-1)
    cos, sin = jnp.cos(ang), jnp.sin(ang)
    first_half = (jnp.arange(HEAD_DIM) % (2 * ROPE_FREQS)) < ROPE_FREQS
    sin_up = jnp.where(first_half, -sin, 0.0)
    sin_dn = jnp.where(first_half, 0.0, sin)
    return cos, sin_up, sin_dn


def _qk_prep_kernel(x_ref, w_ref, cos_ref, su_ref, sd_ref, o_ref, *, n_heads, scale, n_lat_tiles):
    rotate = pl.program_id(0) < n_lat_tiles
    cos = jnp.where(rotate, cos_ref[...], 1.0)
    su = jnp.where(rotate, su_ref[...], 0.0)
    sd = jnp.where(rotate, sd_ref[...], 0.0)
    w = w_ref[...]
    for h in range(n_heads):
        hs = slice(h * HEAD_DIM, (h + 1) * HEAD_DIM)
        x = x_ref[:, hs].astype(F32)
        n = x * lax.rsqrt(jnp.mean(x * x, axis=-1, keepdims=True) + NORM_EPS) * w
        y = n * cos + pltpu.roll(n, HEAD_DIM - ROPE_FREQS, 1) * su + pltpu.roll(n, ROPE_FREQS, 1) * sd
        if scale != 1.0:
            y = y * scale
        o_ref[:, hs] = y.astype(o_ref.dtype)


def qk_prep(proj, norm_w, tables, rows, n_lat, seq, col, width, scale, name):
    n_heads = width // HEAD_DIM
    tiles_per_seq = seq // ROW_TILE
    tab_spec = pl.BlockSpec((ROW_TILE, HEAD_DIM), lambda i: (i % tiles_per_seq, 0))
    return pl.pallas_call(
        functools.partial(_qk_prep_kernel, n_heads=n_heads, scale=scale, n_lat_tiles=n_lat // ROW_TILE),
        out_shape=jax.ShapeDtypeStruct((rows, width), BF16),
        grid=(rows // ROW_TILE,),
        in_specs=[pl.BlockSpec((ROW_TILE, width), lambda i: (i, col // width)),
                  pl.BlockSpec((1, HEAD_DIM), lambda i: (0, 0)),
                  tab_spec, tab_spec, tab_spec],
        out_specs=pl.BlockSpec((ROW_TILE, width), lambda i: (i, 0)),
        compiler_params=_params("arbitrary"),
        name=name,
    )(proj, norm_w.reshape(1, HEAD_DIM), *tables)


def _attn_kernel(q_ref, kp_ref, kc_ref, kn_ref, vp_ref, vc_ref, vn_ref, kx_ref, vx_ref, sink_ref, o_ref, *,
                 n_blocks):
    i = pl.program_id(1)
    T = ATTN_BLOCK
    nq = Q_PER_KV * T
    nk = 3 * T + kx_ref.shape[0]
    qi = lax.broadcasted_iota(jnp.int32, (nq, nk), 0) % T
    kj = lax.broadcasted_iota(jnp.int32, (nq, nk), 1)
    lo = jnp.maximum(qi, jnp.where(i > 0, 0, T))
    hi = jnp.minimum(qi + 2 * WINDOW, jnp.where(i < n_blocks - 1, 3 * T - 1, 2 * T - 1))
    valid = ((kj >= lo) & (kj <= hi)) | (kj >= 3 * T)
    for h in range(N_KV_HEADS):
        hs = slice(h * HEAD_DIM, (h + 1) * HEAD_DIM)
        q = jnp.concatenate([q_ref[:, (h * Q_PER_KV + g) * HEAD_DIM:(h * Q_PER_KV + g + 1) * HEAD_DIM]
                             for g in range(Q_PER_KV)], axis=0)
        kb = jnp.concatenate([kp_ref[:, hs], kc_ref[:, hs], kn_ref[:, hs], kx_ref[:, hs]], axis=0)
        vb = jnp.concatenate([vp_ref[:, hs], vc_ref[:, hs], vn_ref[:, hs], vx_ref[:, hs]], axis=0)
        s = jnp.where(valid, _dot_nt(q, kb), -jnp.inf)
        sk = jnp.concatenate([jnp.broadcast_to(sink_ref[0:1, h * Q_PER_KV + g:h * Q_PER_KV + g + 1], (T, 1))
                              for g in range(Q_PER_KV)], axis=0)
        mx = jnp.maximum(jnp.max(s, axis=-1, keepdims=True), sk)
        p = jnp.exp(s - mx)
        denom = jnp.sum(p, axis=-1, keepdims=True) + jnp.exp(sk - mx)
        o = _dot(p.astype(BF16), vb) * (1.0 / denom)
        for g in range(Q_PER_KV):
            c0 = (h * Q_PER_KV + g) * HEAD_DIM
            o_ref[:, c0:c0 + HEAD_DIM] = o[g * T:(g + 1) * T, :].astype(o_ref.dtype)


def windowed_attention(qn, kn, proj, sink, n_batch, seq, ctx_len):
    nb = seq // ATTN_BLOCK
    ctx0 = n_batch * seq // ctx_len
    sink_row = jnp.zeros((1, LANES), F32).at[0, :N_Q_HEADS].set(sink)
    v_col = COL_V // KV_W

    def kv_spec(d, col):
        return pl.BlockSpec((ATTN_BLOCK, KV_W), lambda b, i: (b * nb + jnp.clip(i + d, 0, nb - 1), col))

    def ctx_spec(col):
        return pl.BlockSpec((ctx_len, KV_W), lambda b, i: (ctx0 + b, col))
    return pl.pallas_call(
        functools.partial(_attn_kernel, n_blocks=nb),
        out_shape=jax.ShapeDtypeStruct((n_batch * seq, Q_W), BF16),
        grid=(n_batch, nb),
        in_specs=[pl.BlockSpec((ATTN_BLOCK, Q_W), lambda b, i: (b * nb + i, 0)),
                  kv_spec(-1, 0), kv_spec(0, 0), kv_spec(1, 0),
                  kv_spec(-1, v_col), kv_spec(0, v_col), kv_spec(1, v_col),
                  ctx_spec(0), ctx_spec(v_col),
                  pl.BlockSpec((1, LANES), lambda b, i: (0, 0))],
        out_specs=pl.BlockSpec((ATTN_BLOCK, Q_W), lambda b, i: (b * nb + i, 0)),
        compiler_params=_params("arbitrary", "arbitrary"),
        name="windowed_attention",
    )(qn, kn, kn, kn, proj, proj, proj, kn, proj, sink_row)


def _gate_norm_kernel(yf_ref, yb_ref, z_ref, w_ref, o_ref):
    gw = D_INNER // N_GROUPS
    for g in range(N_GROUPS):
        gs = slice(g * gw, (g + 1) * gw)
        z = z_ref[:, gs].astype(F32)
        s = (yf_ref[:, gs] + yb_ref[:, gs]) * (z * _sigmoid(z))
        n = s * lax.rsqrt(jnp.mean(s * s, axis=-1, keepdims=True) + NORM_EPS) * w_ref[:, gs]
        o_ref[:, gs] = n.astype(o_ref.dtype)


def gate_norm(y_f, y_b, proj, ssm_norm_w, rows):
    spec = pl.BlockSpec((ROW_TILE, D_INNER), lambda i: (i, 0))
    return pl.pallas_call(
        _gate_norm_kernel,
        out_shape=jax.ShapeDtypeStruct((rows, D_INNER), BF16),
        grid=(rows // ROW_TILE,),
        in_specs=[spec, spec, pl.BlockSpec((ROW_TILE, D_INNER), lambda i: (i, COL_Z // D_INNER)),
                  pl.BlockSpec((1, D_INNER), lambda i: (0, 0))],
        out_specs=spec,
        compiler_params=_params("arbitrary"),
        name="gate_norm",
    )(y_f, y_b, proj, ssm_norm_w.reshape(1, D_INNER))


def _merge_kernel(yg_ref, ya_ref, ws_ref, wa_ref, gs_ref, ga_ref, o_ref):
    o = (_sigmoid(gs_ref[...].astype(F32)) * _dot(yg_ref[...], ws_ref[...])
         + _sigmoid(ga_ref[...].astype(F32)) * _dot(ya_ref[...], wa_ref[...]))
    o_ref[...] = o.astype(o_ref.dtype)


def merge_branches(yg, ya, w_ssm_out, w_attn_out, proj, rows):
    tm = _largest_tile(rows, 1024, 16)
    tn = 512
    return pl.pallas_call(
        _merge_kernel,
        out_shape=jax.ShapeDtypeStruct((rows, D_MODEL), BF16),
        grid=(rows // tm, D_MODEL // tn),
        in_specs=[pl.BlockSpec((tm, D_INNER), lambda i, j: (i, 0)),
                  pl.BlockSpec((tm, Q_W), lambda i, j: (i, 0)),
                  pl.BlockSpec((D_INNER, tn), lambda i, j: (0, j)),
                  pl.BlockSpec((Q_W, tn), lambda i, j: (0, j)),
                  pl.BlockSpec((tm, tn), lambda i, j: (i, COL_GS // tn + j)),
                  pl.BlockSpec((tm, tn), lambda i, j: (i, COL_GA // tn + j))],
        out_specs=pl.BlockSpec((tm, tn), lambda i, j: (i, j)),
        compiler_params=_params("arbitrary", "arbitrary"),
        name="merge_branches",
    )(yg, ya, w_ssm_out, w_attn_out, proj, proj)


def _out_proj_kernel(m_ref, wo_ref, x_ref, mod_ref, nw_ref, rw_ref, rb_ref, x1_ref, h2_ref, lg_ref):
    x1 = x_ref[...] + mod_ref[0, 2:3, :] * _dot(m_ref[...], wo_ref[...])
    x1_ref[...] = x1
    n = x1 * lax.rsqrt(jnp.mean(x1 * x1, axis=-1, keepdims=True) + NORM_EPS) * nw_ref[...]
    h2 = n * (1.0 + mod_ref[0, 4:5, :]) + mod_ref[0, 3:4, :]
    h2_ref[...] = _pack_bf16_pairs(h2[:, :PACK_W], h2[:, PACK_W:])
    lg_ref[...] = _dot_f32(h2, rw_ref[...]) + rb_ref[...]


def out_proj_router(merged, w_o, xa, mod, norm2_w, router_w, router_b, n_batch, seq):
    rows = n_batch * seq
    tm = 2 * ROW_TILE
    n_tiles = rows // tm
    rw = jnp.zeros((D_MODEL, LANES), F32).at[:, :N_EXPERTS].set(router_w)
    rb = jnp.zeros((1, LANES), F32).at[0, :N_EXPERTS].set(router_b)
    tile = pl.BlockSpec((tm, D_MODEL), lambda i: (i, 0))
    return pl.pallas_call(
        _out_proj_kernel,
        out_shape=(jax.ShapeDtypeStruct((rows, D_MODEL), F32),
                   jax.ShapeDtypeStruct((rows, PACK_W), U32),
                   jax.ShapeDtypeStruct((rows, LANES), F32)),
        grid=(n_tiles,),
        in_specs=[tile,
                  pl.BlockSpec((D_MODEL, D_MODEL), lambda i: (0, 0)),
                  tile,
                  pl.BlockSpec((1, 8, D_MODEL), _mod_row_map(n_tiles, seq // tm, n_batch)),
                  pl.BlockSpec((1, D_MODEL), lambda i: (0, 0)),
                  pl.BlockSpec((D_MODEL, LANES), lambda i: (0, 0)),
                  pl.BlockSpec((1, LANES), lambda i: (0, 0))],
        out_specs=(tile, pl.BlockSpec((tm, PACK_W), lambda i: (i, 0)),
                   pl.BlockSpec((tm, LANES), lambda i: (i, 0))),
        compiler_params=_params("arbitrary"),
        name="out_proj_router",
    )(merged, w_o, xa, mod, norm2_w.reshape(1, D_MODEL), rw, rb)


def _route_kernel(lg_ref, o_ref, cnt_ref, carry_ref):
    @pl.when(pl.program_id(0) == 0)
    def _():
        carry_ref[...] = jnp.zeros_like(carry_ref)

    t = lg_ref.shape[0]
    lane = lax.broadcasted_iota(jnp.int32, (t, LANES), 1).astype(F32)
    work = jnp.where(lane < N_EXPERTS, lg_ref[...], -jnp.inf)
    vals, idxs = [], []
    for _ in range(TOP_K):
        m = jnp.max(work, axis=-1, keepdims=True)
        idx = jnp.min(jnp.where(work == m, lane, float(LANES)), axis=-1, keepdims=True)
        vals.append(m)
        idxs.append(idx)
        work = jnp.where(lane == idx, -jnp.inf, work)
    es = [jnp.exp(v - vals[0]) for v in vals]
    inv = 1.0 / (es[0] + es[1] + es[2] + es[3])
    onehot = jnp.zeros((t, LANES), F32)
    for idx in idxs:
        onehot = onehot + jnp.where(lane == idx, 1.0, 0.0)
    r = lax.broadcasted_iota(jnp.int32, (t, t), 0)
    c = lax.broadcasted_iota(jnp.int32, (t, t), 1)
    before = jnp.where(c < r, 1.0, 0.0).astype(BF16)
    excl = _dot(before, onehot.astype(BF16)) + carry_ref[...]
    out = jnp.zeros((t, LANES), F32)
    for k in range(TOP_K):
        rank = jnp.sum(jnp.where(lane == idxs[k], excl, 0.0), axis=-1, keepdims=True)
        out = jnp.where(lane == k, idxs[k], out)
        out = jnp.where(lane == TOP_K + k, es[k] * inv, out)
        out = jnp.where(lane == 2 * TOP_K + k, rank, out)
    o_ref[...] = out
    carry_ref[...] = carry_ref[...] + jnp.sum(onehot, axis=0, keepdims=True)
    cnt_ref[...] = carry_ref[...]


def route(logits):
    n_tok = logits.shape[0]
    return pl.pallas_call(
        _route_kernel,
        out_shape=(jax.ShapeDtypeStruct((n_tok, LANES), F32), jax.ShapeDtypeStruct((1, LANES), F32)),
        grid=(n_tok // ROW_TILE,),
        in_specs=[pl.BlockSpec((ROW_TILE, LANES), lambda i: (i, 0))],
        out_specs=(pl.BlockSpec((ROW_TILE, LANES), lambda i: (i, 0)),
                   pl.BlockSpec((1, LANES), lambda i: (0, 0))),
        scratch_shapes=[pltpu.VMEM((1, LANES), F32)],
        compiler_params=_params("arbitrary"),
        name="route",
    )(logits)


def _row_copy(src_ref, dst_ref, sem, src_row, dst_row):
    return pltpu.make_async_copy(src_ref.at[pl.ds(src_row, 1)], dst_ref.at[pl.ds(dst_row, 1)], sem)


DISPATCH_TILE = 128


def _dispatch_kernel(fill_ref, dest_ref, h_ref, o_hbm, zero_ref, fill_sem, sem):
    def fill_copy(blk):
        return pltpu.make_async_copy(zero_ref, o_hbm.at[pl.ds(blk * MOE_BLOCK, MOE_BLOCK)], fill_sem)

    @pl.when(pl.program_id(0) == 0)
    def _():
        zero_ref[...] = jnp.zeros_like(zero_ref)
        for s in range(fill_ref.shape[0]):
            @pl.when(fill_ref[s] >= 0)
            def _():
                fill_copy(fill_ref[s]).start()
        for s in range(fill_ref.shape[0]):
            @pl.when(fill_ref[s] >= 0)
            def _():
                fill_copy(0).wait()

    t = h_ref.shape[0]

    def start(r, carry):
        for k in range(TOP_K):
            _row_copy(h_ref, o_hbm, sem, r, dest_ref[0, 0, r * TOP_K + k]).start()
        return carry
    lax.fori_loop(0, t, start, 0, unroll=4)

    def wait(r, carry):
        for k in range(TOP_K):
            _row_copy(h_ref, o_hbm, sem, r, 0).wait()
        return carry
    lax.fori_loop(0, t, wait, 0, unroll=8)


def dispatch_rows(h2, dest, fill_blocks, cap):
    n_tok = h2.shape[0]
    t = DISPATCH_TILE
    n_tiles = n_tok // t
    grid_spec = pltpu.PrefetchScalarGridSpec(
        num_scalar_prefetch=1,
        grid=(n_tiles,),
        in_specs=[pl.BlockSpec((1, 1, t * TOP_K), lambda i, fb: (i, 0, 0), memory_space=pltpu.SMEM),
                  pl.BlockSpec((t, PACK_W), lambda i, fb: (i, 0))],
        out_specs=pl.BlockSpec(memory_space=pl.ANY),
        scratch_shapes=[pltpu.VMEM((MOE_BLOCK, PACK_W), U32), pltpu.SemaphoreType.DMA,
                        pltpu.SemaphoreType.DMA])
    return pl.pallas_call(
        _dispatch_kernel,
        out_shape=jax.ShapeDtypeStruct((cap, PACK_W), U32),
        grid_spec=grid_spec,
        compiler_params=_params("arbitrary"),
        name="moe_dispatch",
    )(fill_blocks, dest.reshape(n_tiles, 1, t * TOP_K), h2)


UP_COLS = 512
DOWN_COLS = 1024
UNIT_BLOCKS = 2
UNIT_ROWS = UNIT_BLOCKS * MOE_BLOCK
DMA_ROWS = 256


def _expert_rows_pipeline(e, col, blk0_ref, nblk_ref, tail_ref, x_hbm, o_hbm, xbuf, obuf, zbuf,
                          sem_in, sem_out, sem_tail, compute, tail_max):
    n = nblk_ref[e]
    row0 = blk0_ref[e] * MOE_BLOCK
    n_units = n // UNIT_BLOCKS
    has_rest = lax.rem(n, UNIT_BLOCKS) == 1
    width = obuf.shape[2]

    def copies(unit, slot, n_rows, inbound):
        out = []
        for c in range(n_rows // DMA_ROWS):
            hbm_rows = pl.ds(pl.multiple_of(row0 + unit * UNIT_ROWS + c * DMA_ROWS, DMA_ROWS), DMA_ROWS)
            buf_rows = pl.ds(c * DMA_ROWS, DMA_ROWS)
            if inbound:
                out.append(pltpu.make_async_copy(x_hbm.at[hbm_rows], xbuf.at[slot, buf_rows], sem_in.at[slot]))
            else:
                out.append(pltpu.make_async_copy(obuf.at[slot, buf_rows],
                                                 o_hbm.at[hbm_rows, pl.ds(col, width)], sem_out.at[slot]))
        return out

    def start(cs):
        for c in cs:
            c.start()

    def wait(cs):
        for c in cs:
            c.wait()

    @pl.when(n_units > 0)
    def _():
        start(copies(0, 0, UNIT_ROWS, True))

    @pl.when((n_units == 0) & has_rest)
    def _():
        start(copies(0, 0, MOE_BLOCK, True))

    def body(u, carry):
        slot = lax.rem(u, 2)
        wait(copies(u, slot, UNIT_ROWS, True))

        @pl.when(u + 1 < n_units)
        def _():
            start(copies(u + 1, 1 - slot, UNIT_ROWS, True))

        @pl.when((u + 1 == n_units) & has_rest)
        def _():
            start(copies(u + 1, 1 - slot, MOE_BLOCK, True))

        @pl.when(u >= 2)
        def _():
            wait(copies(u - 2, slot, UNIT_ROWS, False))
        obuf[slot] = compute(xbuf[slot]).astype(obuf.dtype)
        start(copies(u, slot, UNIT_ROWS, False))
        return carry
    lax.fori_loop(0, n_units, body, 0)

    @pl.when(n_units >= 2)
    def _():
        wait(copies(n_units - 2, lax.rem(n_units, 2), UNIT_ROWS, False))

    @pl.when(n_units >= 1)
    def _():
        wait(copies(n_units - 1, lax.rem(n_units - 1, 2), UNIT_ROWS, False))

    @pl.when(has_rest)
    def _():
        slot = lax.rem(n_units, 2)
        block = pl.ds(0, MOE_BLOCK)
        wait(copies(n_units, slot, MOE_BLOCK, True))
        obuf[slot, block] = compute(xbuf[slot, block]).astype(obuf.dtype)
        out = copies(n_units, slot, MOE_BLOCK, False)
        start(out)
        wait(out)

    @pl.when(e == N_EXPERTS - 1)
    def _():
        zbuf[...] = jnp.zeros_like(zbuf)
        first, count = tail_ref[0], tail_ref[1]

        def z_copy(t):
            dst = pl.ds(pl.multiple_of((first + t) * MOE_BLOCK, MOE_BLOCK), MOE_BLOCK)
            return pltpu.make_async_copy(zbuf, o_hbm.at[dst, pl.ds(col, width)], sem_tail)
        for t in range(tail_max):
            @pl.when(t < count)
            def _():
                z_copy(t).start()
        for t in range(tail_max):
            @pl.when(t < count)
            def _():
                z_copy(t).wait()


def _expert_up_kernel(blk0_ref, nblk_ref, tail_ref, x_hbm, wg_ref, wu_ref, bg_ref, bu_ref, o_hbm,
                      xbuf, obuf, zbuf, wgb_ref, wub_ref, sem_in, sem_out, sem_tail, *, tail_max):
    j, e = pl.program_id(0), pl.program_id(1)

    @pl.when(nblk_ref[e] > 0)
    def _():
        wgb_ref[...] = wg_ref[...].astype(BF16)
        wub_ref[...] = wu_ref[...].astype(BF16)

    def compute(words):
        xb = jnp.concatenate(_unpack_bf16_pairs(words), axis=1).astype(BF16)
        gate = jnp.minimum(_dot(xb, wgb_ref[...]) + bg_ref[0], SWIGLU_LIMIT)
        up = jnp.clip(_dot(xb, wub_ref[...]) + bu_ref[0], -SWIGLU_LIMIT, SWIGLU_LIMIT)
        return gate * _sigmoid(SWIGLU_ALPHA * gate) * (up + 1.0)
    _expert_rows_pipeline(e, pl.multiple_of(j * UP_COLS, UP_COLS), blk0_ref, nblk_ref, tail_ref,
                          x_hbm, o_hbm, xbuf, obuf, zbuf, sem_in, sem_out, sem_tail, compute, tail_max)


def _expert_scratch(in_width, in_dtype, out_width, out_dtype):
    return [pltpu.VMEM((2, UNIT_ROWS, in_width), in_dtype),
            pltpu.VMEM((2, UNIT_ROWS, out_width), out_dtype),
            pltpu.VMEM((MOE_BLOCK, out_width), out_dtype)]


_EXPERT_SEMS = [pltpu.SemaphoreType.DMA((2,)), pltpu.SemaphoreType.DMA((2,)), pltpu.SemaphoreType.DMA]


def expert_up(xs, blk0, nblk, tail, tail_max, w_gate_up, b_gate_up):
    cap = xs.shape[0]
    tf = UP_COLS
    nj = D_FF // tf
    b3 = b_gate_up.reshape(N_EXPERTS, 1, 2 * D_FF)
    grid_spec = pltpu.PrefetchScalarGridSpec(
        num_scalar_prefetch=3,
        grid=(nj, N_EXPERTS),
        in_specs=[pl.BlockSpec(memory_space=pl.ANY),
                  pl.BlockSpec((None, D_MODEL, tf), lambda j, e, *_: (e, 0, j)),
                  pl.BlockSpec((None, D_MODEL, tf), lambda j, e, *_: (e, 0, nj + j)),
                  pl.BlockSpec((None, 1, tf), lambda j, e, *_: (e, 0, j)),
                  pl.BlockSpec((None, 1, tf), lambda j, e, *_: (e, 0, nj + j))],
        out_specs=pl.BlockSpec(memory_space=pl.ANY),
        scratch_shapes=_expert_scratch(PACK_W, U32, UP_COLS, BF16)
        + [pltpu.VMEM((D_MODEL, tf), BF16), pltpu.VMEM((D_MODEL, tf), BF16)] + _EXPERT_SEMS)
    return pl.pallas_call(
        functools.partial(_expert_up_kernel, tail_max=tail_max),
        out_shape=jax.ShapeDtypeStruct((cap, D_FF), BF16),
        grid_spec=grid_spec,
        compiler_params=_params("arbitrary", "arbitrary"),
        name="expert_up",
    )(blk0, nblk, tail, xs, w_gate_up, w_gate_up, b3, b3)


def _expert_down_kernel(blk0_ref, nblk_ref, tail_ref, a_hbm, w_ref, b_ref, o_hbm,
                        xbuf, obuf, zbuf, wb_ref, sem_in, sem_out, sem_tail, *, tail_max):
    j, e = pl.program_id(0), pl.program_id(1)

    @pl.when(nblk_ref[e] > 0)
    def _():
        wb_ref[...] = w_ref[...].astype(BF16)

    def compute(a):
        y = _dot(a, wb_ref[...]) + b_ref[0]
        return _pack_bf16_pairs(y[:, :DOWN_COLS // 2], y[:, DOWN_COLS // 2:])
    _expert_rows_pipeline(e, pl.multiple_of(j * (DOWN_COLS // 2), DOWN_COLS // 2), blk0_ref, nblk_ref, tail_ref,
                          a_hbm, o_hbm, xbuf, obuf, zbuf, sem_in, sem_out, sem_tail, compute, tail_max)


def expert_down(act, blk0, nblk, tail, tail_max, w_down, b_down):
    cap = act.shape[0]
    tn = DOWN_COLS
    grid_spec = pltpu.PrefetchScalarGridSpec(
        num_scalar_prefetch=3,
        grid=(D_MODEL // tn, N_EXPERTS),
        in_specs=[pl.BlockSpec(memory_space=pl.ANY),
                  pl.BlockSpec((None, D_FF, tn), lambda j, e, *_: (e, 0, j)),
                  pl.BlockSpec((None, 1, tn), lambda j, e, *_: (e, 0, j))],
        out_specs=pl.BlockSpec(memory_space=pl.ANY),
        scratch_shapes=_expert_scratch(D_FF, BF16, tn // 2, U32) + [pltpu.VMEM((D_FF, tn), BF16)] + _EXPERT_SEMS)
    return pl.pallas_call(
        functools.partial(_expert_down_kernel, tail_max=tail_max),
        out_shape=jax.ShapeDtypeStruct((cap, PACK_W), U32),
        grid_spec=grid_spec,
        compiler_params=_params("arbitrary", "arbitrary"),
        name="expert_down",
    )(blk0, nblk, tail, act, w_down, b_down.reshape(N_EXPERTS, 1, D_MODEL))


COMBINE_TILE = 128


def _combine_kernel(dest_ref, yb_hbm, x1_ref, w_ref, mod_ref, o_ref, buf_ref, sem):
    t = COMBINE_TILE

    def start(r, carry):
        for k in range(TOP_K):
            _row_copy(yb_hbm, buf_ref.at[k], sem, dest_ref[0, 0, r * TOP_K + k], r).start()
        return carry
    lax.fori_loop(0, t, start, 0, unroll=4)

    def wait(r, carry):
        for k in range(TOP_K):
            _row_copy(yb_hbm, buf_ref.at[k], sem, 0, r).wait()
        return carry
    lax.fori_loop(0, t, wait, 0, unroll=8)
    half = DOWN_COLS // 2
    for j in range(D_MODEL // DOWN_COLS):
        accs = None
        for k in range(TOP_K):
            w = w_ref[:, TOP_K + k:TOP_K + k + 1]
            parts = [w * p for p in _unpack_bf16_pairs(buf_ref[k, :, j * half:(j + 1) * half])]
            accs = parts if accs is None else [a + p for a, p in zip(accs, parts)]
        for h, acc in enumerate(accs):
            cols = slice(j * DOWN_COLS + h * half, j * DOWN_COLS + (h + 1) * half)
            o_ref[:, cols] = x1_ref[:, cols] + mod_ref[0, 5:6, cols] * acc


def combine(yb, dest, x1, route_out, mod, n_batch, seq):
    rows = n_batch * seq
    t = COMBINE_TILE
    n_tiles = rows // t
    return pl.pallas_call(
        _combine_kernel,
        out_shape=jax.ShapeDtypeStruct((rows, D_MODEL), F32),
        grid=(n_tiles,),
        in_specs=[pl.BlockSpec((1, 1, t * TOP_K), lambda i: (i, 0, 0), memory_space=pltpu.SMEM),
                  pl.BlockSpec(memory_space=pl.ANY),
                  pl.BlockSpec((t, D_MODEL), lambda i: (i, 0)),
                  pl.BlockSpec((t, LANES), lambda i: (i, 0)),
                  pl.BlockSpec((1, 8, D_MODEL), _mod_row_map(n_tiles, seq // t, n_batch))],
        out_specs=pl.BlockSpec((t, D_MODEL), lambda i: (i, 0)),
        scratch_shapes=[pltpu.VMEM((TOP_K, t, PACK_W), U32), pltpu.SemaphoreType.DMA],
        compiler_params=_params("arbitrary"),
        name="moe_combine",
    )(dest.reshape(n_tiles, 1, t * TOP_K), yb, x1, route_out, mod)


def moe_layout(route_out, counts):
    n_tok = route_out.shape[0]
    idx = route_out[:, :TOP_K].astype(jnp.int32)
    rank = route_out[:, 2 * TOP_K:3 * TOP_K].astype(jnp.int32)
    cnt = counts[0, :N_EXPERTS].astype(jnp.int32)
    padded = (cnt + MOE_BLOCK - 1) // MOE_BLOCK * MOE_BLOCK
    pad_end = jnp.cumsum(padded)
    pad_start = pad_end - padded
    dest = (pad_start[idx] + rank).reshape(-1)
    n_blocks = -(-(n_tok * TOP_K + N_EXPERTS * (MOE_BLOCK - 1)) // MOE_BLOCK)
    n_used = pad_end[-1] // MOE_BLOCK
    tail_max = n_blocks - (n_tok * TOP_K) // MOE_BLOCK
    tail = jnp.stack([n_used, n_blocks - n_used]).astype(jnp.int32)
    last_blk = jnp.where(cnt > 0, pad_end // MOE_BLOCK - 1, -1)
    tail_blk = n_used + jnp.arange(tail_max, dtype=jnp.int32)
    fill_blocks = jnp.concatenate([last_blk, jnp.where(tail_blk < n_blocks, tail_blk, -1)]).astype(jnp.int32)
    blk0 = (pad_start // MOE_BLOCK).astype(jnp.int32)
    nblk = (padded // MOE_BLOCK).astype(jnp.int32)
    return dest, blk0, nblk, tail, tail_max, fill_blocks, n_blocks * MOE_BLOCK


def _in_proj_weights(w_in):
    sizes = (XBC_W, D_INNER, N_SSM_HEADS, N_SSM_HEADS, Q_W, KV_W, KV_W, D_MODEL, D_MODEL)
    offs = [0]
    for s in sizes:
        offs.append(offs[-1] + s)
    seg = lambda i: w_in[:, offs[i]:offs[i + 1]]
    w_main = jnp.concatenate([seg(1), seg(0), seg(4), seg(5), seg(6), seg(7), seg(8)], axis=1).astype(BF16)
    w_dt = jnp.concatenate([seg(2), seg(3)], axis=1).astype(BF16)
    return w_main, w_dt


def hybrid_layer(x, ctx, c, c_ctx, w_ada, b_ada, norm1_w, norm2_w, w_in, conv_w, conv_b, dt_bias_f, dt_bias_b,
                 a_log_f, a_log_b, d_skip, ssm_norm_w, q_norm_w, k_norm_w, sink, w_ssm_out, w_attn_out, w_o,
                 router_w, router_b, w_gate_up, b_gate_up, w_down, b_down):
    n_batch, seq, _ = x.shape
    ctx_len = ctx.shape[1]
    n_lat = n_batch * seq
    x2 = x.reshape(n_lat, D_MODEL)
    rows = n_lat + n_batch * ctx_len

    cvec = jnp.zeros((8, D_MODEL), F32).at[:n_batch].set(c).at[n_batch].set(c_ctx)
    mod = ada_modulation(cvec, w_ada, b_ada).reshape(8, 6, D_MODEL)
    mod = jnp.concatenate([mod, jnp.zeros((8, 2, D_MODEL), F32)], axis=1)

    hn = norm_modulate(x2, ctx.reshape(n_batch * ctx_len, D_MODEL), norm1_w, mod, n_batch, seq)
    w_main, w_dt = _in_proj_weights(w_in)
    proj, dt_raw = in_proj(hn, w_main, w_dt)

    xbc = conv_silu(proj, conv_w, conv_b, n_batch, seq, ctx_len)
    dt_bias = jnp.concatenate([dt_bias_f, dt_bias_b]).reshape(1, LANES)
    a_neg = -jnp.exp(jnp.concatenate([a_log_f, a_log_b])).reshape(1, LANES)
    dskip_x = jnp.repeat(d_skip, SSM_HEAD_DIM).reshape(1, D_INNER)
    y_f = ssd_scan(xbc, dt_raw, dt_bias, a_neg, dskip_x, n_batch, seq, ctx_len, rev=False)
    y_b = ssd_scan(xbc, dt_raw, dt_bias, a_neg, dskip_x, n_batch, seq, ctx_len, rev=True)

    tables = _rope_tables(seq)
    qn = qk_prep(proj, q_norm_w, tables, n_lat, n_lat, seq, COL_Q, Q_W, ATTN_SCALE, "q_prep")
    kn = qk_prep(proj, k_norm_w, tables, rows, n_lat, seq, COL_K, KV_W, 1.0, "k_prep")
    y_attn = windowed_attention(qn, kn, proj, sink, n_batch, seq, ctx_len)

    yg = gate_norm(y_f, y_b, proj, ssm_norm_w, n_lat)
    merged = merge_branches(yg, y_attn, w_ssm_out.astype(BF16), w_attn_out.astype(BF16), proj, n_lat)
    x1, h2, logits = out_proj_router(merged, w_o.astype(BF16), x2, mod, norm2_w, router_w, router_b,
                                     n_batch, seq)

    route_out, counts = route(logits)
    dest, blk0, nblk, tail, tail_max, fill_blocks, cap = moe_layout(route_out, counts)
    xs = dispatch_rows(h2, dest, fill_blocks, cap)
    act = expert_up(xs, blk0, nblk, tail, tail_max, w_gate_up, b_gate_up)
    yb = expert_down(act, blk0, nblk, tail, tail_max, w_down, b_down)
    out = combine(yb, dest, x1, route_out, mod, n_batch, seq)
    return out.reshape(n_batch, seq, D_MODEL)


def kernel(x, c, ctx, c_ctx, w_ada, b_ada, norm1_w, norm2_w, w_in, conv_w, conv_b, dt_bias_f, dt_bias_b,
           a_log_f, a_log_b, d_skip, ssm_norm_w, q_norm_w, k_norm_w, sink, w_ssm_out, w_attn_out, w_o,
           router_w, router_b, w_gate_up, b_gate_up, w_down, b_down):
    assert w_ada.shape[0] == 1, "single-layer block"
    return hybrid_layer(x, ctx, c, c_ctx, w_ada[0], b_ada[0], norm1_w[0], norm2_w[0], w_in[0], conv_w[0],
                        conv_b[0], dt_bias_f[0], dt_bias_b[0], a_log_f[0], a_log_b[0], d_skip[0],
                        ssm_norm_w[0], q_norm_w[0], k_norm_w[0], sink[0], w_ssm_out[0], w_attn_out[0], w_o[0],
                        router_w[0], router_b[0], w_gate_up[0], b_gate_up[0], w_down[0], b_down[0])
```

```python
import functools
import math

import jax
import jax.numpy as jnp
from jax import lax
from jax.experimental import pallas as pl
from jax.experimental.pallas import tpu as pltpu

F32 = jnp.float32
BF16 = jnp.bfloat16

D_MODEL = 2048
GRID_W = 64
NORM_EPS = 1e-6
D_INNER = 2 * D_MODEL
SSM_HEAD_DIM = 64
N_SSM_HEADS = D_INNER // SSM_HEAD_DIM
N_GROUPS = 8
HEADS_PER_GROUP = N_SSM_HEADS // N_GROUPS
D_STATE = 128
BC_W = N_GROUPS * D_STATE
XBC_W = D_INNER + 2 * BC_W
CONV_K = 5
SSD_CHUNK = 128
HEAD_DIM = 128
N_Q_HEADS = D_MODEL // HEAD_DIM
N_KV_HEADS = 4
Q_PER_KV = N_Q_HEADS // N_KV_HEADS
Q_W = N_Q_HEADS * HEAD_DIM
KV_W = N_KV_HEADS * HEAD_DIM
WINDOW = 128
ATTN_BLOCK = 128
ATTN_SCALE = HEAD_DIM ** -0.5
ROPE_FREQS = HEAD_DIM // 4
ROPE_BASE = 10000.0
N_EXPERTS = 32
TOP_K = 4
D_FF = D_MODEL
SWIGLU_LIMIT = 7.0
SWIGLU_ALPHA = 1.702
MOE_BLOCK = 512

LOG2_E = math.log2(math.e)
LANES = 128
ROW_TILE = 256
MAIN_W = D_INNER + XBC_W + Q_W + 2 * KV_W + 2 * D_MODEL
COL_Z, COL_XBC, COL_Q = 0, D_INNER, D_INNER + XBC_W
COL_K, COL_V = COL_Q + Q_W, COL_Q + Q_W + KV_W
COL_GS, COL_GA = COL_V + KV_W, COL_V + KV_W + D_MODEL
VMEM_LIMIT = 56 * 1024 * 1024
EXPERT_DOWN_VMEM = 60 * 1024 * 1024


def _params(*sem, vmem_limit=VMEM_LIMIT):
    return pltpu.CompilerParams(dimension_semantics=sem, vmem_limit_bytes=vmem_limit)


def _dot(a, b):
    return jnp.dot(a, b, preferred_element_type=F32)


def _dot_nt(a, b):
    return lax.dot_general(a, b, (((1,), (1,)), ((), ())), preferred_element_type=F32)


def _split3(a):
    a1 = a.astype(BF16)
    r = a - a1.astype(F32)
    a2 = r.astype(BF16)
    a3 = (r - a2.astype(F32)).astype(BF16)
    return a1, a2, a3


def _dot_f32(a, b):
    a1, a2, _ = _split3(a)
    b1, b2, _ = _split3(b)
    return _dot(a1, b1) + (_dot(a1, b2) + _dot(a2, b1))


def _dot_sel_rhs(a, sel):
    a1, a2, a3 = _split3(a)
    return _dot(a1, sel) + (_dot(a2, sel) + _dot(a3, sel))


def _dot_sel_lhs(sel, a):
    a1, a2, a3 = _split3(a)
    return _dot(sel, a1) + (_dot(sel, a2) + _dot(sel, a3))


def _sigmoid(x):
    return 1.0 / (1.0 + jnp.exp(-x))


U32 = jnp.uint32
PACK_W = D_MODEL // 2


def _pack_bf16_pairs(lo, hi):
    lo_bits = lax.bitcast_convert_type(lo.astype(BF16).astype(F32), U32)
    hi_bits = lax.bitcast_convert_type(hi.astype(BF16).astype(F32), U32)
    return (lo_bits >> 16) | (hi_bits & U32(0xFFFF0000))


def _unpack_bf16_pairs(words):
    lo = lax.bitcast_convert_type(words << 16, F32)
    hi = lax.bitcast_convert_type(words & U32(0xFFFF0000), F32)
    return lo, hi


def _ada_kernel(c_ref, w_ref, b_ref, o_ref):
    c = c_ref[...]
    o_ref[...] = _dot_f32(c * _sigmoid(c), w_ref[...]) + b_ref[...]


def ada_modulation(cvec, w_ada, b_ada):
    n = w_ada.shape[1]
    tn = 1024
    return pl.pallas_call(
        _ada_kernel,
        out_shape=jax.ShapeDtypeStruct((8, n), F32),
        grid=(n // tn,),
        in_specs=[pl.BlockSpec((8, D_MODEL), lambda j: (0, 0)),
                  pl.BlockSpec((D_MODEL, tn), lambda j: (0, j)),
                  pl.BlockSpec((1, tn), lambda j: (0, j))],
        out_specs=pl.BlockSpec((8, tn), lambda j: (0, j)),
        compiler_params=_params("arbitrary"),
        name="ada_modulation",
    )(cvec, w_ada, b_ada.reshape(1, n))


def _norm_mod_kernel(x_ref, c_ref, w_ref, mod_ref, o_ref, *, n_lat_tiles):
    def emit(x):
        y = x * lax.rsqrt(jnp.mean(x * x, axis=-1, keepdims=True) + NORM_EPS) * w_ref[...]
        o_ref[...] = (y * (1.0 + mod_ref[0, 1:2, :]) + mod_ref[0, 0:1, :]).astype(o_ref.dtype)

    @pl.when(pl.program_id(0) < n_lat_tiles)
    def _():
        emit(x_ref[...])

    @pl.when(pl.program_id(0) >= n_lat_tiles)
    def _():
        emit(c_ref[...])


def _mod_row_map(n_lat_tiles, tiles_per_batch, n_batch):
    def index_map(i, *_):
        return (jnp.where(i < n_lat_tiles, i // tiles_per_batch, n_batch), 0, 0)
    return index_map


def norm_modulate(x2, ctx2, norm_w, mod, n_batch, seq):
    rows = x2.shape[0] + ctx2.shape[0]
    n_lat_tiles = n_batch * seq // ROW_TILE
    return pl.pallas_call(
        functools.partial(_norm_mod_kernel, n_lat_tiles=n_lat_tiles),
        out_shape=jax.ShapeDtypeStruct((rows, D_MODEL), BF16),
        grid=(rows // ROW_TILE,),
        in_specs=[pl.BlockSpec((ROW_TILE, D_MODEL), lambda i: (jnp.minimum(i, n_lat_tiles - 1), 0)),
                  pl.BlockSpec((ROW_TILE, D_MODEL), lambda i: (jnp.maximum(i - n_lat_tiles, 0), 0)),
                  pl.BlockSpec((1, D_MODEL), lambda i: (0, 0)),
                  pl.BlockSpec((1, 8, D_MODEL), _mod_row_map(n_lat_tiles, seq // ROW_TILE, n_batch))],
        out_specs=pl.BlockSpec((ROW_TILE, D_MODEL), lambda i: (i, 0)),
        compiler_params=_params("arbitrary"),
        name="norm_modulate",
    )(x2, ctx2, norm_w.reshape(1, D_MODEL), mod)


def _largest_tile(n, cap, multiple):
    return max(d for d in range(multiple, cap + 1, multiple) if n % d == 0)


def _in_proj_kernel(a_ref, b_ref, bdt_ref, o_ref, odt_ref):
    a = a_ref[...]
    o_ref[...] = _dot(a, b_ref[...]).astype(o_ref.dtype)

    @pl.when(pl.program_id(1) == 0)
    def _():
        odt_ref[...] = _dot(a, bdt_ref[...])


def in_proj(hn, w_main, w_dt):
    m, k = hn.shape
    n = w_main.shape[1]
    tm = _largest_tile(m, 2112, 16)
    tn = 1024
    return pl.pallas_call(
        _in_proj_kernel,
        out_shape=(jax.ShapeDtypeStruct((m, n), BF16), jax.ShapeDtypeStruct((m, LANES), F32)),
        grid=(m // tm, n // tn),
        in_specs=[pl.BlockSpec((tm, k), lambda i, j: (i, 0)),
                  pl.BlockSpec((k, tn), lambda i, j: (0, j)),
                  pl.BlockSpec((k, LANES), lambda i, j: (0, 0))],
        out_specs=(pl.BlockSpec((tm, tn), lambda i, j: (i, j)),
                   pl.BlockSpec((tm, LANES), lambda i, j: (i, 0))),
        compiler_params=_params("arbitrary", "arbitrary"),
        name="in_proj",
    )(hn, w_main, w_dt)


CONV_COLS = 2048
HALO = 16


def _conv_kernel(prev_ref, cur_ref, next_ref, w_ref, b_ref, o_ref, *,
                 n_lat_tiles, lat_tiles_per_seq, ctx_tiles_per_seq):
    i = pl.program_id(1)
    in_lat = i < n_lat_tiles
    pos = jnp.where(in_lat, i % lat_tiles_per_seq, (i - n_lat_tiles) % ctx_tiles_per_seq)
    per_seq = jnp.where(in_lat, lat_tiles_per_seq, ctx_tiles_per_seq)
    has_prev = pos > 0
    has_next = pos < per_seq - 1
    t = cur_ref.shape[0]
    n = t + 2 * HALO
    full = jnp.concatenate([jnp.where(has_prev, prev_ref[...].astype(F32), 0.0), cur_ref[...].astype(F32),
                            jnp.where(has_next, next_ref[...].astype(F32), 0.0)], axis=0)
    acc = jnp.broadcast_to(b_ref[...], (t, CONV_COLS))
    for k in range(CONV_K):
        shift = CONV_K // 2 - k
        shifted = full if shift == 0 else pltpu.roll(full, shift % n, 0)
        acc = acc + w_ref[k:k + 1, :] * shifted[HALO:HALO + t, :]
    o_ref[...] = acc * _sigmoid(acc)


def conv_silu(proj, conv_w, conv_b, n_batch, seq, ctx_len):
    rows = proj.shape[0]
    n_lat_tiles = n_batch * seq // ROW_TILE
    col0 = COL_XBC // CONV_COLS
    per = ROW_TILE // HALO
    last_halo = rows // HALO - 1
    kern = functools.partial(_conv_kernel, n_lat_tiles=n_lat_tiles,
                             lat_tiles_per_seq=seq // ROW_TILE, ctx_tiles_per_seq=ctx_len // ROW_TILE)
    w8 = jnp.concatenate([conv_w, jnp.zeros((8 - CONV_K, XBC_W), F32)], axis=0)
    return pl.pallas_call(
        kern,
        out_shape=jax.ShapeDtypeStruct((rows, XBC_W), F32),
        grid=(XBC_W // CONV_COLS, rows // ROW_TILE),
        in_specs=[pl.BlockSpec((HALO, CONV_COLS), lambda j, i: (jnp.maximum(i * per - 1, 0), col0 + j)),
                  pl.BlockSpec((ROW_TILE, CONV_COLS), lambda j, i: (i, col0 + j)),
                  pl.BlockSpec((HALO, CONV_COLS), lambda j, i: (jnp.minimum((i + 1) * per, last_halo), col0 + j)),
                  pl.BlockSpec((8, CONV_COLS), lambda j, i: (0, j)),
                  pl.BlockSpec((1, CONV_COLS), lambda j, i: (0, j))],
        out_specs=pl.BlockSpec((ROW_TILE, CONV_COLS), lambda j, i: (i, j)),
        compiler_params=_params("arbitrary", "arbitrary"),
        name="conv_silu",
    )(proj, proj, proj, w8, conv_b.reshape(1, XBC_W))


def _softplus(x):
    return jnp.maximum(x, 0.0) + jnp.log(1.0 + jnp.exp(-jnp.abs(x)))


def _ssd_kernel(xs_ref, b_ref, c_ref, dt_ref, bias_ref, a_ref, dskip_ref, y_ref, h_ref, *, rev):
    @pl.when(pl.program_id(1) == 0)
    def _():
        h_ref[...] = jnp.zeros_like(h_ref)

    L = SSD_CHUNK
    gw = HEADS_PER_GROUP * SSM_HEAD_DIM
    off = N_SSM_HEADS if rev else 0
    row = lax.broadcasted_iota(jnp.int32, (L, L), 0)
    col = lax.broadcasted_iota(jnp.int32, (L, L), 1)
    causal = (col >= row) if rev else (col <= row)
    first_head = col < SSM_HEAD_DIM
    tmat = jnp.where(causal, 1.0, 0.0).astype(BF16)

    dt = _softplus(dt_ref[...] + bias_ref[...])
    a = dt * (a_ref[...] * LOG2_E)
    acum = _dot_sel_lhs(tmat, a)
    acum_t = acum.T
    dt_t = dt.T
    last = 0 if rev else L - 1
    to_end_t = jnp.exp2(acum_t[:, last:last + 1] - acum_t) * dt_t
    src_t = acum_t - jnp.log2(dt_t)

    for g in range(N_GROUPS):
        bg = b_ref[:, g * D_STATE:(g + 1) * D_STATE]
        cgb = c_ref[:, g * D_STATE:(g + 1) * D_STATE].astype(BF16)
        cb = _dot_nt(cgb, bg.astype(BF16))
        bg_t = bg.T
        y_off = _dot(cgb, h_ref[:, g * gw:(g + 1) * gw].astype(BF16))
        for pair in range(HEADS_PER_GROUP // 2):
            c0 = g * gw + pair * LANES
            x_f = xs_ref[:, c0:c0 + LANES]
            x2 = jnp.concatenate([jnp.where(first_head, x_f, 0.0).astype(BF16),
                                  jnp.where(first_head, 0.0, x_f).astype(BF16)], axis=0)
            ms, ws, bcs = [], [], []
            for k in range(2):
                hd = off + g * HEADS_PER_GROUP + 2 * pair + k
                bc = jnp.broadcast_to(acum[:, hd:hd + 1], (L, L))
                decay_dt = jnp.exp2(jnp.where(causal, bc - src_t[hd:hd + 1, :], -jnp.inf))
                ms.append((cb * decay_dt).astype(BF16))
                ws.append((bg_t * to_end_t[hd:hd + 1, :]).astype(BF16))
                bcs.append(bc)
            lhs = jnp.concatenate([jnp.concatenate(ms, axis=1), jnp.concatenate(ws, axis=1)], axis=0)
            res = _dot(lhs, x2)
            e_t = jnp.exp2(jnp.where(first_head, bcs[0], bcs[1]))
            y = res[:L] + y_off[:, pair * LANES:(pair + 1) * LANES] * e_t
            if not rev:
                y = y + dskip_ref[:, c0:c0 + LANES] * x_f
            y_ref[:, c0:c0 + LANES] = y
            h_ref[:, c0:c0 + LANES] = e_t[last:last + 1, :] * h_ref[:, c0:c0 + LANES] + res[L:]


def ssd_scan(xbc, dt_raw, dt_bias, a_neg, dskip_x, n_batch, seq, ctx_len, rev):
    rows = xbc.shape[0]
    nc, ncc = seq // SSD_CHUNK, ctx_len // SSD_CHUNK
    lat_blocks = n_batch * nc

    def blk(b, j):
        cj = (ncc - 1 - j) if rev else j
        lj = (nc - 1 - (j - ncc)) if rev else (j - ncc)
        return jnp.where(j < ncc, lat_blocks + b * ncc + cj, b * nc + lj)

    return pl.pallas_call(
        functools.partial(_ssd_kernel, rev=rev),
        out_shape=jax.ShapeDtypeStruct((rows, D_INNER), F32),
        grid=(n_batch, ncc + nc),
        in_specs=[pl.BlockSpec((SSD_CHUNK, D_INNER), lambda b, j: (blk(b, j), 0)),
                  pl.BlockSpec((SSD_CHUNK, BC_W), lambda b, j: (blk(b, j), D_INNER // BC_W)),
                  pl.BlockSpec((SSD_CHUNK, BC_W), lambda b, j: (blk(b, j), D_INNER // BC_W + 1)),
                  pl.BlockSpec((SSD_CHUNK, LANES), lambda b, j: (blk(b, j), 0)),
                  pl.BlockSpec((1, LANES), lambda b, j: (0, 0)),
                  pl.BlockSpec((1, LANES), lambda b, j: (0, 0)),
                  pl.BlockSpec((1, D_INNER), lambda b, j: (0, 0))],
        out_specs=pl.BlockSpec((SSD_CHUNK, D_INNER), lambda b, j: (blk(b, j), 0)),
        scratch_shapes=[pltpu.VMEM((D_STATE, D_INNER), F32)],
        compiler_params=_params("arbitrary", "arbitrary"),
        name="ssd_scan_bwd" if rev else "ssd_scan_fwd",
    )(xbc, xbc, xbc, dt_raw, dt_bias, a_neg, dskip_x)


def _rope_tables(seq):
    inv_freq = ROPE_BASE ** (-jnp.arange(ROPE_FREQS, dtype=F32) / ROPE_FREQS)
    n_rows = seq // GRID_W
    r = jnp.repeat(jnp.arange(n_rows, dtype=F32), GRID_W)
    c = jnp.tile(jnp.arange(GRID_W, dtype=F32), n_rows)
    ar = r[:, None] * inv_freq
    ac = c[:, None] * inv_freq
    ang = jnp.concatenate([ar, ar, ac, ac], axis=-1)
    cos, sin = jnp.cos(ang), jnp.sin(ang)
    first_half = (jnp.arange(HEAD_DIM) % (2 * ROPE_FREQS)) < ROPE_FREQS
    sin_up = jnp.where(first_half, -sin, 0.0)
    sin_dn = jnp.where(first_half, 0.0, sin)
    return cos, sin_up, sin_dn


def _qk_prep_kernel(x_ref, w_ref, cos_ref, su_ref, sd_ref, o_ref, *, n_heads, scale, n_lat_tiles):
    rotate = pl.program_id(0) < n_lat_tiles
    cos = jnp.where(rotate, cos_ref[...], 1.0)
    su = jnp.where(rotate, su_ref[...], 0.0)
    sd = jnp.where(rotate, sd_ref[...], 0.0)
    w = w_ref[...]
    for h in range(n_heads):
        hs = slice(h * HEAD_DIM, (h + 1) * HEAD_DIM)
        x = x_ref[:, hs].astype(F32)
        n = x * lax.rsqrt(jnp.mean(x * x, axis=-1, keepdims=True) + NORM_EPS) * w
        y = n * cos + pltpu.roll(n, HEAD_DIM - ROPE_FREQS, 1) * su + pltpu.roll(n, ROPE_FREQS, 1) * sd
        if scale != 1.0:
            y = y * scale
        o_ref[:, hs] = y.astype(o_ref.dtype)


def qk_prep(proj, norm_w, tables, rows, n_lat, seq, col, width, scale, name):
    n_heads = width // HEAD_DIM
    tiles_per_seq = seq // ROW_TILE
    tab_spec = pl.BlockSpec((ROW_TILE, HEAD_DIM), lambda i: (i % tiles_per_seq, 0))
    return pl.pallas_call(
        functools.partial(_qk_prep_kernel, n_heads=n_heads, scale=scale, n_lat_tiles=n_lat // ROW_TILE),
        out_shape=jax.ShapeDtypeStruct((rows, width), BF16),
        grid=(rows // ROW_TILE,),
        in_specs=[pl.BlockSpec((ROW_TILE, width), lambda i: (i, col // width)),
                  pl.BlockSpec((1, HEAD_DIM), lambda i: (0, 0)),
                  tab_spec, tab_spec, tab_spec],
        out_specs=pl.BlockSpec((ROW_TILE, width), lambda i: (i, 0)),
        compiler_params=_params("arbitrary"),
        name=name,
    )(proj, norm_w.reshape(1, HEAD_DIM), *tables)


def _attn_kernel(q_ref, kp_ref, kc_ref, kn_ref, vp_ref, vc_ref, vn_ref, kx_ref, vx_ref, sink_ref, o_ref, *,
                 n_blocks):
    i = pl.program_id(1)
    T = ATTN_BLOCK
    nq = Q_PER_KV * T
    nk = 3 * T + kx_ref.shape[0]
    qi = lax.broadcasted_iota(jnp.int32, (nq, nk), 0) % T
    kj = lax.broadcasted_iota(jnp.int32, (nq, nk), 1)
    lo = jnp.maximum(qi, jnp.where(i > 0, 0, T))
    hi = jnp.minimum(qi + 2 * WINDOW, jnp.where(i < n_blocks - 1, 3 * T - 1, 2 * T - 1))
    valid = ((kj >= lo) & (kj <= hi)) | (kj >= 3 * T)
    for h in range(N_KV_HEADS):
        hs = slice(h * HEAD_DIM, (h + 1) * HEAD_DIM)
        q = jnp.concatenate([q_ref[:, (h * Q_PER_KV + g) * HEAD_DIM:(h * Q_PER_KV + g + 1) * HEAD_DIM]
                             for g in range(Q_PER_KV)], axis=0)
        kb = jnp.concatenate([kp_ref[:, hs], kc_ref[:, hs], kn_ref[:, hs], kx_ref[:, hs]], axis=0)
        vb = jnp.concatenate([vp_ref[:, hs], vc_ref[:, hs], vn_ref[:, hs], vx_ref[:, hs]], axis=0)
        s = jnp.where(valid, _dot_nt(q, kb), -jnp.inf)
        sk = jnp.concatenate([jnp.broadcast_to(sink_ref[0:1, h * Q_PER_KV + g:h * Q_PER_KV + g + 1], (T, 1))
                              for g in range(Q_PER_KV)], axis=0)
        mx = jnp.maximum(jnp.max(s, axis=-1, keepdims=True), sk)
        p = jnp.exp(s - mx)
        denom = jnp.sum(p, axis=-1, keepdims=True) + jnp.exp(sk - mx)
        o = _dot(p.astype(BF16), vb) * (1.0 / denom)
        for g in range(Q_PER_KV):
            c0 = (h * Q_PER_KV + g) * HEAD_DIM
            o_ref[:, c0:c0 + HEAD_DIM] = o[g * T:(g + 1) * T, :].astype(o_ref.dtype)


def windowed_attention(qn, kn, proj, sink, n_batch, seq, ctx_len):
    nb = seq // ATTN_BLOCK
    ctx0 = n_batch * seq // ctx_len
    sink_row = jnp.zeros((1, LANES), F32).at[0, :N_Q_HEADS].set(sink)
    v_col = COL_V // KV_W

    def kv_spec(d, col):
        return pl.BlockSpec((ATTN_BLOCK, KV_W), lambda b, i: (b * nb + jnp.clip(i + d, 0, nb - 1), col))

    def ctx_spec(col):
        return pl.BlockSpec((ctx_len, KV_W), lambda b, i: (ctx0 + b, col))
    return pl.pallas_call(
        functools.partial(_attn_kernel, n_blocks=nb),
        out_shape=jax.ShapeDtypeStruct((n_batch * seq, Q_W), BF16),
        grid=(n_batch, nb),
        in_specs=[pl.BlockSpec((ATTN_BLOCK, Q_W), lambda b, i: (b * nb + i, 0)),
                  kv_spec(-1, 0), kv_spec(0, 0), kv_spec(1, 0),
                  kv_spec(-1, v_col), kv_spec(0, v_col), kv_spec(1, v_col),
                  ctx_spec(0), ctx_spec(v_col),
                  pl.BlockSpec((1, LANES), lambda b, i: (0, 0))],
        out_specs=pl.BlockSpec((ATTN_BLOCK, Q_W), lambda b, i: (b * nb + i, 0)),
        compiler_params=_params("arbitrary", "arbitrary"),
        name="windowed_attention",
    )(qn, kn, kn, kn, proj, proj, proj, kn, proj, sink_row)


def _gate_norm_kernel(yf_ref, yb_ref, z_ref, w_ref, o_ref):
    gw = D_INNER // N_GROUPS
    for g in range(N_GROUPS):
        gs = slice(g * gw, (g + 1) * gw)
        z = z_ref[:, gs].astype(F32)
        s = (yf_ref[:, gs] + yb_ref[:, gs]) * (z * _sigmoid(z))
        n = s * lax.rsqrt(jnp.mean(s * s, axis=-1, keepdims=True) + NORM_EPS) * w_ref[:, gs]
        o_ref[:, gs] = n.astype(o_ref.dtype)


def gate_norm(y_f, y_b, proj, ssm_norm_w, rows):
    spec = pl.BlockSpec((ROW_TILE, D_INNER), lambda i: (i, 0))
    return pl.pallas_call(
        _gate_norm_kernel,
        out_shape=jax.ShapeDtypeStruct((rows, D_INNER), BF16),
        grid=(rows // ROW_TILE,),
        in_specs=[spec, spec, pl.BlockSpec((ROW_TILE, D_INNER), lambda i: (i, COL_Z // D_INNER)),
                  pl.BlockSpec((1, D_INNER), lambda i: (0, 0))],
        out_specs=spec,
        compiler_params=_params("arbitrary"),
        name="gate_norm",
    )(y_f, y_b, proj, ssm_norm_w.reshape(1, D_INNER))


def _merge_kernel(yg_ref, ya_ref, ws_ref, wa_ref, gs_ref, ga_ref, o_ref):
    o = (_sigmoid(gs_ref[...].astype(F32)) * _dot(yg_ref[...], ws_ref[...])
         + _sigmoid(ga_ref[...].astype(F32)) * _dot(ya_ref[...], wa_ref[...]))
    o_ref[...] = o.astype(o_ref.dtype)


def merge_branches(yg, ya, w_ssm_out, w_attn_out, proj, rows):
    tm, tn = 512, 1024
    return pl.pallas_call(
        _merge_kernel,
        out_shape=jax.ShapeDtypeStruct((rows, D_MODEL), BF16),
        grid=(D_MODEL // tn, rows // tm),
        in_specs=[pl.BlockSpec((tm, D_INNER), lambda j, i: (i, 0)),
                  pl.BlockSpec((tm, Q_W), lambda j, i: (i, 0)),
                  pl.BlockSpec((D_INNER, tn), lambda j, i: (0, j)),
                  pl.BlockSpec((Q_W, tn), lambda j, i: (0, j)),
                  pl.BlockSpec((tm, tn), lambda j, i: (i, COL_GS // tn + j)),
                  pl.BlockSpec((tm, tn), lambda j, i: (i, COL_GA // tn + j))],
        out_specs=pl.BlockSpec((tm, tn), lambda j, i: (i, j)),
        compiler_params=_params("arbitrary", "arbitrary"),
        name="merge_branches",
    )(yg, ya, w_ssm_out, w_attn_out, proj, proj)


def _out_proj_kernel(m_ref, wo_ref, x_ref, mod_ref, nw_ref, rw_ref, rb_ref, x1_ref, h2_ref, lg_ref):
    x1 = x_ref[...] + mod_ref[0, 2:3, :] * _dot(m_ref[...], wo_ref[...])
    x1_ref[...] = x1
    n = x1 * lax.rsqrt(jnp.mean(x1 * x1, axis=-1, keepdims=True) + NORM_EPS) * nw_ref[...]
    h2 = n * (1.0 + mod_ref[0, 4:5, :]) + mod_ref[0, 3:4, :]
    h2_ref[...] = _pack_bf16_pairs(h2[:, :PACK_W], h2[:, PACK_W:])
    lg_ref[...] = _dot_f32(h2, rw_ref[...]) + rb_ref[...]


def out_proj_router(merged, w_o, xa, mod, norm2_w, router_w, router_b, n_batch, seq):
    rows = n_batch * seq
    tm = 2 * ROW_TILE
    n_tiles = rows // tm
    rw = jnp.zeros((D_MODEL, LANES), F32).at[:, :N_EXPERTS].set(router_w)
    rb = jnp.zeros((1, LANES), F32).at[0, :N_EXPERTS].set(router_b)
    tile = pl.BlockSpec((tm, D_MODEL), lambda i: (i, 0))
    return pl.pallas_call(
        _out_proj_kernel,
        out_shape=(jax.ShapeDtypeStruct((rows, D_MODEL), F32),
                   jax.ShapeDtypeStruct((rows, PACK_W), U32),
                   jax.ShapeDtypeStruct((rows, LANES), F32)),
        grid=(n_tiles,),
        in_specs=[tile,
                  pl.BlockSpec((D_MODEL, D_MODEL), lambda i: (0, 0)),
                  tile,
                  pl.BlockSpec((1, 8, D_MODEL), _mod_row_map(n_tiles, seq // tm, n_batch)),
                  pl.BlockSpec((1, D_MODEL), lambda i: (0, 0)),
                  pl.BlockSpec((D_MODEL, LANES), lambda i: (0, 0)),
                  pl.BlockSpec((1, LANES), lambda i: (0, 0))],
        out_specs=(tile, pl.BlockSpec((tm, PACK_W), lambda i: (i, 0)),
                   pl.BlockSpec((tm, LANES), lambda i: (i, 0))),
        compiler_params=_params("arbitrary"),
        name="out_proj_router",
    )(merged, w_o, xa, mod, norm2_w.reshape(1, D_MODEL), rw, rb)


def _route_kernel(lg_ref, o_ref, cnt_ref, carry_ref):
    @pl.when(pl.program_id(0) == 0)
    def _():
        carry_ref[...] = jnp.zeros_like(carry_ref)

    t = lg_ref.shape[0]
    lane = lax.broadcasted_iota(jnp.int32, (t, LANES), 1).astype(F32)
    work = jnp.where(lane < N_EXPERTS, lg_ref[...], -jnp.inf)
    vals, idxs = [], []
    for _ in range(TOP_K):
        m = jnp.max(work, axis=-1, keepdims=True)
        idx = jnp.min(jnp.where(work == m, lane, float(LANES)), axis=-1, keepdims=True)
        vals.append(m)
        idxs.append(idx)
        work = jnp.where(lane == idx, -jnp.inf, work)
    es = [jnp.exp(v - vals[0]) for v in vals]
    inv = 1.0 / (es[0] + es[1] + es[2] + es[3])
    onehot = jnp.zeros((t, LANES), F32)
    for idx in idxs:
        onehot = onehot + jnp.where(lane == idx, 1.0, 0.0)
    r = lax.broadcasted_iota(jnp.int32, (t, t), 0)
    c = lax.broadcasted_iota(jnp.int32, (t, t), 1)
    before = jnp.where(c < r, 1.0, 0.0).astype(BF16)
    excl = _dot(before, onehot.astype(BF16)) + carry_ref[...]
    out = jnp.zeros((t, LANES), F32)
    for k in range(TOP_K):
        rank = jnp.sum(jnp.where(lane == idxs[k], excl, 0.0), axis=-1, keepdims=True)
        out = jnp.where(lane == k, idxs[k], out)
        out = jnp.where(lane == TOP_K + k, es[k] * inv, out)
        out = jnp.where(lane == 2 * TOP_K + k, rank, out)
    o_ref[...] = out
    carry_ref[...] = carry_ref[...] + jnp.sum(onehot, axis=0, keepdims=True)
    cnt_ref[...] = carry_ref[...]


def route(logits):
    n_tok = logits.shape[0]
    return pl.pallas_call(
        _route_kernel,
        out_shape=(jax.ShapeDtypeStruct((n_tok, LANES), F32), jax.ShapeDtypeStruct((1, LANES), F32)),
        grid=(n_tok // ROW_TILE,),
        in_specs=[pl.BlockSpec((ROW_TILE, LANES), lambda i: (i, 0))],
        out_specs=(pl.BlockSpec((ROW_TILE, LANES), lambda i: (i, 0)),
                   pl.BlockSpec((1, LANES), lambda i: (0, 0))),
        scratch_shapes=[pltpu.VMEM((1, LANES), F32)],
        compiler_params=_params("arbitrary"),
        name="route",
    )(logits)


def _row_copy(src_ref, dst_ref, sem, src_row, dst_row):
    return pltpu.make_async_copy(src_ref.at[pl.ds(src_row, 1)], dst_ref.at[pl.ds(dst_row, 1)], sem)


DISPATCH_TILE = 128


def _dispatch_kernel(fill_ref, dest_ref, h_ref, o_hbm, zero_ref, fill_sem, sem):
    def fill_copy(blk):
        return pltpu.make_async_copy(zero_ref, o_hbm.at[pl.ds(blk * MOE_BLOCK, MOE_BLOCK)], fill_sem)

    @pl.when(pl.program_id(0) == 0)
    def _():
        zero_ref[...] = jnp.zeros_like(zero_ref)
        for s in range(fill_ref.shape[0]):
            @pl.when(fill_ref[s] >= 0)
            def _():
                fill_copy(fill_ref[s]).start()
        for s in range(fill_ref.shape[0]):
            @pl.when(fill_ref[s] >= 0)
            def _():
                fill_copy(0).wait()

    t = h_ref.shape[0]

    def start(r, carry):
        for k in range(TOP_K):
            _row_copy(h_ref, o_hbm, sem, r, dest_ref[0, 0, r * TOP_K + k]).start()
        return carry
    lax.fori_loop(0, t, start, 0, unroll=4)

    def wait(r, carry):
        for k in range(TOP_K):
            _row_copy(h_ref, o_hbm, sem, r, 0).wait()
        return carry
    lax.fori_loop(0, t, wait, 0, unroll=8)


def dispatch_rows(h2, dest, fill_blocks, cap):
    n_tok = h2.shape[0]
    t = DISPATCH_TILE
    n_tiles = n_tok // t
    grid_spec = pltpu.PrefetchScalarGridSpec(
        num_scalar_prefetch=1,
        grid=(n_tiles,),
        in_specs=[pl.BlockSpec((1, 1, t * TOP_K), lambda i, fb: (i, 0, 0), memory_space=pltpu.SMEM),
                  pl.BlockSpec((t, PACK_W), lambda i, fb: (i, 0))],
        out_specs=pl.BlockSpec(memory_space=pl.ANY),
        scratch_shapes=[pltpu.VMEM((MOE_BLOCK, PACK_W), U32), pltpu.SemaphoreType.DMA,
                        pltpu.SemaphoreType.DMA])
    return pl.pallas_call(
        _dispatch_kernel,
        out_shape=jax.ShapeDtypeStruct((cap, PACK_W), U32),
        grid_spec=grid_spec,
        compiler_params=_params("arbitrary"),
        name="moe_dispatch",
    )(fill_blocks, dest.reshape(n_tiles, 1, t * TOP_K), h2)


UP_COLS = 1024
DOWN_COLS = D_MODEL
DMA_CHUNKS = 4


def _expert_rows_pipeline(j, n_col_tiles, col, blk0_ref, nblk_ref, tail_ref, x_hbm, o_hbm, xbuf, obuf, zbuf,
                          sem_in, sem_out, sem_tail, compute, tail_max):
    e = pl.program_id(1)
    n = nblk_ref[e]
    width = obuf.shape[2]
    chunk = MOE_BLOCK // DMA_CHUNKS

    def rows(expert, blk, c):
        return pl.ds(pl.multiple_of((blk0_ref[expert] + blk) * MOE_BLOCK + c * chunk, chunk), chunk)

    def x_copy(expert, blk, slot):
        return [pltpu.make_async_copy(x_hbm.at[rows(expert, blk, c)], xbuf.at[slot, pl.ds(c * chunk, chunk)],
                                      sem_in.at[slot]) for c in range(DMA_CHUNKS)]

    def o_copy(blk, slot):
        return [pltpu.make_async_copy(obuf.at[slot, pl.ds(c * chunk, chunk)],
                                      o_hbm.at[rows(e, blk, c), pl.ds(col, width)], sem_out.at[slot])
                for c in range(DMA_CHUNKS)]

    def start(copies):
        for c in copies:
            c.start()

    def wait(copies):
        for c in copies:
            c.wait()

    @pl.when((j == 0) & (e == 0) & (n > 0))
    def _():
        start(x_copy(e, 0, 0))

    @pl.when(n > 0)
    def _():
        def body(blk, carry):
            slot = lax.rem(blk, 2)
            wait(x_copy(e, blk, slot))

            @pl.when(blk + 1 < n)
            def _():
                start(x_copy(e, blk + 1, 1 - slot))

            @pl.when(blk >= 2)
            def _():
                wait(o_copy(blk - 2, slot))
            obuf[slot] = compute(xbuf[slot]).astype(obuf.dtype)
            start(o_copy(blk, slot))
            return carry
        lax.fori_loop(0, n, body, 0)

        @pl.when(n >= 2)
        def _():
            wait(o_copy(n - 2, lax.rem(n, 2)))
        wait(o_copy(n - 1, lax.rem(n - 1, 2)))

    e_next = jnp.where(e == N_EXPERTS - 1, 0, e + 1)
    has_next = (e < N_EXPERTS - 1) | (j < n_col_tiles - 1)

    @pl.when(has_next & (nblk_ref[e_next] > 0))
    def _():
        start(x_copy(e_next, 0, 0))

    @pl.when(e == N_EXPERTS - 1)
    def _():
        zbuf[...] = jnp.zeros_like(zbuf)
        first, count = tail_ref[0], tail_ref[1]

        def z_copy(t):
            dst = pl.ds(pl.multiple_of((first + t) * MOE_BLOCK, MOE_BLOCK), MOE_BLOCK)
            return pltpu.make_async_copy(zbuf, o_hbm.at[dst, pl.ds(col, width)], sem_tail)
        for t in range(tail_max):
            @pl.when(t < count)
            def _():
                z_copy(t).start()
        for t in range(tail_max):
            @pl.when(t < count)
            def _():
                z_copy(t).wait()


def _expert_up_kernel(blk0_ref, nblk_ref, tail_ref, x_hbm, wg_ref, wu_ref, bg_ref, bu_ref, o_hbm,
                      xbuf, obuf, zbuf, wgb_ref, wub_ref, sem_in, sem_out, sem_tail, *, tail_max):
    j, e = pl.program_id(0), pl.program_id(1)

    @pl.when(nblk_ref[e] > 0)
    def _():
        wgb_ref[...] = wg_ref[...].astype(BF16)
        wub_ref[...] = wu_ref[...].astype(BF16)

    def compute(words):
        xb = jnp.concatenate(_unpack_bf16_pairs(words), axis=1).astype(BF16)
        gate = jnp.minimum(_dot(xb, wgb_ref[...]) + bg_ref[0], SWIGLU_LIMIT)
        up = jnp.clip(_dot(xb, wub_ref[...]) + bu_ref[0], -SWIGLU_LIMIT, SWIGLU_LIMIT)
        return gate * _sigmoid(SWIGLU_ALPHA * gate) * (up + 1.0)
    _expert_rows_pipeline(j, D_FF // UP_COLS, pl.multiple_of(j * UP_COLS, UP_COLS), blk0_ref, nblk_ref, tail_ref,
                          x_hbm, o_hbm, xbuf, obuf, zbuf, sem_in, sem_out, sem_tail, compute, tail_max)


def _expert_scratch(in_width, in_dtype, out_width, out_dtype):
    return [pltpu.VMEM((2, MOE_BLOCK, in_width), in_dtype),
            pltpu.VMEM((2, MOE_BLOCK, out_width), out_dtype),
            pltpu.VMEM((MOE_BLOCK, out_width), out_dtype)]


_EXPERT_SEMS = [pltpu.SemaphoreType.DMA((2,)), pltpu.SemaphoreType.DMA((2,)), pltpu.SemaphoreType.DMA]


def expert_up(xs, blk0, nblk, tail, tail_max, w_gate_up, b_gate_up):
    cap = xs.shape[0]
    tf = UP_COLS
    nj = D_FF // tf
    b3 = b_gate_up.reshape(N_EXPERTS, 1, 2 * D_FF)
    grid_spec = pltpu.PrefetchScalarGridSpec(
        num_scalar_prefetch=3,
        grid=(nj, N_EXPERTS),
        in_specs=[pl.BlockSpec(memory_space=pl.ANY),
                  pl.BlockSpec((None, D_MODEL, tf), lambda j, e, *_: (e, 0, j)),
                  pl.BlockSpec((None, D_MODEL, tf), lambda j, e, *_: (e, 0, nj + j)),
                  pl.BlockSpec((None, 1, tf), lambda j, e, *_: (e, 0, j)),
                  pl.BlockSpec((None, 1, tf), lambda j, e, *_: (e, 0, nj + j))],
        out_specs=pl.BlockSpec(memory_space=pl.ANY),
        scratch_shapes=_expert_scratch(PACK_W, U32, UP_COLS, BF16)
        + [pltpu.VMEM((D_MODEL, tf), BF16), pltpu.VMEM((D_MODEL, tf), BF16)] + _EXPERT_SEMS)
    return pl.pallas_call(
        functools.partial(_expert_up_kernel, tail_max=tail_max),
        out_shape=jax.ShapeDtypeStruct((cap, D_FF), BF16),
        grid_spec=grid_spec,
        compiler_params=_params("arbitrary", "arbitrary"),
        name="expert_up",
    )(blk0, nblk, tail, xs, w_gate_up, w_gate_up, b3, b3)


def _expert_down_kernel(blk0_ref, nblk_ref, tail_ref, a_hbm, w_ref, b_ref, o_hbm,
                        xbuf, obuf, zbuf, wb_ref, sem_in, sem_out, sem_tail, *, tail_max):
    j, e = pl.program_id(0), pl.program_id(1)

    @pl.when(nblk_ref[e] > 0)
    def _():
        wb_ref[...] = w_ref[...].astype(BF16)

    def compute(a):
        y = _dot(a, wb_ref[...]) + b_ref[0]
        return _pack_bf16_pairs(y[:, :DOWN_COLS // 2], y[:, DOWN_COLS // 2:])
    _expert_rows_pipeline(j, D_MODEL // DOWN_COLS, pl.multiple_of(j * (DOWN_COLS // 2), DOWN_COLS // 2), blk0_ref,
                          nblk_ref, tail_ref, a_hbm, o_hbm, xbuf, obuf, zbuf, sem_in, sem_out, sem_tail, compute, tail_max)


def expert_down(act, blk0, nblk, tail, tail_max, w_down, b_down):
    cap = act.shape[0]
    tn = DOWN_COLS
    grid_spec = pltpu.PrefetchScalarGridSpec(
        num_scalar_prefetch=3,
        grid=(D_MODEL // tn, N_EXPERTS),
        in_specs=[pl.BlockSpec(memory_space=pl.ANY),
                  pl.BlockSpec((None, D_FF, tn), lambda j, e, *_: (e, 0, j)),
                  pl.BlockSpec((None, 1, tn), lambda j, e, *_: (e, 0, j))],
        out_specs=pl.BlockSpec(memory_space=pl.ANY),
        scratch_shapes=_expert_scratch(D_FF, BF16, tn // 2, U32) + [pltpu.VMEM((D_FF, tn), BF16)] + _EXPERT_SEMS)
    return pl.pallas_call(
        functools.partial(_expert_down_kernel, tail_max=tail_max),
        out_shape=jax.ShapeDtypeStruct((cap, PACK_W), U32),
        grid_spec=grid_spec,
        compiler_params=_params("arbitrary", "arbitrary", vmem_limit=EXPERT_DOWN_VMEM),
        name="expert_down",
    )(blk0, nblk, tail, act, w_down, b_down.reshape(N_EXPERTS, 1, D_MODEL))


COMBINE_TILE = 128


def _combine_kernel(dest_ref, yb_hbm, x1_ref, w_ref, mod_ref, o_ref, buf_ref, sem):
    t = COMBINE_TILE

    def start(r, carry):
        for k in range(TOP_K):
            _row_copy(yb_hbm, buf_ref.at[k], sem, dest_ref[0, 0, r * TOP_K + k], r).start()
        return carry
    lax.fori_loop(0, t, start, 0, unroll=4)

    def wait(r, carry):
        for k in range(TOP_K):
            _row_copy(yb_hbm, buf_ref.at[k], sem, 0, r).wait()
        return carry
    lax.fori_loop(0, t, wait, 0, unroll=8)
    half = DOWN_COLS // 2
    for j in range(D_MODEL // DOWN_COLS):
        accs = None
        for k in range(TOP_K):
            w = w_ref[:, TOP_K + k:TOP_K + k + 1]
            parts = [w * p for p in _unpack_bf16_pairs(buf_ref[k, :, j * half:(j + 1) * half])]
            accs = parts if accs is None else [a + p for a, p in zip(accs, parts)]
        for h, acc in enumerate(accs):
            cols = slice(j * DOWN_COLS + h * half, j * DOWN_COLS + (h + 1) * half)
            o_ref[:, cols] = x1_ref[:, cols] + mod_ref[0, 5:6, cols] * acc


def combine(yb, dest, x1, route_out, mod, n_batch, seq):
    rows = n_batch * seq
    t = COMBINE_TILE
    n_tiles = rows // t
    return pl.pallas_call(
        _combine_kernel,
        out_shape=jax.ShapeDtypeStruct((rows, D_MODEL), F32),
        grid=(n_tiles,),
        in_specs=[pl.BlockSpec((1, 1, t * TOP_K), lambda i: (i, 0, 0), memory_space=pltpu.SMEM),
                  pl.BlockSpec(memory_space=pl.ANY),
                  pl.BlockSpec((t, D_MODEL), lambda i: (i, 0)),
                  pl.BlockSpec((t, LANES), lambda i: (i, 0)),
                  pl.BlockSpec((1, 8, D_MODEL), _mod_row_map(n_tiles, seq // t, n_batch))],
        out_specs=pl.BlockSpec((t, D_MODEL), lambda i: (i, 0)),
        scratch_shapes=[pltpu.VMEM((TOP_K, t, PACK_W), U32), pltpu.SemaphoreType.DMA],
        compiler_params=_params("arbitrary"),
        name="moe_combine",
    )(dest.reshape(n_tiles, 1, t * TOP_K), yb, x1, route_out, mod)


def moe_layout(route_out, counts):
    n_tok = route_out.shape[0]
    idx = route_out[:, :TOP_K].astype(jnp.int32)
    rank = route_out[:, 2 * TOP_K:3 * TOP_K].astype(jnp.int32)
    cnt = counts[0, :N_EXPERTS].astype(jnp.int32)
    padded = (cnt + MOE_BLOCK - 1) // MOE_BLOCK * MOE_BLOCK
    pad_end = jnp.cumsum(padded)
    pad_start = pad_end - padded
    dest = (pad_start[idx] + rank).reshape(-1)
    n_blocks = -(-(n_tok * TOP_K + N_EXPERTS * (MOE_BLOCK - 1)) // MOE_BLOCK)
    n_used = pad_end[-1] // MOE_BLOCK
    tail_max = n_blocks - (n_tok * TOP_K) // MOE_BLOCK
    tail = jnp.stack([n_used, n_blocks - n_used]).astype(jnp.int32)
    last_blk = jnp.where(cnt > 0, pad_end // MOE_BLOCK - 1, -1)
    tail_blk = n_used + jnp.arange(tail_max, dtype=jnp.int32)
    fill_blocks = jnp.concatenate([last_blk, jnp.where(tail_blk < n_blocks, tail_blk, -1)]).astype(jnp.int32)
    blk0 = (pad_start // MOE_BLOCK).astype(jnp.int32)
    nblk = (padded // MOE_BLOCK).astype(jnp.int32)
    return dest, blk0, nblk, tail, tail_max, fill_blocks, n_blocks * MOE_BLOCK


def _in_proj_weights(w_in):
    sizes = (XBC_W, D_INNER, N_SSM_HEADS, N_SSM_HEADS, Q_W, KV_W, KV_W, D_MODEL, D_MODEL)
    offs = [0]
    for s in sizes:
        offs.append(offs[-1] + s)
    seg = lambda i: w_in[:, offs[i]:offs[i + 1]]
    w_main = jnp.concatenate([seg(1), seg(0), seg(4), seg(5), seg(6), seg(7), seg(8)], axis=1).astype(BF16)
    w_dt = jnp.concatenate([seg(2), seg(3)], axis=1).astype(BF16)
    return w_main, w_dt


def hybrid_layer(x, ctx, c, c_ctx, w_ada, b_ada, norm1_w, norm2_w, w_in, conv_w, conv_b, dt_bias_f, dt_bias_b,
                 a_log_f, a_log_b, d_skip, ssm_norm_w, q_norm_w, k_norm_w, sink, w_ssm_out, w_attn_out, w_o,
                 router_w, router_b, w_gate_up, b_gate_up, w_down, b_down):
    n_batch, seq, _ = x.shape
    ctx_len = ctx.shape[1]
    n_lat = n_batch * seq
    x2 = x.reshape(n_lat, D_MODEL)
    rows = n_lat + n_batch * ctx_len

    cvec = jnp.zeros((8, D_MODEL), F32).at[:n_batch].set(c).at[n_batch].set(c_ctx)
    mod = ada_modulation(cvec, w_ada, b_ada).reshape(8, 6, D_MODEL)
    mod = jnp.concatenate([mod, jnp.zeros((8, 2, D_MODEL), F32)], axis=1)

    hn = norm_modulate(x2, ctx.reshape(n_batch * ctx_len, D_MODEL), norm1_w, mod, n_batch, seq)
    w_main, w_dt = _in_proj_weights(w_in)
    proj, dt_raw = in_proj(hn, w_main, w_dt)

    xbc = conv_silu(proj, conv_w, conv_b, n_batch, seq, ctx_len)
    dt_bias = jnp.concatenate([dt_bias_f, dt_bias_b]).reshape(1, LANES)
    a_neg = -jnp.exp(jnp.concatenate([a_log_f, a_log_b])).reshape(1, LANES)
    dskip_x = jnp.repeat(d_skip, SSM_HEAD_DIM).reshape(1, D_INNER)
    y_f = ssd_scan(xbc, dt_raw, dt_bias, a_neg, dskip_x, n_batch, seq, ctx_len, rev=False)
    y_b = ssd_scan(xbc, dt_raw, dt_bias, a_neg, dskip_x, n_batch, seq, ctx_len, rev=True)

    tables = _rope_tables(seq)
    qn = qk_prep(proj, q_norm_w, tables, n_lat, n_lat, seq, COL_Q, Q_W, ATTN_SCALE, "q_prep")
    kn = qk_prep(proj, k_norm_w, tables, rows, n_lat, seq, COL_K, KV_W, 1.0, "k_prep")
    y_attn = windowed_attention(qn, kn, proj, sink, n_batch, seq, ctx_len)

    yg = gate_norm(y_f, y_b, proj, ssm_norm_w, n_lat)
    merged = merge_branches(yg, y_attn, w_ssm_out.astype(BF16), w_attn_out.astype(BF16), proj, n_lat)
    x1, h2, logits = out_proj_router(merged, w_o.astype(BF16), x2, mod, norm2_w, router_w, router_b,
                                     n_batch, seq)

    route_out, counts = route(logits)
    dest, blk0, nblk, tail, tail_max, fill_blocks, cap = moe_layout(route_out, counts)
    xs = dispatch_rows(h2, dest, fill_blocks, cap)
    act = expert_up(xs, blk0, nblk, tail, tail_max, w_gate_up, b_gate_up)
    yb = expert_down(act, blk0, nblk, tail, tail_max, w_down, b_down)
    out = combine(yb, dest, x1, route_out, mod, n_batch, seq)
    return out.reshape(n_batch, seq, D_MODEL)


def kernel(x, c, ctx, c_ctx, w_ada, b_ada, norm1_w, norm2_w, w_in, conv_w, conv_b, dt_bias_f, dt_bias_b,
           a_log_f, a_log_b, d_skip, ssm_norm_w, q_norm_w, k_norm_w, sink, w_ssm_out, w_attn_out, w_o,
           router_w, router_b, w_gate_up, b_gate_up, w_down, b_down):
    assert w_ada.shape[0] == 1, "single-layer block"
    return hybrid_layer(x, ctx, c, c_ctx, w_ada[0], b_ada[0], norm1_w[0], norm2_w[0], w_in[0], conv_w[0],
                        conv_b[0], dt_bias_f[0], dt_bias_b[0], a_log_f[0], a_log_b[0], d_skip[0],
                        ssm_norm_w[0], q_norm_w[0], k_norm_w[0], sink[0], w_ssm_out[0], w_attn_out[0], w_o[0],
                        router_w[0], router_b[0], w_gate_up[0], b_gate_up[0], w_down[0], b_down[0])
```

```python
import functools
import math

import jax
import jax.numpy as jnp
from jax import lax
from jax.experimental import pallas as pl
from jax.experimental.pallas import tpu as pltpu

F32 = jnp.float32
BF16 = jnp.bfloat16

D_MODEL = 2048
GRID_W = 64
NORM_EPS = 1e-6
D_INNER = 2 * D_MODEL
SSM_HEAD_DIM = 64
N_SSM_HEADS = D_INNER // SSM_HEAD_DIM
N_GROUPS = 8
HEADS_PER_GROUP = N_SSM_HEADS // N_GROUPS
D_STATE = 128
BC_W = N_GROUPS * D_STATE
XBC_W = D_INNER + 2 * BC_W
CONV_K = 5
SSD_CHUNK = 128
HEAD_DIM = 128
N_Q_HEADS = D_MODEL // HEAD_DIM
N_KV_HEADS = 4
Q_PER_KV = N_Q_HEADS // N_KV_HEADS
Q_W = N_Q_HEADS * HEAD_DIM
KV_W = N_KV_HEADS * HEAD_DIM
WINDOW = 128
ATTN_BLOCK = 128
ATTN_SCALE = HEAD_DIM ** -0.5
ROPE_FREQS = HEAD_DIM // 4
ROPE_BASE = 10000.0
N_EXPERTS = 32
TOP_K = 4
D_FF = D_MODEL
SWIGLU_LIMIT = 7.0
SWIGLU_ALPHA = 1.702
MOE_BLOCK = 512

LOG2_E = math.log2(math.e)
LANES = 128
ROW_TILE = 256
MAIN_W = D_INNER + XBC_W + Q_W + 2 * KV_W + 2 * D_MODEL
COL_Z, COL_XBC, COL_Q = 0, D_INNER, D_INNER + XBC_W
COL_K, COL_V = COL_Q + Q_W, COL_Q + Q_W + KV_W
COL_GS, COL_GA = COL_V + KV_W, COL_V + KV_W + D_MODEL
VMEM_LIMIT = 56 * 1024 * 1024
EXPERT_DOWN_VMEM = 60 * 1024 * 1024


def _params(*sem, vmem_limit=VMEM_LIMIT):
    return pltpu.CompilerParams(dimension_semantics=sem, vmem_limit_bytes=vmem_limit)


def _dot(a, b):
    return jnp.dot(a, b, preferred_element_type=F32)


def _dot_nt(a, b):
    return lax.dot_general(a, b, (((1,), (1,)), ((), ())), preferred_element_type=F32)


def _split3(a):
    a1 = a.astype(BF16)
    r = a - a1.astype(F32)
    a2 = r.astype(BF16)
    a3 = (r - a2.astype(F32)).astype(BF16)
    return a1, a2, a3


def _dot_f32(a, b):
    a1, a2, _ = _split3(a)
    b1, b2, _ = _split3(b)
    return _dot(a1, b1) + (_dot(a1, b2) + _dot(a2, b1))


def _dot_sel_rhs(a, sel):
    a1, a2, a3 = _split3(a)
    return _dot(a1, sel) + (_dot(a2, sel) + _dot(a3, sel))


def _dot_sel_lhs(sel, a):
    a1, a2, a3 = _split3(a)
    return _dot(sel, a1) + (_dot(sel, a2) + _dot(sel, a3))


def _sigmoid(x):
    return 1.0 / (1.0 + jnp.exp(-x))


U32 = jnp.uint32
PACK_W = D_MODEL // 2


def _pack_bf16_pairs(lo, hi):
    lo_bits = lax.bitcast_convert_type(lo.astype(BF16).astype(F32), U32)
    hi_bits = lax.bitcast_convert_type(hi.astype(BF16).astype(F32), U32)
    return (lo_bits >> 16) | (hi_bits & U32(0xFFFF0000))


def _unpack_bf16_pairs(words):
    lo = lax.bitcast_convert_type(words << 16, F32)
    hi = lax.bitcast_convert_type(words & U32(0xFFFF0000), F32)
    return lo, hi


def _ada_kernel(c_ref, w_ref, b_ref, o_ref):
    c = c_ref[...]
    o_ref[...] = _dot_f32(c * _sigmoid(c), w_ref[...]) + b_ref[...]


def ada_modulation(cvec, w_ada, b_ada):
    n = w_ada.shape[1]
    tn = 1024
    return pl.pallas_call(
        _ada_kernel,
        out_shape=jax.ShapeDtypeStruct((8, n), F32),
        grid=(n // tn,),
        in_specs=[pl.BlockSpec((8, D_MODEL), lambda j: (0, 0)),
                  pl.BlockSpec((D_MODEL, tn), lambda j: (0, j)),
                  pl.BlockSpec((1, tn), lambda j: (0, j))],
        out_specs=pl.BlockSpec((8, tn), lambda j: (0, j)),
        compiler_params=_params("arbitrary"),
        name="ada_modulation",
    )(cvec, w_ada, b_ada.reshape(1, n))


def _norm_mod_kernel(x_ref, c_ref, w_ref, mod_ref, o_ref, *, n_lat_tiles):
    def emit(x):
        y = x * lax.rsqrt(jnp.mean(x * x, axis=-1, keepdims=True) + NORM_EPS) * w_ref[...]
        o_ref[...] = (y * (1.0 + mod_ref[0, 1:2, :]) + mod_ref[0, 0:1, :]).astype(o_ref.dtype)

    @pl.when(pl.program_id(0) < n_lat_tiles)
    def _():
        emit(x_ref[...])

    @pl.when(pl.program_id(0) >= n_lat_tiles)
    def _():
        emit(c_ref[...])


def _mod_row_map(n_lat_tiles, tiles_per_batch, n_batch):
    def index_map(i, *_):
        return (jnp.where(i < n_lat_tiles, i // tiles_per_batch, n_batch), 0, 0)
    return index_map


def norm_modulate(x2, ctx2, norm_w, mod, n_batch, seq):
    rows = x2.shape[0] + ctx2.shape[0]
    n_lat_tiles = n_batch * seq // ROW_TILE
    return pl.pallas_call(
        functools.partial(_norm_mod_kernel, n_lat_tiles=n_lat_tiles),
        out_shape=jax.ShapeDtypeStruct((rows, D_MODEL), BF16),
        grid=(rows // ROW_TILE,),
        in_specs=[pl.BlockSpec((ROW_TILE, D_MODEL), lambda i: (jnp.minimum(i, n_lat_tiles - 1), 0)),
                  pl.BlockSpec((ROW_TILE, D_MODEL), lambda i: (jnp.maximum(i - n_lat_tiles, 0), 0)),
                  pl.BlockSpec((1, D_MODEL), lambda i: (0, 0)),
                  pl.BlockSpec((1, 8, D_MODEL), _mod_row_map(n_lat_tiles, seq // ROW_TILE, n_batch))],
        out_specs=pl.BlockSpec((ROW_TILE, D_MODEL), lambda i: (i, 0)),
        compiler_params=_params("arbitrary"),
        name="norm_modulate",
    )(x2, ctx2, norm_w.reshape(1, D_MODEL), mod)


def _largest_tile(n, cap, multiple):
    return max(d for d in range(multiple, cap + 1, multiple) if n % d == 0)


def _in_proj_kernel(a_ref, b_ref, bdt_ref, o_ref, odt_ref):
    a = a_ref[...]
    o_ref[...] = _dot(a, b_ref[...]).astype(o_ref.dtype)

    @pl.when(pl.program_id(1) == 0)
    def _():
        odt_ref[...] = _dot(a, bdt_ref[...])


def in_proj(hn, w_main, w_dt):
    m, k = hn.shape
    n = w_main.shape[1]
    tm = _largest_tile(m, 2112, 16)
    tn = 1024
    return pl.pallas_call(
        _in_proj_kernel,
        out_shape=(jax.ShapeDtypeStruct((m, n), BF16), jax.ShapeDtypeStruct((m, LANES), F32)),
        grid=(m // tm, n // tn),
        in_specs=[pl.BlockSpec((tm, k), lambda i, j: (i, 0)),
                  pl.BlockSpec((k, tn), lambda i, j: (0, j)),
                  pl.BlockSpec((k, LANES), lambda i, j: (0, 0))],
        out_specs=(pl.BlockSpec((tm, tn), lambda i, j: (i, j)),
                   pl.BlockSpec((tm, LANES), lambda i, j: (i, 0))),
        compiler_params=_params("arbitrary", "arbitrary"),
        name="in_proj",
    )(hn, w_main, w_dt)


CONV_COLS = 2048
HALO = 16


def _conv_kernel(prev_ref, cur_ref, next_ref, w_ref, b_ref, o_ref, *,
                 n_lat_tiles, lat_tiles_per_seq, ctx_tiles_per_seq):
    i = pl.program_id(1)
    in_lat = i < n_lat_tiles
    pos = jnp.where(in_lat, i % lat_tiles_per_seq, (i - n_lat_tiles) % ctx_tiles_per_seq)
    per_seq = jnp.where(in_lat, lat_tiles_per_seq, ctx_tiles_per_seq)
    has_prev = pos > 0
    has_next = pos < per_seq - 1
    t = cur_ref.shape[0]
    n = t + 2 * HALO
    full = jnp.concatenate([jnp.where(has_prev, prev_ref[...].astype(F32), 0.0), cur_ref[...].astype(F32),
                            jnp.where(has_next, next_ref[...].astype(F32), 0.0)], axis=0)
    acc = jnp.broadcast_to(b_ref[...], (t, CONV_COLS))
    for k in range(CONV_K):
        shift = CONV_K // 2 - k
        shifted = full if shift == 0 else pltpu.roll(full, shift % n, 0)
        acc = acc + w_ref[k:k + 1, :] * shifted[HALO:HALO + t, :]
    o_ref[...] = acc * _sigmoid(acc)


def conv_silu(proj, conv_w, conv_b, n_batch, seq, ctx_len):
    rows = proj.shape[0]
    n_lat_tiles = n_batch * seq // ROW_TILE
    col0 = COL_XBC // CONV_COLS
    per = ROW_TILE // HALO
    last_halo = rows // HALO - 1
    kern = functools.partial(_conv_kernel, n_lat_tiles=n_lat_tiles,
                             lat_tiles_per_seq=seq // ROW_TILE, ctx_tiles_per_seq=ctx_len // ROW_TILE)
    w8 = jnp.concatenate([conv_w, jnp.zeros((8 - CONV_K, XBC_W), F32)], axis=0)
    return pl.pallas_call(
        kern,
        out_shape=jax.ShapeDtypeStruct((rows, XBC_W), F32),
        grid=(XBC_W // CONV_COLS, rows // ROW_TILE),
        in_specs=[pl.BlockSpec((HALO, CONV_COLS), lambda j, i: (jnp.maximum(i * per - 1, 0), col0 + j)),
                  pl.BlockSpec((ROW_TILE, CONV_COLS), lambda j, i: (i, col0 + j)),
                  pl.BlockSpec((HALO, CONV_COLS), lambda j, i: (jnp.minimum((i + 1) * per, last_halo), col0 + j)),
                  pl.BlockSpec((8, CONV_COLS), lambda j, i: (0, j)),
                  pl.BlockSpec((1, CONV_COLS), lambda j, i: (0, j))],
        out_specs=pl.BlockSpec((ROW_TILE, CONV_COLS), lambda j, i: (i, j)),
        compiler_params=_params("arbitrary", "arbitrary"),
        name="conv_silu",
    )(proj, proj, proj, w8, conv_b.reshape(1, XBC_W))


def _softplus(x):
    return jnp.maximum(x, 0.0) + jnp.log(1.0 + jnp.exp(-jnp.abs(x)))


def _ssd_kernel(xs_ref, b_ref, c_ref, dt_ref, bias_ref, a_ref, dskip_ref, y_ref, h_ref, *, rev):
    @pl.when(pl.program_id(1) == 0)
    def _():
        h_ref[...] = jnp.zeros_like(h_ref)

    L = SSD_CHUNK
    gw = HEADS_PER_GROUP * SSM_HEAD_DIM
    off = N_SSM_HEADS if rev else 0
    row = lax.broadcasted_iota(jnp.int32, (L, L), 0)
    col = lax.broadcasted_iota(jnp.int32, (L, L), 1)
    causal = (col >= row) if rev else (col <= row)
    first_head = col < SSM_HEAD_DIM
    tmat = jnp.where(causal, 1.0, 0.0).astype(BF16)

    dt = _softplus(dt_ref[...] + bias_ref[...])
    a = dt * (a_ref[...] * LOG2_E)
    acum = _dot_sel_lhs(tmat, a)
    acum_t = acum.T
    dt_t = dt.T
    last = 0 if rev else L - 1
    to_end_t = jnp.exp2(acum_t[:, last:last + 1] - acum_t) * dt_t
    src_t = acum_t - jnp.log2(dt_t)

    for g in range(N_GROUPS):
        bg = b_ref[:, g * D_STATE:(g + 1) * D_STATE]
        cgb = c_ref[:, g * D_STATE:(g + 1) * D_STATE].astype(BF16)
        cb = _dot_nt(cgb, bg.astype(BF16))
        bg_t = bg.T
        y_off = _dot(cgb, h_ref[:, g * gw:(g + 1) * gw].astype(BF16))
        for pair in range(HEADS_PER_GROUP // 2):
            c0 = g * gw + pair * LANES
            x_f = xs_ref[:, c0:c0 + LANES]
            x2 = jnp.concatenate([jnp.where(first_head, x_f, 0.0).astype(BF16),
                                  jnp.where(first_head, 0.0, x_f).astype(BF16)], axis=0)
            ms, ws, bcs = [], [], []
            for k in range(2):
                hd = off + g * HEADS_PER_GROUP + 2 * pair + k
                bc = jnp.broadcast_to(acum[:, hd:hd + 1], (L, L))
                decay_dt = jnp.exp2(jnp.where(causal, bc - src_t[hd:hd + 1, :], -jnp.inf))
                ms.append((cb * decay_dt).astype(BF16))
                ws.append((bg_t * to_end_t[hd:hd + 1, :]).astype(BF16))
                bcs.append(bc)
            lhs = jnp.concatenate([jnp.concatenate(ms, axis=1), jnp.concatenate(ws, axis=1)], axis=0)
            res = _dot(lhs, x2)
            e_t = jnp.exp2(jnp.where(first_head, bcs[0], bcs[1]))
            y = res[:L] + y_off[:, pair * LANES:(pair + 1) * LANES] * e_t
            if not rev:
                y = y + dskip_ref[:, c0:c0 + LANES] * x_f
            y_ref[:, c0:c0 + LANES] = y
            h_ref[:, c0:c0 + LANES] = e_t[last:last + 1, :] * h_ref[:, c0:c0 + LANES] + res[L:]


def ssd_scan(xbc, dt_raw, dt_bias, a_neg, dskip_x, n_batch, seq, ctx_len, rev):
    rows = xbc.shape[0]
    nc, ncc = seq // SSD_CHUNK, ctx_len // SSD_CHUNK
    lat_blocks = n_batch * nc

    def blk(b, j):
        cj = (ncc - 1 - j) if rev else j
        lj = (nc - 1 - (j - ncc)) if rev else (j - ncc)
        return jnp.where(j < ncc, lat_blocks + b * ncc + cj, b * nc + lj)

    return pl.pallas_call(
        functools.partial(_ssd_kernel, rev=rev),
        out_shape=jax.ShapeDtypeStruct((rows, D_INNER), F32),
        grid=(n_batch, ncc + nc),
        in_specs=[pl.BlockSpec((SSD_CHUNK, D_INNER), lambda b, j: (blk(b, j), 0)),
                  pl.BlockSpec((SSD_CHUNK, BC_W), lambda b, j: (blk(b, j), D_INNER // BC_W)),
                  pl.BlockSpec((SSD_CHUNK, BC_W), lambda b, j: (blk(b, j), D_INNER // BC_W + 1)),
                  pl.BlockSpec((SSD_CHUNK, LANES), lambda b, j: (blk(b, j), 0)),
                  pl.BlockSpec((1, LANES), lambda b, j: (0, 0)),
                  pl.BlockSpec((1, LANES), lambda b, j: (0, 0)),
                  pl.BlockSpec((1, D_INNER), lambda b, j: (0, 0))],
        out_specs=pl.BlockSpec((SSD_CHUNK, D_INNER), lambda b, j: (blk(b, j), 0)),
        scratch_shapes=[pltpu.VMEM((D_STATE, D_INNER), F32)],
        compiler_params=_params("arbitrary", "arbitrary"),
        name="ssd_scan_bwd" if rev else "ssd_scan_fwd",
    )(xbc, xbc, xbc, dt_raw, dt_bias, a_neg, dskip_x)


def _rope_tables(seq):
    inv_freq = ROPE_BASE ** (-jnp.arange(ROPE_FREQS, dtype=F32) / ROPE_FREQS)
    n_rows = seq // GRID_W
    r = jnp.repeat(jnp.arange(n_rows, dtype=F32), GRID_W)
    c = jnp.tile(jnp.arange(GRID_W, dtype=F32), n_rows)
    ar = r[:, None] * inv_freq
    ac = c[:, None] * inv_freq
    ang = jnp.concatenate([ar, ar, ac, ac], axis=-1)
    cos, sin = jnp.cos(ang), jnp.sin(ang)
    first_half = (jnp.arange(HEAD_DIM) % (2 * ROPE_FREQS)) < ROPE_FREQS
    sin_up = jnp.where(first_half, -sin, 0.0)
    sin_dn = jnp.where(first_half, 0.0, sin)
    return cos, sin_up, sin_dn


def _qk_prep_kernel(x_ref, w_ref, cos_ref, su_ref, sd_ref, o_ref, *, n_heads, scale, n_lat_tiles):
    rotate = pl.program_id(0) < n_lat_tiles
    cos = jnp.where(rotate, cos_ref[...], 1.0)
    su = jnp.where(rotate, su_ref[...], 0.0)
    sd = jnp.where(rotate, sd_ref[...], 0.0)
    w = w_ref[...]
    for h in range(n_heads):
        hs = slice(h * HEAD_DIM, (h + 1) * HEAD_DIM)
        x = x_ref[:, hs].astype(F32)
        n = x * lax.rsqrt(jnp.mean(x * x, axis=-1, keepdims=True) + NORM_EPS) * w
        y = n * cos + pltpu.roll(n, HEAD_DIM - ROPE_FREQS, 1) * su + pltpu.roll(n, ROPE_FREQS, 1) * sd
        if scale != 1.0:
            y = y * scale
        o_ref[:, hs] = y.astype(o_ref.dtype)


def qk_prep(proj, norm_w, tables, rows, n_lat, seq, col, width, scale, name):
    n_heads = width // HEAD_DIM
    tiles_per_seq = seq // ROW_TILE
    tab_spec = pl.BlockSpec((ROW_TILE, HEAD_DIM), lambda i: (i % tiles_per_seq, 0))
    return pl.pallas_call(
        functools.partial(_qk_prep_kernel, n_heads=n_heads, scale=scale, n_lat_tiles=n_lat // ROW_TILE),
        out_shape=jax.ShapeDtypeStruct((rows, width), BF16),
        grid=(rows // ROW_TILE,),
        in_specs=[pl.BlockSpec((ROW_TILE, width), lambda i: (i, col // width)),
                  pl.BlockSpec((1, HEAD_DIM), lambda i: (0, 0)),
                  tab_spec, tab_spec, tab_spec],
        out_specs=pl.BlockSpec((ROW_TILE, width), lambda i: (i, 0)),
        compiler_params=_params("arbitrary"),
        name=name,
    )(proj, norm_w.reshape(1, HEAD_DIM), *tables)


def _attn_kernel(q_ref, kp_ref, kc_ref, kn_ref, vp_ref, vc_ref, vn_ref, kx_ref, vx_ref, sink_ref, o_ref, *,
                 n_blocks):
    i = pl.program_id(1)
    T = ATTN_BLOCK
    nq = Q_PER_KV * T
    nk = 3 * T + kx_ref.shape[0]
    qi = lax.broadcasted_iota(jnp.int32, (nq, nk), 0) % T
    kj = lax.broadcasted_iota(jnp.int32, (nq, nk), 1)
    lo = jnp.maximum(qi, jnp.where(i > 0, 0, T))
    hi = jnp.minimum(qi + 2 * WINDOW, jnp.where(i < n_blocks - 1, 3 * T - 1, 2 * T - 1))
    valid = ((kj >= lo) & (kj <= hi)) | (kj >= 3 * T)
    for h in range(N_KV_HEADS):
        hs = slice(h * HEAD_DIM, (h + 1) * HEAD_DIM)
        q = jnp.concatenate([q_ref[:, (h * Q_PER_KV + g) * HEAD_DIM:(h * Q_PER_KV + g + 1) * HEAD_DIM]
                             for g in range(Q_PER_KV)], axis=0)
        kb = jnp.concatenate([kp_ref[:, hs], kc_ref[:, hs], kn_ref[:, hs], kx_ref[:, hs]], axis=0)
        vb = jnp.concatenate([vp_ref[:, hs], vc_ref[:, hs], vn_ref[:, hs], vx_ref[:, hs]], axis=0)
        s = jnp.where(valid, _dot_nt(q, kb), -jnp.inf)
        sk = jnp.concatenate([jnp.broadcast_to(sink_ref[0:1, h * Q_PER_KV + g:h * Q_PER_KV + g + 1], (T, 1))
                              for g in range(Q_PER_KV)], axis=0)
        mx = jnp.maximum(jnp.max(s, axis=-1, keepdims=True), sk)
        p = jnp.exp(s - mx)
        denom = jnp.sum(p, axis=-1, keepdims=True) + jnp.exp(sk - mx)
        o = _dot(p.astype(BF16), vb) * (1.0 / denom)
        for g in range(Q_PER_KV):
            c0 = (h * Q_PER_KV + g) * HEAD_DIM
            o_ref[:, c0:c0 + HEAD_DIM] = o[g * T:(g + 1) * T, :].astype(o_ref.dtype)


def windowed_attention(qn, kn, proj, sink, n_batch, seq, ctx_len):
    nb = seq // ATTN_BLOCK
    ctx0 = n_batch * seq // ctx_len
    sink_row = jnp.zeros((1, LANES), F32).at[0, :N_Q_HEADS].set(sink)
    v_col = COL_V // KV_W

    def kv_spec(d, col):
        return pl.BlockSpec((ATTN_BLOCK, KV_W), lambda b, i: (b * nb + jnp.clip(i + d, 0, nb - 1), col))

    def ctx_spec(col):
        return pl.BlockSpec((ctx_len, KV_W), lambda b, i: (ctx0 + b, col))
    return pl.pallas_call(
        functools.partial(_attn_kernel, n_blocks=nb),
        out_shape=jax.ShapeDtypeStruct((n_batch * seq, Q_W), BF16),
        grid=(n_batch, nb),
        in_specs=[pl.BlockSpec((ATTN_BLOCK, Q_W), lambda b, i: (b * nb + i, 0)),
                  kv_spec(-1, 0), kv_spec(0, 0), kv_spec(1, 0),
                  kv_spec(-1, v_col), kv_spec(0, v_col), kv_spec(1, v_col),
                  ctx_spec(0), ctx_spec(v_col),
                  pl.BlockSpec((1, LANES), lambda b, i: (0, 0))],
        out_specs=pl.BlockSpec((ATTN_BLOCK, Q_W), lambda b, i: (b * nb + i, 0)),
        compiler_params=_params("arbitrary", "arbitrary"),
        name="windowed_attention",
    )(qn, kn, kn, kn, proj, proj, proj, kn, proj, sink_row)


def _gate_norm_kernel(yf_ref, yb_ref, z_ref, w_ref, o_ref):
    gw = D_INNER // N_GROUPS
    for g in range(N_GROUPS):
        gs = slice(g * gw, (g + 1) * gw)
        z = z_ref[:, gs].astype(F32)
        s = (yf_ref[:, gs] + yb_ref[:, gs]) * (z * _sigmoid(z))
        n = s * lax.rsqrt(jnp.mean(s * s, axis=-1, keepdims=True) + NORM_EPS) * w_ref[:, gs]
        o_ref[:, gs] = n.astype(o_ref.dtype)


def gate_norm(y_f, y_b, proj, ssm_norm_w, rows):
    spec = pl.BlockSpec((ROW_TILE, D_INNER), lambda i: (i, 0))
    return pl.pallas_call(
        _gate_norm_kernel,
        out_shape=jax.ShapeDtypeStruct((rows, D_INNER), BF16),
        grid=(rows // ROW_TILE,),
        in_specs=[spec, spec, pl.BlockSpec((ROW_TILE, D_INNER), lambda i: (i, COL_Z // D_INNER)),
                  pl.BlockSpec((1, D_INNER), lambda i: (0, 0))],
        out_specs=spec,
        compiler_params=_params("arbitrary"),
        name="gate_norm",
    )(y_f, y_b, proj, ssm_norm_w.reshape(1, D_INNER))


def _merge_kernel(yg_ref, ya_ref, ws_ref, wa_ref, gs_ref, ga_ref, o_ref):
    o = (_sigmoid(gs_ref[...].astype(F32)) * _dot(yg_ref[...], ws_ref[...])
         + _sigmoid(ga_ref[...].astype(F32)) * _dot(ya_ref[...], wa_ref[...]))
    o_ref[...] = o.astype(o_ref.dtype)


def merge_branches(yg, ya, w_ssm_out, w_attn_out, proj, rows):
    tm, tn = 512, 1024
    return pl.pallas_call(
        _merge_kernel,
        out_shape=jax.ShapeDtypeStruct((rows, D_MODEL), BF16),
        grid=(D_MODEL // tn, rows // tm),
        in_specs=[pl.BlockSpec((tm, D_INNER), lambda j, i: (i, 0)),
                  pl.BlockSpec((tm, Q_W), lambda j, i: (i, 0)),
                  pl.BlockSpec((D_INNER, tn), lambda j, i: (0, j)),
                  pl.BlockSpec((Q_W, tn), lambda j, i: (0, j)),
                  pl.BlockSpec((tm, tn), lambda j, i: (i, COL_GS // tn + j)),
                  pl.BlockSpec((tm, tn), lambda j, i: (i, COL_GA // tn + j))],
        out_specs=pl.BlockSpec((tm, tn), lambda j, i: (i, j)),
        compiler_params=_params("arbitrary", "arbitrary"),
        name="merge_branches",
    )(yg, ya, w_ssm_out, w_attn_out, proj, proj)


def _out_proj_kernel(m_ref, wo_ref, x_ref, mod_ref, nw_ref, rw_ref, rb_ref, x1_ref, h2_ref, lg_ref):
    x1 = x_ref[...] + mod_ref[0, 2:3, :] * _dot(m_ref[...], wo_ref[...])
    x1_ref[...] = x1
    n = x1 * lax.rsqrt(jnp.mean(x1 * x1, axis=-1, keepdims=True) + NORM_EPS) * nw_ref[...]
    h2 = n * (1.0 + mod_ref[0, 4:5, :]) + mod_ref[0, 3:4, :]
    h2_ref[...] = _pack_bf16_pairs(h2[:, :PACK_W], h2[:, PACK_W:])
    lg_ref[...] = _dot_f32(h2, rw_ref[...]) + rb_ref[...]


def out_proj_router(merged, w_o, xa, mod, norm2_w, router_w, router_b, n_batch, seq):
    rows = n_batch * seq
    tm = 2 * ROW_TILE
    n_tiles = rows // tm
    rw = jnp.zeros((D_MODEL, LANES), F32).at[:, :N_EXPERTS].set(router_w)
    rb = jnp.zeros((1, LANES), F32).at[0, :N_EXPERTS].set(router_b)
    tile = pl.BlockSpec((tm, D_MODEL), lambda i: (i, 0))
    return pl.pallas_call(
        _out_proj_kernel,
        out_shape=(jax.ShapeDtypeStruct((rows, D_MODEL), F32),
                   jax.ShapeDtypeStruct((rows, PACK_W), U32),
                   jax.ShapeDtypeStruct((rows, LANES), F32)),
        grid=(n_tiles,),
        in_specs=[tile,
                  pl.BlockSpec((D_MODEL, D_MODEL), lambda i: (0, 0)),
                  tile,
                  pl.BlockSpec((1, 8, D_MODEL), _mod_row_map(n_tiles, seq // tm, n_batch)),
                  pl.BlockSpec((1, D_MODEL), lambda i: (0, 0)),
                  pl.BlockSpec((D_MODEL, LANES), lambda i: (0, 0)),
                  pl.BlockSpec((1, LANES), lambda i: (0, 0))],
        out_specs=(tile, pl.BlockSpec((tm, PACK_W), lambda i: (i, 0)),
                   pl.BlockSpec((tm, LANES), lambda i: (i, 0))),
        compiler_params=_params("arbitrary"),
        name="out_proj_router",
    )(merged, w_o, xa, mod, norm2_w.reshape(1, D_MODEL), rw, rb)


def _route_kernel(lg_ref, o_ref, cnt_ref, carry_ref):
    @pl.when(pl.program_id(0) == 0)
    def _():
        carry_ref[...] = jnp.zeros_like(carry_ref)

    t = lg_ref.shape[0]
    lane = lax.broadcasted_iota(jnp.int32, (t, LANES), 1).astype(F32)
    work = jnp.where(lane < N_EXPERTS, lg_ref[...], -jnp.inf)
    vals, idxs = [], []
    for _ in range(TOP_K):
        m = jnp.max(work, axis=-1, keepdims=True)
        idx = jnp.min(jnp.where(work == m, lane, float(LANES)), axis=-1, keepdims=True)
        vals.append(m)
        idxs.append(idx)
        work = jnp.where(lane == idx, -jnp.inf, work)
    es = [jnp.exp(v - vals[0]) for v in vals]
    inv = 1.0 / (es[0] + es[1] + es[2] + es[3])
    onehot = jnp.zeros((t, LANES), F32)
    for idx in idxs:
        onehot = onehot + jnp.where(lane == idx, 1.0, 0.0)
    r = lax.broadcasted_iota(jnp.int32, (t, t), 0)
    c = lax.broadcasted_iota(jnp.int32, (t, t), 1)
    before = jnp.where(c < r, 1.0, 0.0).astype(BF16)
    excl = _dot(before, onehot.astype(BF16)) + carry_ref[...]
    out = jnp.zeros((t, LANES), F32)
    for k in range(TOP_K):
        rank = jnp.sum(jnp.where(lane == idxs[k], excl, 0.0), axis=-1, keepdims=True)
        out = jnp.where(lane == k, idxs[k], out)
        out = jnp.where(lane == TOP_K + k, es[k] * inv, out)
        out = jnp.where(lane == 2 * TOP_K + k, rank, out)
    o_ref[...] = out
    carry_ref[...] = carry_ref[...] + jnp.sum(onehot, axis=0, keepdims=True)
    cnt_ref[...] = carry_ref[...]


def route(logits):
    n_tok = logits.shape[0]
    return pl.pallas_call(
        _route_kernel,
        out_shape=(jax.ShapeDtypeStruct((n_tok, LANES), F32), jax.ShapeDtypeStruct((1, LANES), F32)),
        grid=(n_tok // ROW_TILE,),
        in_specs=[pl.BlockSpec((ROW_TILE, LANES), lambda i: (i, 0))],
        out_specs=(pl.BlockSpec((ROW_TILE, LANES), lambda i: (i, 0)),
                   pl.BlockSpec((1, LANES), lambda i: (0, 0))),
        scratch_shapes=[pltpu.VMEM((1, LANES), F32)],
        compiler_params=_params("arbitrary"),
        name="route",
    )(logits)


def _row_copy(src_ref, dst_ref, sem, src_row, dst_row):
    return pltpu.make_async_copy(src_ref.at[pl.ds(src_row, 1)], dst_ref.at[pl.ds(dst_row, 1)], sem)


DISPATCH_TILE = 128


def _dispatch_kernel(fill_ref, dest_ref, h_ref, o_hbm, zero_ref, fill_sem, sem):
    def fill_copy(blk):
        return pltpu.make_async_copy(zero_ref, o_hbm.at[pl.ds(blk * MOE_BLOCK, MOE_BLOCK)], fill_sem)

    @pl.when(pl.program_id(0) == 0)
    def _():
        zero_ref[...] = jnp.zeros_like(zero_ref)
        for s in range(fill_ref.shape[0]):
            @pl.when(fill_ref[s] >= 0)
            def _():
                fill_copy(fill_ref[s]).start()
        for s in range(fill_ref.shape[0]):
            @pl.when(fill_ref[s] >= 0)
            def _():
                fill_copy(0).wait()

    t = h_ref.shape[0]

    def start(r, carry):
        for k in range(TOP_K):
            _row_copy(h_ref, o_hbm, sem, r, dest_ref[0, 0, r * TOP_K + k]).start()
        return carry
    lax.fori_loop(0, t, start, 0, unroll=4)

    def wait(r, carry):
        for k in range(TOP_K):
            _row_copy(h_ref, o_hbm, sem, r, 0).wait()
        return carry
    lax.fori_loop(0, t, wait, 0, unroll=8)


def dispatch_rows(h2, dest, fill_blocks, cap):
    n_tok = h2.shape[0]
    t = DISPATCH_TILE
    n_tiles = n_tok // t
    grid_spec = pltpu.PrefetchScalarGridSpec(
        num_scalar_prefetch=1,
        grid=(n_tiles,),
        in_specs=[pl.BlockSpec((1, 1, t * TOP_K), lambda i, fb: (i, 0, 0), memory_space=pltpu.SMEM),
                  pl.BlockSpec((t, PACK_W), lambda i, fb: (i, 0))],
        out_specs=pl.BlockSpec(memory_space=pl.ANY),
        scratch_shapes=[pltpu.VMEM((MOE_BLOCK, PACK_W), U32), pltpu.SemaphoreType.DMA,
                        pltpu.SemaphoreType.DMA])
    return pl.pallas_call(
        _dispatch_kernel,
        out_shape=jax.ShapeDtypeStruct((cap, PACK_W), U32),
        grid_spec=grid_spec,
        compiler_params=_params("arbitrary"),
        name="moe_dispatch",
    )(fill_blocks, dest.reshape(n_tiles, 1, t * TOP_K), h2)


UP_COLS = 1024
DOWN_COLS = D_MODEL
DMA_CHUNKS = 4


def _expert_rows_pipeline(j, n_col_tiles, col, blk0_ref, nblk_ref, tail_ref, x_hbm, o_hbm, xbuf, obuf, zbuf,
                          sem_in, sem_out, sem_tail, pending, compute, tail_max):
    e = pl.program_id(1)
    n = nblk_ref[e]
    width = obuf.shape[2]
    chunk = MOE_BLOCK // DMA_CHUNKS

    def rows(expert, blk, c):
        return pl.ds(pl.multiple_of((blk0_ref[expert] + blk) * MOE_BLOCK + c * chunk, chunk), chunk)

    def x_copy(expert, blk, slot):
        return [pltpu.make_async_copy(x_hbm.at[rows(expert, blk, c)], xbuf.at[slot, pl.ds(c * chunk, chunk)],
                                      sem_in.at[slot]) for c in range(DMA_CHUNKS)]

    def o_copy(blk, slot):
        return [pltpu.make_async_copy(obuf.at[slot, pl.ds(c * chunk, chunk)],
                                      o_hbm.at[rows(e, blk, c), pl.ds(col, width)], sem_out.at[slot])
                for c in range(DMA_CHUNKS)]

    def start(copies):
        for c in copies:
            c.start()

    def wait(copies):
        for c in copies:
            c.wait()

    def settle(slot):
        @pl.when(pending[slot] == 1)
        def _():
            wait([pltpu.make_async_copy(obuf.at[slot, pl.ds(c * chunk, chunk)],
                                        o_hbm.at[pl.ds(c * chunk, chunk), pl.ds(col, width)], sem_out.at[slot])
                  for c in range(DMA_CHUNKS)])
            pending[slot] = 0

    @pl.when((j == 0) & (e == 0))
    def _():
        pending[0] = 0
        pending[1] = 0

    @pl.when((j == 0) & (e == 0) & (n > 0))
    def _():
        start(x_copy(e, 0, 0))

    @pl.when(n > 0)
    def _():
        def body(blk, carry):
            slot = lax.rem(blk, 2)
            wait(x_copy(e, blk, slot))

            @pl.when(blk + 1 < n)
            def _():
                start(x_copy(e, blk + 1, 1 - slot))

            y = compute(xbuf[slot]).astype(obuf.dtype)
            settle(slot)
            obuf[slot] = y
            start(o_copy(blk, slot))
            pending[slot] = 1
            return carry
        lax.fori_loop(0, n, body, 0)

    e_next = jnp.where(e == N_EXPERTS - 1, 0, e + 1)
    has_next = (e < N_EXPERTS - 1) | (j < n_col_tiles - 1)

    @pl.when(jnp.logical_not(has_next))
    def _():
        settle(0)
        settle(1)

    @pl.when(has_next & (nblk_ref[e_next] > 0))
    def _():
        start(x_copy(e_next, 0, 0))

    @pl.when(e == N_EXPERTS - 1)
    def _():
        zbuf[...] = jnp.zeros_like(zbuf)
        first, count = tail_ref[0], tail_ref[1]

        def z_copy(t):
            dst = pl.ds(pl.multiple_of((first + t) * MOE_BLOCK, MOE_BLOCK), MOE_BLOCK)
            return pltpu.make_async_copy(zbuf, o_hbm.at[dst, pl.ds(col, width)], sem_tail)
        for t in range(tail_max):
            @pl.when(t < count)
            def _():
                z_copy(t).start()
        for t in range(tail_max):
            @pl.when(t < count)
            def _():
                z_copy(t).wait()


def _expert_up_kernel(blk0_ref, nblk_ref, tail_ref, x_hbm, wg_ref, wu_ref, bg_ref, bu_ref, o_hbm,
                      xbuf, obuf, zbuf, wgb_ref, wub_ref, sem_in, sem_out, sem_tail, pending, *, tail_max):
    j, e = pl.program_id(0), pl.program_id(1)

    @pl.when(nblk_ref[e] > 0)
    def _():
        wgb_ref[...] = wg_ref[...].astype(BF16)
        wub_ref[...] = wu_ref[...].astype(BF16)

    def compute(words):
        xb = jnp.concatenate(_unpack_bf16_pairs(words), axis=1).astype(BF16)
        gate = jnp.minimum(_dot(xb, wgb_ref[...]) + bg_ref[0], SWIGLU_LIMIT)
        up = jnp.clip(_dot(xb, wub_ref[...]) + bu_ref[0], -SWIGLU_LIMIT, SWIGLU_LIMIT)
        return gate * _sigmoid(SWIGLU_ALPHA * gate) * (up + 1.0)
    _expert_rows_pipeline(j, D_FF // UP_COLS, pl.multiple_of(j * UP_COLS, UP_COLS), blk0_ref, nblk_ref, tail_ref,
                          x_hbm, o_hbm, xbuf, obuf, zbuf, sem_in, sem_out, sem_tail, pending, compute, tail_max)


def _expert_scratch(in_width, in_dtype, out_width, out_dtype):
    return [pltpu.VMEM((2, MOE_BLOCK, in_width), in_dtype),
            pltpu.VMEM((2, MOE_BLOCK, out_width), out_dtype),
            pltpu.VMEM((MOE_BLOCK, out_width), out_dtype)]


_EXPERT_SEMS = [pltpu.SemaphoreType.DMA((2,)), pltpu.SemaphoreType.DMA((2,)), pltpu.SemaphoreType.DMA,
                pltpu.SMEM((2,), jnp.int32)]


def expert_up(xs, blk0, nblk, tail, tail_max, w_gate_up, b_gate_up):
    cap = xs.shape[0]
    tf = UP_COLS
    nj = D_FF // tf
    b3 = b_gate_up.reshape(N_EXPERTS, 1, 2 * D_FF)
    grid_spec = pltpu.PrefetchScalarGridSpec(
        num_scalar_prefetch=3,
        grid=(nj, N_EXPERTS),
        in_specs=[pl.BlockSpec(memory_space=pl.ANY),
                  pl.BlockSpec((None, D_MODEL, tf), lambda j, e, *_: (e, 0, j)),
                  pl.BlockSpec((None, D_MODEL, tf), lambda j, e, *_: (e, 0, nj + j)),
                  pl.BlockSpec((None, 1, tf), lambda j, e, *_: (e, 0, j)),
                  pl.BlockSpec((None, 1, tf), lambda j, e, *_: (e, 0, nj + j))],
        out_specs=pl.BlockSpec(memory_space=pl.ANY),
        scratch_shapes=_expert_scratch(PACK_W, U32, UP_COLS, BF16)
        + [pltpu.VMEM((D_MODEL, tf), BF16), pltpu.VMEM((D_MODEL, tf), BF16)] + _EXPERT_SEMS)
    return pl.pallas_call(
        functools.partial(_expert_up_kernel, tail_max=tail_max),
        out_shape=jax.ShapeDtypeStruct((cap, D_FF), BF16),
        grid_spec=grid_spec,
        compiler_params=_params("arbitrary", "arbitrary"),
        name="expert_up",
    )(blk0, nblk, tail, xs, w_gate_up, w_gate_up, b3, b3)


def _expert_down_kernel(blk0_ref, nblk_ref, tail_ref, a_hbm, w_ref, b_ref, o_hbm,
                        xbuf, obuf, zbuf, wb_ref, sem_in, sem_out, sem_tail, pending, *, tail_max):
    j, e = pl.program_id(0), pl.program_id(1)

    @pl.when(nblk_ref[e] > 0)
    def _():
        wb_ref[...] = w_ref[...].astype(BF16)

    def compute(a):
        y = _dot(a, wb_ref[...]) + b_ref[0]
        return _pack_bf16_pairs(y[:, :DOWN_COLS // 2], y[:, DOWN_COLS // 2:])
    _expert_rows_pipeline(j, D_MODEL // DOWN_COLS, pl.multiple_of(j * (DOWN_COLS // 2), DOWN_COLS // 2), blk0_ref,
                          nblk_ref, tail_ref, a_hbm, o_hbm, xbuf, obuf, zbuf, sem_in, sem_out, sem_tail, pending, compute,
                          tail_max)


def expert_down(act, blk0, nblk, tail, tail_max, w_down, b_down):
    cap = act.shape[0]
    tn = DOWN_COLS
    grid_spec = pltpu.PrefetchScalarGridSpec(
        num_scalar_prefetch=3,
        grid=(D_MODEL // tn, N_EXPERTS),
        in_specs=[pl.BlockSpec(memory_space=pl.ANY),
                  pl.BlockSpec((None, D_FF, tn), lambda j, e, *_: (e, 0, j)),
                  pl.BlockSpec((None, 1, tn), lambda j, e, *_: (e, 0, j))],
        out_specs=pl.BlockSpec(memory_space=pl.ANY),
        scratch_shapes=_expert_scratch(D_FF, BF16, tn // 2, U32) + [pltpu.VMEM((D_FF, tn), BF16)] + _EXPERT_SEMS)
    return pl.pallas_call(
        functools.partial(_expert_down_kernel, tail_max=tail_max),
        out_shape=jax.ShapeDtypeStruct((cap, PACK_W), U32),
        grid_spec=grid_spec,
        compiler_params=_params("arbitrary", "arbitrary", vmem_limit=EXPERT_DOWN_VMEM),
        name="expert_down",
    )(blk0, nblk, tail, act, w_down, b_down.reshape(N_EXPERTS, 1, D_MODEL))


COMBINE_TILE = 128


def _combine_kernel(dest_ref, yb_hbm, x1_ref, w_ref, mod_ref, o_ref, buf_ref, sem):
    t = COMBINE_TILE

    def start(r, carry):
        for k in range(TOP_K):
            _row_copy(yb_hbm, buf_ref.at[k], sem, dest_ref[0, 0, r * TOP_K + k], r).start()
        return carry
    lax.fori_loop(0, t, start, 0, unroll=4)

    def wait(r, carry):
        for k in range(TOP_K):
            _row_copy(yb_hbm, buf_ref.at[k], sem, 0, r).wait()
        return carry
    lax.fori_loop(0, t, wait, 0, unroll=8)
    half = DOWN_COLS // 2
    for j in range(D_MODEL // DOWN_COLS):
        accs = None
        for k in range(TOP_K):
            w = w_ref[:, TOP_K + k:TOP_K + k + 1]
            parts = [w * p for p in _unpack_bf16_pairs(buf_ref[k, :, j * half:(j + 1) * half])]
            accs = parts if accs is None else [a + p for a, p in zip(accs, parts)]
        for h, acc in enumerate(accs):
            cols = slice(j * DOWN_COLS + h * half, j * DOWN_COLS + (h + 1) * half)
            o_ref[:, cols] = x1_ref[:, cols] + mod_ref[0, 5:6, cols] * acc


def combine(yb, dest, x1, route_out, mod, n_batch, seq):
    rows = n_batch * seq
    t = COMBINE_TILE
    n_tiles = rows // t
    return pl.pallas_call(
        _combine_kernel,
        out_shape=jax.ShapeDtypeStruct((rows, D_MODEL), F32),
        grid=(n_tiles,),
        in_specs=[pl.BlockSpec((1, 1, t * TOP_K), lambda i: (i, 0, 0), memory_space=pltpu.SMEM),
                  pl.BlockSpec(memory_space=pl.ANY),
                  pl.BlockSpec((t, D_MODEL), lambda i: (i, 0)),
                  pl.BlockSpec((t, LANES), lambda i: (i, 0)),
                  pl.BlockSpec((1, 8, D_MODEL), _mod_row_map(n_tiles, seq // t, n_batch))],
        out_specs=pl.BlockSpec((t, D_MODEL), lambda i: (i, 0)),
        scratch_shapes=[pltpu.VMEM((TOP_K, t, PACK_W), U32), pltpu.SemaphoreType.DMA],
        compiler_params=_params("arbitrary"),
        name="moe_combine",
    )(dest.reshape(n_tiles, 1, t * TOP_K), yb, x1, route_out, mod)


def moe_layout(route_out, counts):
    n_tok = route_out.shape[0]
    idx = route_out[:, :TOP_K].astype(jnp.int32)
    rank = route_out[:, 2 * TOP_K:3 * TOP_K].astype(jnp.int32)
    cnt = counts[0, :N_EXPERTS].astype(jnp.int32)
    padded = (cnt + MOE_BLOCK - 1) // MOE_BLOCK * MOE_BLOCK
    pad_end = jnp.cumsum(padded)
    pad_start = pad_end - padded
    dest = (pad_start[idx] + rank).reshape(-1)
    n_blocks = -(-(n_tok * TOP_K + N_EXPERTS * (MOE_BLOCK - 1)) // MOE_BLOCK)
    n_used = pad_end[-1] // MOE_BLOCK
    tail_max = n_blocks - (n_tok * TOP_K) // MOE_BLOCK
    tail = jnp.stack([n_used, n_blocks - n_used]).astype(jnp.int32)
    last_blk = jnp.where(cnt > 0, pad_end // MOE_BLOCK - 1, -1)
    tail_blk = n_used + jnp.arange(tail_max, dtype=jnp.int32)
    fill_blocks = jnp.concatenate([last_blk, jnp.where(tail_blk < n_blocks, tail_blk, -1)]).astype(jnp.int32)
    blk0 = (pad_start // MOE_BLOCK).astype(jnp.int32)
    nblk = (padded // MOE_BLOCK).astype(jnp.int32)
    return dest, blk0, nblk, tail, tail_max, fill_blocks, n_blocks * MOE_BLOCK


def _in_proj_weights(w_in):
    sizes = (XBC_W, D_INNER, N_SSM_HEADS, N_SSM_HEADS, Q_W, KV_W, KV_W, D_MODEL, D_MODEL)
    offs = [0]
    for s in sizes:
        offs.append(offs[-1] + s)
    seg = lambda i: w_in[:, offs[i]:offs[i + 1]]
    w_main = jnp.concatenate([seg(1), seg(0), seg(4), seg(5), seg(6), seg(7), seg(8)], axis=1).astype(BF16)
    w_dt = jnp.concatenate([seg(2), seg(3)], axis=1).astype(BF16)
    return w_main, w_dt


def hybrid_layer(x, ctx, c, c_ctx, w_ada, b_ada, norm1_w, norm2_w, w_in, conv_w, conv_b, dt_bias_f, dt_bias_b,
                 a_log_f, a_log_b, d_skip, ssm_norm_w, q_norm_w, k_norm_w, sink, w_ssm_out, w_attn_out, w_o,
                 router_w, router_b, w_gate_up, b_gate_up, w_down, b_down):
    n_batch, seq, _ = x.shape
    ctx_len = ctx.shape[1]
    n_lat = n_batch * seq
    x2 = x.reshape(n_lat, D_MODEL)
    rows = n_lat + n_batch * ctx_len

    cvec = jnp.zeros((8, D_MODEL), F32).at[:n_batch].set(c).at[n_batch].set(c_ctx)
    mod = ada_modulation(cvec, w_ada, b_ada).reshape(8, 6, D_MODEL)
    mod = jnp.concatenate([mod, jnp.zeros((8, 2, D_MODEL), F32)], axis=1)

    hn = norm_modulate(x2, ctx.reshape(n_batch * ctx_len, D_MODEL), norm1_w, mod, n_batch, seq)
    w_main, w_dt = _in_proj_weights(w_in)
    proj, dt_raw = in_proj(hn, w_main, w_dt)

    xbc = conv_silu(proj, conv_w, conv_b, n_batch, seq, ctx_len)
    dt_bias = jnp.concatenate([dt_bias_f, dt_bias_b]).reshape(1, LANES)
    a_neg = -jnp.exp(jnp.concatenate([a_log_f, a_log_b])).reshape(1, LANES)
    dskip_x = jnp.repeat(d_skip, SSM_HEAD_DIM).reshape(1, D_INNER)
    y_f = ssd_scan(xbc, dt_raw, dt_bias, a_neg, dskip_x, n_batch, seq, ctx_len, rev=False)
    y_b = ssd_scan(xbc, dt_raw, dt_bias, a_neg, dskip_x, n_batch, seq, ctx_len, rev=True)

    tables = _rope_tables(seq)
    qn = qk_prep(proj, q_norm_w, tables, n_lat, n_lat, seq, COL_Q, Q_W, ATTN_SCALE, "q_prep")
    kn = qk_prep(proj, k_norm_w, tables, rows, n_lat, seq, COL_K, KV_W, 1.0, "k_prep")
    y_attn = windowed_attention(qn, kn, proj, sink, n_batch, seq, ctx_len)

    yg = gate_norm(y_f, y_b, proj, ssm_norm_w, n_lat)
    merged = merge_branches(yg, y_attn, w_ssm_out.astype(BF16), w_attn_out.astype(BF16), proj, n_lat)
    x1, h2, logits = out_proj_router(merged, w_o.astype(BF16), x2, mod, norm2_w, router_w, router_b,
                                     n_batch, seq)

    route_out, counts = route(logits)
    dest, blk0, nblk, tail, tail_max, fill_blocks, cap = moe_layout(route_out, counts)
    xs = dispatch_rows(h2, dest, fill_blocks, cap)
    act = expert_up(xs, blk0, nblk, tail, tail_max, w_gate_up, b_gate_up)
    yb = expert_down(act, blk0, nblk, tail, tail_max, w_down, b_down)
    out = combine(yb, dest, x1, route_out, mod, n_batch, seq)
    return out.reshape(n_batch, seq, D_MODEL)


def kernel(x, c, ctx, c_ctx, w_ada, b_ada, norm1_w, norm2_w, w_in, conv_w, conv_b, dt_bias_f, dt_bias_b,
           a_log_f, a_log_b, d_skip, ssm_norm_w, q_norm_w, k_norm_w, sink, w_ssm_out, w_attn_out, w_o,
           router_w, router_b, w_gate_up, b_gate_up, w_down, b_down):
    assert w_ada.shape[0] == 1, "single-layer block"
    return hybrid_layer(x, ctx, c, c_ctx, w_ada[0], b_ada[0], norm1_w[0], norm2_w[0], w_in[0], conv_w[0],
                        conv_b[0], dt_bias_f[0], dt_bias_b[0], a_log_f[0], a_log_b[0], d_skip[0],
                        ssm_norm_w[0], q_norm_w[0], k_norm_w[0], sink[0], w_ssm_out[0], w_attn_out[0], w_o[0],
                        router_w[0], router_b[0], w_gate_up[0], b_gate_up[0], w_down[0], b_down[0])
```
